```python
import math
import jax, jax.numpy as jnp
from jax import lax
import numpy as np

D_MODEL = 2048
BATCH = 4
SEQ = 2048
DEPTH = 4
DEC_BATCH = 8
DEC_SEQ = 8
PAST_LEN = 16384
PAGE_SIZE = 128

N_A = DEPTH // 2
N_B = DEPTH - N_A
CONV_CH = D_MODEL
CONV_W = 3
N_HEADS = 16
N_KV = 4
GRP = N_HEADS // N_KV
HEAD_DIM = D_MODEL // N_HEADS
N_BRANCH = 3
BLK = 64
N_SEL = 16
WINDOW = 512
Q_BLK = 64
ROPE_THETA = 10000.0
EPS = 1e-6
NEG = -1e30
FORCE = 1e3

kernel_name = 'yoco_shortconv_nsa_step'


def rms_norm(x, g):
    xf = x.astype(jnp.float32)
    y = xf * lax.rsqrt(jnp.mean(xf * xf, axis=-1, keepdims=True) + EPS)
    return (y * g.astype(jnp.float32)).astype(x.dtype)


def rope(x, pos):
    half = x.shape[-1] // 2
    inv = ROPE_THETA ** (-jnp.arange(half, dtype=jnp.float32) / half)
    ang = pos.astype(jnp.float32)[:, None] * inv[None, :]
    shp = (pos.shape[0],) + (1,) * (x.ndim - 3) + (half,)
    c = jnp.cos(ang).reshape(shp)
    s = jnp.sin(ang).reshape(shp)
    xf = x.astype(jnp.float32)
    x1, x2 = xf[..., :half], xf[..., half:]
    return jnp.concatenate([x1 * c - x2 * s, x2 * c + x1 * s], axis=-1).astype(x.dtype)


def masked_softmax(s, mask):
    p = jax.nn.softmax(jnp.where(mask, s.astype(jnp.float32), NEG), axis=-1)
    return jnp.where(mask, p, 0.0)


def short_conv_layer(x, prev, g, w_in, w_conv, w_out):
    T = x.shape[1]
    u = rms_norm(x, g) @ w_in
    b_gate, c_gate, h, z = jnp.split(u, 4, axis=-1)
    c = c_gate * h
    cp = jnp.concatenate([prev.astype(c.dtype), c], axis=1)
    conv = w_conv[0] * cp[:, 0:T]
    for j in range(1, CONV_W):
        conv = conv + w_conv[j] * cp[:, j:j + T]
    y = jax.nn.silu(z) * (b_gate * conv)
    return x + y @ w_out, cp[:, T:]


def shared_kv(h, pos, g_kv, w_kv, g_k):
    B, T, _ = h.shape
    kv = (rms_norm(h, g_kv) @ w_kv).reshape(B, T, N_BRANCH, 2, N_KV, HEAD_DIM)
    k = rope(rms_norm(kv[:, :, :, 0], g_k[:, None, :]), pos)
    kv = jnp.stack([k, kv[:, :, :, 1]], axis=3)
    rows = kv[:, :, :2].reshape(B, T, 4, N_KV, HEAD_DIM)
    return rows, kv[:, :, 2]


def compress(rows, w_cmp):
    B, T = rows.shape[:2]
    nb = -(-T // BLK)
    rows = jnp.pad(rows, ((0, 0), (0, nb * BLK - T), (0, 0), (0, 0), (0, 0)))
    blocks = rows.reshape(B, nb, BLK, 2, N_KV, HEAD_DIM)
    kvc = jnp.einsum('bnlcgd,lcg->bncgd', blocks, w_cmp.astype(rows.dtype))
    blk_end = jnp.arange(nb, dtype=jnp.int32) * BLK + (BLK - 1)
    return kvc, blk_end


def nsa_branches(q, qpos, kvc, blk_end, gather_sel, kvw, kwpos):
    scale = HEAD_DIM ** -0.5
    s = jnp.einsum('bqgrd,bngd->bqgrn', q, kvc[:, :, 0]) * scale
    cmask = (blk_end[None, :] <= qpos[:, None])[None, :, None, None, :]
    p = masked_softmax(s, cmask)
    o_cmp = jnp.einsum('bqgrn,bngd->bqgrd', p.astype(q.dtype), kvc[:, :, 1])
    nb = kvc.shape[1]
    cur = (qpos // BLK)[None, :, None, None]
    j = jnp.arange(nb, dtype=jnp.int32)
    forced = ((j == 0) | (j == cur) | (j == cur - 1)).astype(jnp.float32)
    score = jnp.where(j <= cur, p.sum(axis=3) + FORCE * forced, NEG)
    _, idx = lax.top_k(score, min(N_SEL, nb))
    valid = idx <= cur
    ksv = gather_sel(idx)
    kpos = idx[..., None] * BLK + jnp.arange(BLK, dtype=jnp.int32)
    smask = valid[..., None] & (kpos <= qpos[None, :, None, None, None])
    s = jnp.einsum('bqgrd,bqgkld->bqgrkl', q, ksv[..., 0, :]) * scale
    Bq, Q, G, R = s.shape[:4]
    p = masked_softmax(s.reshape(Bq, Q, G, R, -1), smask.reshape(Bq, Q, G, 1, -1)).reshape(s.shape)
    o_sel = jnp.einsum('bqgrkl,bqgkld->bqgrd', p.astype(q.dtype), ksv[..., 1, :])
    s = jnp.einsum('bqgrd,bngd->bqgrn', q, kvw[:, :, 0]) * scale
    dpos = qpos[:, None] - kwpos[None, :]
    wmask = ((dpos >= 0) & (dpos < WINDOW) & (kwpos[None, :] >= 0))[None, :, None, None, :]
    p = masked_softmax(s, wmask)
    o_win = jnp.einsum('bqgrn,bngd->bqgrd', p.astype(q.dtype), kvw[:, :, 1])
    return jnp.stack([o_cmp, o_sel, o_win], axis=2)


def nsa_query(x, pos, g, w_qz, g_q):
    B, T, _ = x.shape
    HD = N_HEADS * HEAD_DIM
    u = rms_norm(x, g) @ w_qz
    q = rope(rms_norm(u[..., :HD].reshape(B, T, N_KV, GRP, HEAD_DIM), g_q), pos)
    z = jax.nn.silu(u[..., HD:4 * HD]).reshape(B, T, N_BRANCH, N_HEADS, HEAD_DIM)
    gate = jax.nn.sigmoid(u[..., 4 * HD:]).reshape(B, T, N_BRANCH, N_HEADS, 1)
    return q, z * gate


def nsa_out(x, o3, zg, w_out):
    B, T = x.shape[:2]
    o = (o3.reshape(B, T, N_BRANCH, N_HEADS, HEAD_DIM) * zg).sum(axis=2)
    return x + o.reshape(B, T, N_HEADS * HEAD_DIM) @ w_out


def prompt_forward(x, g_a, w_in_a, conv_w, w_out_a, g_kv, w_kv, g_k, w_cmp, g_b, w_qz, g_q, w_out_b):
    B, S, _ = x.shape
    pos = jnp.arange(S, dtype=jnp.int32)
    conv_states = []
    for layer in range(DEPTH):
        if layer < N_A:
            prev = jnp.zeros((B, CONV_W - 1, CONV_CH), x.dtype)
            x, st = short_conv_layer(x, prev, g_a[layer], w_in_a[layer], conv_w[layer], w_out_a[layer])
            conv_states.append(st)
            continue
        if layer == N_A:
            rows, win = shared_kv(x, pos, g_kv, w_kv, g_k)
            kvc, blk_end = compress(rows[:, :, 0:2], w_cmp)
            nb = kvc.shape[1]
            sel_blocks = rows[:, :, 2:4].reshape(B, nb, BLK, 2, N_KV, HEAD_DIM)
            win_pad = jnp.pad(win, ((0, 0), (WINDOW, 0), (0, 0), (0, 0), (0, 0)))
            n_qb = S // Q_BLK
            b_ids = jnp.repeat(jnp.arange(B, dtype=jnp.int32), n_qb)
            i_ids = jnp.tile(jnp.arange(n_qb, dtype=jnp.int32), B)
            gidx = jnp.arange(N_KV, dtype=jnp.int32)[None, None, :, None]

            def item(args):
                qi, b, i = args
                qpos = i * Q_BLK + jnp.arange(Q_BLK, dtype=jnp.int32)
                sel_b = sel_blocks[b]
                gather = lambda idx: sel_b[idx, :, :, gidx]
                kvw = lax.dynamic_slice_in_dim(win_pad[b], i * Q_BLK, WINDOW + Q_BLK, axis=0)[None]
                kwpos = i * Q_BLK - WINDOW + jnp.arange(WINDOW + Q_BLK, dtype=jnp.int32)
                return nsa_branches(qi[None], qpos, kvc[b][None], blk_end, gather, kvw, kwpos)[0]
        lb = layer - N_A
        q, zg = nsa_query(x, pos, g_b[lb], w_qz[lb], g_q[lb])
        q_items = q.reshape(B * n_qb, Q_BLK, N_KV, GRP, HEAD_DIM)
        o3 = lax.map(item, (q_items, b_ids, i_ids)).reshape(B, S, N_BRANCH, N_KV, GRP, HEAD_DIM)
        x = nsa_out(x, o3, zg, w_out_b[lb])
    win_state = win[:, S - min(WINDOW, S):]
    return x, jnp.stack(conv_states), rows, win_state


def sample_forward(x, state_conv, cache_kv, state_win, page_table, g_a, w_in_a, conv_w, w_out_a,
                   g_kv, w_kv, g_k, w_cmp, g_b, w_qz, g_q, w_out_b):
    Bd, T, _ = x.shape
    page = cache_kv.shape[1]
    past = page_table.shape[1] * page
    pos = past + jnp.arange(T, dtype=jnp.int32)
    conv_states = []
    for layer in range(DEPTH):
        if layer < N_A:
            x, st = short_conv_layer(x, state_conv[layer], g_a[layer], w_in_a[layer], conv_w[layer], w_out_a[layer])
            conv_states.append(st)
            continue
        if layer == N_A:
            rows, win = shared_kv(x, pos, g_kv, w_kv, g_k)
            past_cmp = cache_kv[page_table, :, 0:2].reshape(Bd, past, 2, N_KV, HEAD_DIM)
            kvc, blk_end = compress(jnp.concatenate([past_cmp.astype(rows.dtype), rows[:, :, 0:2]], axis=1), w_cmp)
            nb_past = past // BLK
            nb_new = kvc.shape[1] - nb_past
            bpp = page // BLK
            pool_blocks = cache_kv.reshape(cache_kv.shape[0], bpp, BLK, 4, N_KV, HEAD_DIM)
            new_sel = jnp.pad(rows[:, :, 2:4], ((0, 0), (0, nb_new * BLK - T), (0, 0), (0, 0), (0, 0)))
            new_sel = new_sel.reshape(Bd, nb_new, BLK, 2, N_KV, HEAD_DIM)
            bidx = jnp.arange(Bd, dtype=jnp.int32)[:, None, None, None]
            gidx = jnp.arange(N_KV, dtype=jnp.int32)[None, None, :, None]

            def gather(idx):
                jp = jnp.minimum(idx, nb_past - 1)
                phys = page_table[bidx, jp // bpp]
                old = pool_blocks[phys, jp % bpp, :, 2:4, gidx].astype(new_sel.dtype)
                new = new_sel[bidx, jnp.clip(idx - nb_past, 0, nb_new - 1), :, :, gidx]
                return jnp.where((idx >= nb_past)[..., None, None, None], new, old)
            kvw = jnp.concatenate([state_win.astype(win.dtype), win], axis=1)
            kwpos = past - state_win.shape[1] + jnp.arange(kvw.shape[1], dtype=jnp.int32)
        lb = layer - N_A
        q, zg = nsa_query(x, pos, g_b[lb], w_qz[lb], g_q[lb])
        o3 = nsa_branches(q, pos, kvc, blk_end, gather, kvw, kwpos)
        x = nsa_out(x, o3, zg, w_out_b[lb])
    win_state = kvw[:, kvw.shape[1] - min(WINDOW, kvw.shape[1]):]
    return x, jnp.stack(conv_states), rows, win_state


def setup_inputs(seed: int = 0) -> dict:
    key = jax.random.key(seed)
    ks = jax.random.split(key, 18)
    n_pages = PAST_LEN // PAGE_SIZE
    n_used = DEC_BATCH * n_pages
    n_pool = n_used + max(1, n_used // 4)
    win_buf = min(WINDOW, PAST_LEN)
    HD = N_HEADS * HEAD_DIM

    def nrm(k, shape, scale=1.0):
        return jax.random.normal(k, shape, jnp.float32) * scale

    return {
        'x_prompt': nrm(ks[0], (BATCH, SEQ, D_MODEL)),
        'x_sample': nrm(ks[1], (DEC_BATCH, DEC_SEQ, D_MODEL)),
        'state_conv': nrm(ks[2], (N_A, DEC_BATCH, CONV_W - 1, CONV_CH)),
        'cache_kv': nrm(ks[3], (n_pool, PAGE_SIZE, 4, N_KV, HEAD_DIM)),
        'state_win': nrm(ks[4], (DEC_BATCH, win_buf, 2, N_KV, HEAD_DIM)),
        'page_table': jax.random.permutation(ks[5], n_pool)[:n_used].reshape(DEC_BATCH, n_pages).astype(jnp.int32),
        'g_a': 1.0 + nrm(ks[6], (N_A, D_MODEL), 0.02),
        'w_in_a': nrm(ks[7], (N_A, D_MODEL, 4 * CONV_CH), D_MODEL ** -0.5),
        'conv_w': nrm(ks[8], (N_A, CONV_W, CONV_CH), CONV_W ** -0.5),
        'w_out_a': nrm(ks[9], (N_A, CONV_CH, D_MODEL), CONV_CH ** -0.5),
        'g_kv': 1.0 + nrm(ks[10], (D_MODEL,), 0.02),
        'w_kv': nrm(ks[11], (D_MODEL, N_BRANCH * 2 * N_KV * HEAD_DIM), D_MODEL ** -0.5),
        'g_k': 1.0 + nrm(ks[12], (N_BRANCH, HEAD_DIM), 0.02),
        'w_cmp': (1.0 + nrm(ks[13], (BLK, 2, N_KV), 0.1)) / BLK,
        'g_b': 1.0 + nrm(ks[14], (N_B, D_MODEL), 0.02),
        'w_qz': nrm(ks[15], (N_B, D_MODEL, 4 * HD + N_BRANCH * N_HEADS), D_MODEL ** -0.5),
        'g_q': 1.0 + nrm(ks[16], (N_B, HEAD_DIM), 0.02),
        'w_out_b': nrm(ks[17], (N_B, HD, D_MODEL), HD ** -0.5),
    }


def reference(x_prompt, x_sample, state_conv, cache_kv, state_win, page_table, g_a, w_in_a, conv_w,
              w_out_a, g_kv, w_kv, g_k, w_cmp, g_b, w_qz, g_q, w_out_b):
    y_prompt, conv_p, rows_p, win_p = prompt_forward(
        x_prompt, g_a, w_in_a, conv_w, w_out_a, g_kv, w_kv, g_k, w_cmp, g_b, w_qz, g_q, w_out_b)
    y_sample, conv_s, rows_s, win_s = sample_forward(
        x_sample, state_conv, cache_kv, state_win, page_table, g_a, w_in_a, conv_w, w_out_a,
        g_kv, w_kv, g_k, w_cmp, g_b, w_qz, g_q, w_out_b)
    return (y_prompt, y_sample, conv_p, conv_s, rows_p, rows_s, win_p, win_s)
```

```python
import functools

import jax
import jax.numpy as jnp
from jax import lax
from jax.experimental import pallas as pl
from jax.experimental.pallas import tpu as pltpu

N_A = 2
N_B = 2
CONV_W = 3
N_HEADS = 16
N_KV = 4
GRP = N_HEADS // N_KV
HEAD_DIM = 128
N_BRANCH = 3
BLK = 64
N_SEL = 16
WINDOW = 512
ROPE_THETA = 10000.0
EPS = 1e-6
NEG = -1e30
FORCE = 1e3

HD = N_HEADS * HEAD_DIM
KVW = N_KV * HEAD_DIM
NEW_PAD = 128
PAGES_PER_STEP = 4
VMEM_LIMIT = 56 * 1024 * 1024

F32 = jnp.float32
BF16 = jnp.bfloat16


def _cparams(n_axes):
    return pltpu.CompilerParams(
        dimension_semantics=("arbitrary",) * n_axes, vmem_limit_bytes=VMEM_LIMIT)


def _dot_nt(a, b):
    return lax.dot_general(a, b, (((1,), (1,)), ((), ())), preferred_element_type=F32)


def _dot(a, b):
    return jnp.dot(a, b, preferred_element_type=F32)


def _pick(n, prefs):
    for p in prefs:
        if n % p == 0:
            return p
    return n


def _rms_cast_kernel(x_ref, g_ref, o_ref):
    x = x_ref[...]
    y = x * lax.rsqrt(jnp.mean(x * x, axis=-1, keepdims=True) + EPS)
    o_ref[...] = (y * g_ref[...]).astype(BF16)


def _rms_cast(x, g):
    n, d = x.shape
    tm = _pick(n, (512, 256, 128, 64, 8))
    return pl.pallas_call(
        _rms_cast_kernel,
        grid=(n // tm,),
        in_specs=[pl.BlockSpec((tm, d), lambda i: (i, 0)),
                  pl.BlockSpec((1, d), lambda i: (0, 0))],
        out_specs=pl.BlockSpec((tm, d), lambda i: (i, 0)),
        out_shape=jax.ShapeDtypeStruct((n, d), BF16),
        compiler_params=_cparams(1),
        name="rms_cast",
    )(x, g.reshape(1, d))


def _mm_kernel(*refs, has_res):
    if has_res:
        x_ref, w_ref, r_ref, o_ref, wb_ref = refs
    else:
        x_ref, w_ref, o_ref, wb_ref = refs

    @pl.when(pl.program_id(1) == 0)
    def _():
        wb_ref[...] = w_ref[...].astype(BF16)

    acc = _dot(x_ref[...], wb_ref[...])
    if has_res:
        acc = r_ref[...] + acc
    o_ref[...] = acc


def _matmul(x, w, layer, m_out, res=None):
    n, k = x.shape
    tm = _pick(n, (1024, 512, 256, 128, 64, 8))
    tn = _pick(m_out, (512, 256, 128))
    if w.ndim == 3:
        w_spec = pl.BlockSpec((None, k, tn), lambda j, i: (layer, 0, j))
    else:
        w_spec = pl.BlockSpec((k, tn), lambda j, i: (0, j))
    in_specs = [pl.BlockSpec((tm, k), lambda j, i: (i, 0)), w_spec]
    args = [x, w]
    if res is not None:
        in_specs.append(pl.BlockSpec((tm, tn), lambda j, i: (i, j)))
        args.append(res)
    return pl.pallas_call(
        functools.partial(_mm_kernel, has_res=res is not None),
        grid=(m_out // tn, n // tm),
        in_specs=in_specs,
        out_specs=pl.BlockSpec((tm, tn), lambda j, i: (i, j)),
        out_shape=jax.ShapeDtypeStruct((n, m_out), F32),
        scratch_shapes=[pltpu.VMEM((k, tn), BF16)],
        compiler_params=_cparams(2),
        name="matmul_res" if res is not None else "matmul",
    )(*args)


def _conv_gate_kernel(b_ref, cg_ref, h_ref, z_ref, prev_ref, w_ref, y_ref, st_ref, carry_ref, *, tps):
    i = pl.program_id(1)
    c = cg_ref[...] * h_ref[...]
    tm = c.shape[0]

    @pl.when(i % tps == 0)
    def _():
        carry_ref[0:2, :] = prev_ref[0]

    p0 = carry_ref[0:1, :]
    p1 = carry_ref[1:2, :]
    row = lax.broadcasted_iota(jnp.int32, c.shape, 0)
    c1 = jnp.where(row == 0, p1, pltpu.roll(c, 1, 0))
    c2 = jnp.where(row == 0, p0, jnp.where(row == 1, p1, pltpu.roll(c, 2, 0)))
    conv = w_ref[0:1, :] * c2 + w_ref[1:2, :] * c1 + w_ref[2:3, :] * c
    y = jax.nn.silu(z_ref[...]) * (b_ref[...] * conv)
    y_ref[...] = y.astype(BF16)
    last = c[tm - 2:tm, :]
    carry_ref[0:2, :] = last
    st_ref[0] = last


def _conv_gate(u, prev, w_conv, seq):
    n = u.shape[0]
    ch = u.shape[1] // 4
    tm = _pick(seq, (256, 128, 64, 8))
    tc = _pick(ch, (512, 256, 128))
    tps = seq // tm
    nc = ch // tc
    nb = n // seq
    return pl.pallas_call(
        functools.partial(_conv_gate_kernel, tps=tps),
        grid=(nc, n // tm),
        in_specs=[pl.BlockSpec((tm, tc), lambda j, i: (i, j)),
                  pl.BlockSpec((tm, tc), lambda j, i: (i, nc + j)),
                  pl.BlockSpec((tm, tc), lambda j, i: (i, 2 * nc + j)),
                  pl.BlockSpec((tm, tc), lambda j, i: (i, 3 * nc + j)),
                  pl.BlockSpec((1, 2, tc), lambda j, i: (i // tps, 0, j)),
                  pl.BlockSpec((CONV_W, tc), lambda j, i: (0, j))],
        out_specs=[pl.BlockSpec((tm, tc), lambda j, i: (i, j)),
                   pl.BlockSpec((1, 2, tc), lambda j, i: (i // tps, 0, j))],
        out_shape=[jax.ShapeDtypeStruct((n, ch), BF16),
                   jax.ShapeDtypeStruct((nb, CONV_W - 1, ch), F32)],
        scratch_shapes=[pltpu.VMEM((8, tc), F32)],
        compiler_params=_cparams(2),
        name="conv_gate",
    )(u, u, u, u, prev, w_conv)


def _norm_rope(x, g, cos2, sin2):
    y = x * lax.rsqrt(jnp.mean(x * x, axis=-1, keepdims=True) + EPS) * g
    return y * cos2 + pltpu.roll(y, HEAD_DIM // 2, 1) * sin2


def _kv_post_kernel(kv_ref, cos_ref, sin_ref, gk_ref, rows_ref, win_ref, selb_ref, winb_ref):
    cos2 = cos_ref[...]
    sin2 = sin_ref[...]
    for br in range(N_BRANCH):
        base = br * 2 * KVW
        for g in range(N_KV):
            c0 = base + g * HEAD_DIM
            k = _norm_rope(kv_ref[:, c0:c0 + HEAD_DIM], gk_ref[br:br + 1, :], cos2, sin2)
            if br < 2:
                rows_ref[:, c0:c0 + HEAD_DIM] = k
            else:
                win_ref[:, c0 - 4 * KVW:c0 - 4 * KVW + HEAD_DIM] = k
                winb_ref[:, c0 - 4 * KVW:c0 - 4 * KVW + HEAD_DIM] = k.astype(BF16)
            if br == 1:
                selb_ref[:, c0 - 2 * KVW:c0 - 2 * KVW + HEAD_DIM] = k.astype(BF16)
        v = kv_ref[:, base + KVW:base + 2 * KVW]
        if br < 2:
            rows_ref[:, base + KVW:base + 2 * KVW] = v
        else:
            win_ref[:, KVW:2 * KVW] = v
            winb_ref[:, KVW:2 * KVW] = v.astype(BF16)
        if br == 1:
            selb_ref[:, KVW:2 * KVW] = v.astype(BF16)


def _kv_post(kv, cos2, sin2, g_k, seq):
    n = kv.shape[0]
    tm = _pick(seq, (256, 128, 64, 8))
    tps = seq // tm
    return pl.pallas_call(
        _kv_post_kernel,
        grid=(n // tm,),
        in_specs=[pl.BlockSpec((tm, 6 * KVW), lambda i: (i, 0)),
                  pl.BlockSpec((tm, HEAD_DIM), lambda i: (i % tps, 0)),
                  pl.BlockSpec((tm, HEAD_DIM), lambda i: (i % tps, 0)),
                  pl.BlockSpec((N_BRANCH, HEAD_DIM), lambda i: (0, 0))],
        out_specs=[pl.BlockSpec((tm, 4 * KVW), lambda i: (i, 0)),
                   pl.BlockSpec((tm, 2 * KVW), lambda i: (i, 0)),
                   pl.BlockSpec((tm, 2 * KVW), lambda i: (i, 0)),
                   pl.BlockSpec((tm, 2 * KVW), lambda i: (i, 0))],
        out_shape=[jax.ShapeDtypeStruct((n, 4 * KVW), F32),
                   jax.ShapeDtypeStruct((n, 2 * KVW), F32),
                   jax.ShapeDtypeStruct((n, 2 * KVW), BF16),
                   jax.ShapeDtypeStruct((n, 2 * KVW), BF16)],
        compiler_params=_cparams(1),
        name="kv_post",
    )(kv, cos2, sin2, g_k)


def _compress_kernel(rows_ref, w_ref, o_ref):
    x = rows_ref[...]
    nb = x.shape[0] // BLK
    x = x.reshape(nb, BLK, x.shape[1]) * w_ref[...][None]
    o_ref[...] = jnp.sum(x, axis=1)


def _compress(rows, w_exp):
    n = rows.shape[0]
    nblk = n // BLK
    per = _pick(nblk, (8,))
    return pl.pallas_call(
        _compress_kernel,
        grid=(nblk // per,),
        in_specs=[pl.BlockSpec((per * BLK, 2 * KVW), lambda i: (i, 0)),
                  pl.BlockSpec((BLK, 2 * KVW), lambda i: (0, 0))],
        out_specs=pl.BlockSpec((per, 2 * KVW), lambda i: (i, 0)),
        out_shape=jax.ShapeDtypeStruct((nblk, 2 * KVW), F32),
        compiler_params=_cparams(1),
        name="compress",
    )(rows, w_exp)


def _compress_pages_kernel(pt_ref, *refs):
    page_refs = refs[:PAGES_PER_STEP]
    w_ref = refs[PAGES_PER_STEP]
    o_ref = refs[PAGES_PER_STEP + 1]
    w = w_ref[...]
    outs = []
    for pr in page_refs:
        x = pr[0]
        nb = x.shape[0] // BLK
        outs.append(jnp.sum(x.reshape(nb, BLK, x.shape[1]) * w[None], axis=1))
    o_ref[0] = jnp.concatenate(outs, axis=0)


def _compress_pages(cache2, page_table, w_exp):
    bd, n_pages = page_table.shape
    page = cache2.shape[1]
    bpp = page // BLK
    steps = n_pages // PAGES_PER_STEP
    rows_out = PAGES_PER_STEP * bpp

    def page_spec(k):
        return pl.BlockSpec((1, page, 2 * KVW),
                            lambda b, c, pt: (pt[b, c * PAGES_PER_STEP + k], 0, 0))

    grid_spec = pltpu.PrefetchScalarGridSpec(
        num_scalar_prefetch=1,
        grid=(bd, steps),
        in_specs=[page_spec(k) for k in range(PAGES_PER_STEP)]
        + [pl.BlockSpec((BLK, 2 * KVW), lambda b, c, pt: (0, 0))],
        out_specs=pl.BlockSpec((1, rows_out, 2 * KVW), lambda b, c, pt: (b, c, 0)),
    )
    return pl.pallas_call(
        _compress_pages_kernel,
        grid_spec=grid_spec,
        out_shape=jax.ShapeDtypeStruct((bd, n_pages * bpp, 2 * KVW), F32),
        compiler_params=_cparams(2),
        name="compress_pages",
    )(page_table, *([cache2] * PAGES_PER_STEP), w_exp)


def _softmax_parts(s, mask):
    sm = jnp.where(mask, s, NEG)
    m = jnp.max(sm, axis=-1, keepdims=True)
    p = jnp.where(mask, jnp.exp(sm - m), 0.0)
    return p, m, jnp.sum(p, axis=-1, keepdims=True)


def _safe_inv(l):
    return jnp.where(l > 0.0, 1.0 / l, 0.0)


def _topk_mask(score, n_f, k):
    sel = jnp.zeros(score.shape, F32)
    big = jnp.float32(score.shape[-1] + 1)
    for _ in range(k):
        mx = jnp.max(score, axis=-1, keepdims=True)
        idx = jnp.min(jnp.where(score == mx, n_f, big), axis=-1, keepdims=True)
        hit = n_f == idx
        sel = jnp.where(hit, 1.0, sel)
        score = jnp.where(hit, -jnp.inf, score)
    return sel


def _select_blocks(psum, cur, n_i):
    forced = ((n_i == 0) | (n_i == cur) | (n_i == cur - 1)).astype(F32)
    valid = n_i <= cur
    score = jnp.where(valid, psum + FORCE * forced, NEG)
    sel = _topk_mask(score, n_i.astype(F32), min(N_SEL, psum.shape[-1]))
    return jnp.where(valid, sel, 0.0)


def _q_heads(uq_ref, g, gq, cos2, sin2):
    hs = []
    for r in range(GRP):
        c0 = (g * GRP + r) * HEAD_DIM
        hs.append(_norm_rope(uq_ref[:, c0:c0 + HEAD_DIM], gq, cos2, sin2))
    return jnp.concatenate(hs, axis=0).astype(BF16)


def _tile_rows(x, reps):
    return jnp.concatenate([x] * reps, axis=0)


def _gate_and_store(o_ref, g, outs, z_refs, gate, t):
    for r in range(GRP):
        h = g * GRP + r
        acc = None
        for br in range(N_BRANCH):
            zg = jax.nn.silu(z_refs[br][:, h * HEAD_DIM:(h + 1) * HEAD_DIM]) \
                * gate[:, br * N_HEADS + h:br * N_HEADS + h + 1]
            term = outs[br][r * t:(r + 1) * t, :] * zg
            acc = term if acc is None else acc + term
        o_ref[:, h * HEAD_DIM:(h + 1) * HEAD_DIM] = acc.astype(o_ref.dtype)


def _prompt_attn_kernel(uq_ref, z0_ref, z1_ref, z2_ref, gate_ref, kvc_ref, sel_ref, win_ref,
                        cos_ref, sin_ref, gq_ref, e_ref, o_ref, *, seq, wlen):
    i = pl.program_id(1)
    t = BLK
    nb = seq // BLK
    scale = HEAD_DIM ** -0.5
    cos2 = cos_ref[...]
    sin2 = sin_ref[...]
    gq = gq_ref[...]
    gate = jax.nn.sigmoid(gate_ref[...])
    qpos = i * t + lax.broadcasted_iota(jnp.int32, (t, 1), 0)
    qpos_r = _tile_rows(qpos, GRP)
    cur = qpos // BLK
    n_i = lax.broadcasted_iota(jnp.int32, (1, nb), 1)
    kpos = lax.broadcasted_iota(jnp.int32, (1, seq), 1)
    wstart = pl.multiple_of(jnp.clip(i * t - WINDOW, 0, seq - wlen), BLK)
    kwpos = wstart + lax.broadcasted_iota(jnp.int32, (1, wlen), 1)
    dpos = qpos_r - kwpos
    wmask = (dpos >= 0) & (dpos < WINDOW)
    cmask = (n_i * BLK + (BLK - 1)) <= qpos_r
    causal = kpos <= qpos_r

    for g in range(N_KV):
        q = _q_heads(uq_ref, g, gq, cos2, sin2)
        ksl = slice(g * HEAD_DIM, (g + 1) * HEAD_DIM)
        vsl = slice(KVW + g * HEAD_DIM, KVW + (g + 1) * HEAD_DIM)

        s = _dot_nt(q, kvc_ref[0, :, ksl].astype(BF16)) * scale
        p, _, l = _softmax_parts(s, cmask)
        p = p * _safe_inv(l)
        o_cmp = _dot(p.astype(BF16), kvc_ref[0, :, vsl].astype(BF16))

        psum = p[0:t]
        for r in range(1, GRP):
            psum = psum + p[r * t:(r + 1) * t]
        sel = _select_blocks(psum, cur, n_i)
        sel_keys = _dot(sel.astype(BF16), e_ref[...]) > 0.5
        smask = _tile_rows(sel_keys, GRP) & causal

        s = _dot_nt(q, sel_ref[0, :, ksl]) * scale
        p, _, l = _softmax_parts(s, smask)
        o_sel = _dot(p.astype(BF16), sel_ref[0, :, vsl]) * _safe_inv(l)

        s = _dot_nt(q, win_ref[0, pl.ds(wstart, wlen), ksl]) * scale
        p, _, l = _softmax_parts(s, wmask)
        o_win = _dot(p.astype(BF16), win_ref[0, pl.ds(wstart, wlen), vsl]) * _safe_inv(l)

        _gate_and_store(o_ref, g, (o_cmp, o_sel, o_win), (z0_ref, z1_ref, z2_ref), gate, t)


def _prompt_attn(u, gate, kvc, selb, winb, cos2, sin2, gq, e_mat, batch, seq):
    n = u.shape[0]
    t = BLK
    nqb = seq // t
    nb = seq // BLK
    wlen = min(WINDOW + t, seq)
    row = lambda b, i: (b * nqb + i, 0)
    return pl.pallas_call(
        functools.partial(_prompt_attn_kernel, seq=seq, wlen=wlen),
        grid=(batch, nqb),
        in_specs=[pl.BlockSpec((t, HD), row),
                  pl.BlockSpec((t, HD), lambda b, i: (b * nqb + i, 1)),
                  pl.BlockSpec((t, HD), lambda b, i: (b * nqb + i, 2)),
                  pl.BlockSpec((t, HD), lambda b, i: (b * nqb + i, 3)),
                  pl.BlockSpec((t, HEAD_DIM), row),
                  pl.BlockSpec((1, nb, 2 * KVW), lambda b, i: (b, 0, 0)),
                  pl.BlockSpec((1, seq, 2 * KVW), lambda b, i: (b, 0, 0)),
                  pl.BlockSpec((1, seq, 2 * KVW), lambda b, i: (b, 0, 0)),
                  pl.BlockSpec((t, HEAD_DIM), lambda b, i: (i, 0)),
                  pl.BlockSpec((t, HEAD_DIM), lambda b, i: (i, 0)),
                  pl.BlockSpec((1, HEAD_DIM), lambda b, i: (0, 0)),
                  pl.BlockSpec((nb, seq), lambda b, i: (0, 0))],
        out_specs=pl.BlockSpec((t, HD), row),
        out_shape=jax.ShapeDtypeStruct((n, HD), BF16),
        compiler_params=_cparams(2),
        name="prompt_attn",
    )(u, u, u, u, gate, kvc.reshape(batch, nb, 2 * KVW), selb.reshape(batch, seq, 2 * KVW),
      winb.reshape(batch, seq, 2 * KVW), cos2, sin2, gq, e_mat)


def _decode_front_kernel(uq_ref, kvc_ref, swin_ref, nwin_ref, cos_ref, sin_ref, gq_ref,
                         qn_ref, sel_ref, ocmp_ref, owin_ref, *, past, t):
    scale = HEAD_DIM ** -0.5
    cos2 = cos_ref[...]
    sin2 = sin_ref[...]
    gq = gq_ref[...]
    nbp = kvc_ref.shape[1]
    wbuf = swin_ref.shape[1]
    qpos = past + lax.broadcasted_iota(jnp.int32, (t, 1), 0)
    qpos_r = _tile_rows(qpos, GRP)
    cur = qpos // BLK
    n_i = lax.broadcasted_iota(jnp.int32, (1, nbp), 1)
    cmask = (n_i * BLK + (BLK - 1)) <= qpos_r
    kwpos_old = (past - wbuf) + lax.broadcasted_iota(jnp.int32, (1, wbuf), 1)
    kwpos_new = past + lax.broadcasted_iota(jnp.int32, (1, NEW_PAD), 1)
    d_old = qpos_r - kwpos_old
    d_new = qpos_r - kwpos_new
    m_old = (d_old >= 0) & (d_old < WINDOW) & (kwpos_old >= 0)
    m_new = (d_new >= 0) & (d_new < WINDOW) & (kwpos_new < past + t)

    for g in range(N_KV):
        q = _q_heads(uq_ref, g, gq, cos2, sin2)
        rows = slice(g * GRP * t, (g + 1) * GRP * t)
        ksl = slice(g * HEAD_DIM, (g + 1) * HEAD_DIM)
        vsl = slice(KVW + g * HEAD_DIM, KVW + (g + 1) * HEAD_DIM)
        qn_ref[0, rows, :] = q

        s = _dot_nt(q, kvc_ref[0, :, ksl].astype(BF16)) * scale
        p, _, l = _softmax_parts(s, cmask)
        p = p * _safe_inv(l)
        ocmp_ref[0, rows, :] = _dot(p.astype(BF16), kvc_ref[0, :, vsl].astype(BF16))

        psum = p[0:t]
        for r in range(1, GRP):
            psum = psum + p[r * t:(r + 1) * t]
        sel_ref[0, g * t:(g + 1) * t, :] = _select_blocks(psum, cur, n_i)

        s_old = _dot_nt(q, swin_ref[0, :, ksl].astype(BF16)) * scale
        s_new = _dot_nt(q, nwin_ref[0, :, ksl].astype(BF16)) * scale
        mx = jnp.maximum(jnp.max(jnp.where(m_old, s_old, NEG), axis=-1, keepdims=True),
                         jnp.max(jnp.where(m_new, s_new, NEG), axis=-1, keepdims=True))
        p_old = jnp.where(m_old, jnp.exp(jnp.where(m_old, s_old, NEG) - mx), 0.0)
        p_new = jnp.where(m_new, jnp.exp(jnp.where(m_new, s_new, NEG) - mx), 0.0)
        l = jnp.sum(p_old, axis=-1, keepdims=True) + jnp.sum(p_new, axis=-1, keepdims=True)
        o = _dot(p_old.astype(BF16), swin_ref[0, :, vsl].astype(BF16)) \
            + _dot(p_new.astype(BF16), nwin_ref[0, :, vsl].astype(BF16))
        owin_ref[0, rows, :] = o * _safe_inv(l)


def _decode_front(u, kvc_all, state_win2, new_win, cos2, sin2, gq, bd, t, past):
    nbp = kvc_all.shape[1]
    wbuf = state_win2.shape[1]
    rows = N_HEADS * t
    return pl.pallas_call(
        functools.partial(_decode_front_kernel, past=past, t=t),
        grid=(bd,),
        in_specs=[pl.BlockSpec((t, HD), lambda b: (b, 0)),
                  pl.BlockSpec((1, nbp, 2 * KVW), lambda b: (b, 0, 0)),
                  pl.BlockSpec((1, wbuf, 2 * KVW), lambda b: (b, 0, 0)),
                  pl.BlockSpec((1, NEW_PAD, 2 * KVW), lambda b: (b, 0, 0)),
                  pl.BlockSpec((t, HEAD_DIM), lambda b: (0, 0)),
                  pl.BlockSpec((t, HEAD_DIM), lambda b: (0, 0)),
                  pl.BlockSpec((1, HEAD_DIM), lambda b: (0, 0))],
        out_specs=[pl.BlockSpec((1, rows, HEAD_DIM), lambda b: (b, 0, 0)),
                   pl.BlockSpec((1, N_KV * t, nbp), lambda b: (b, 0, 0)),
                   pl.BlockSpec((1, rows, HEAD_DIM), lambda b: (b, 0, 0)),
                   pl.BlockSpec((1, rows, HEAD_DIM), lambda b: (b, 0, 0))],
        out_shape=[jax.ShapeDtypeStruct((bd, rows, HEAD_DIM), BF16),
                   jax.ShapeDtypeStruct((bd, N_KV * t, nbp), F32),
                   jax.ShapeDtypeStruct((bd, rows, HEAD_DIM), F32),
                   jax.ShapeDtypeStruct((bd, rows, HEAD_DIM), F32)],
        compiler_params=_cparams(1),
        name="decode_front",
    )(u, kvc_all, state_win2, new_win, cos2, sin2, gq)


def _decode_sel_kernel(pt_ref, *refs, t, n_steps):
    del pt_ref
    np_ = PAGES_PER_STEP
    (qn_ref, selc_ref, seln_ref) = refs[:3]
    page_refs = refs[3:3 + np_]
    (new_ref, ocmp_ref, owin_ref, z0_ref, z1_ref, z2_ref, gate_ref, e_ref,
     o_ref, m_ref, l_ref, acc_ref) = refs[3 + np_:]
    c = pl.program_id(1)
    scale = HEAD_DIM ** -0.5
    gt = GRP * t

    @pl.when(c == 0)
    def _():
        m_ref[...] = jnp.full(m_ref.shape, NEG, F32)
        l_ref[...] = jnp.zeros(l_ref.shape, F32)
        acc_ref[...] = jnp.zeros(acc_ref.shape, F32)

    def update(g, s, mask, v):
        rows = slice(g * gt, (g + 1) * gt)
        sm = jnp.where(mask, s, NEG)
        m_old = m_ref[rows, :]
        m_new = jnp.maximum(m_old, jnp.max(sm, axis=-1, keepdims=True))
        alpha = jnp.exp(m_old - m_new)
        p = jnp.where(mask, jnp.exp(sm - m_new), 0.0)
        l_ref[rows, :] = alpha * l_ref[rows, :] + jnp.sum(p, axis=-1, keepdims=True)
        acc_ref[rows, :] = alpha * acc_ref[rows, :] + _dot(p.astype(BF16), v)
        m_ref[rows, :] = m_new

    for g in range(N_KV):
        q = qn_ref[0, g * gt:(g + 1) * gt, :]
        ksl = slice(g * HEAD_DIM, (g + 1) * HEAD_DIM)
        vsl = slice(KVW + g * HEAD_DIM, KVW + (g + 1) * HEAD_DIM)
        k = jnp.concatenate([pr[0, :, ksl] for pr in page_refs], axis=0).astype(BF16)
        v = jnp.concatenate([pr[0, :, vsl] for pr in page_refs], axis=0).astype(BF16)
        s = _dot_nt(q, k) * scale
        selk = _dot(selc_ref[0, 0, g * t:(g + 1) * t, :].astype(BF16), e_ref[...]) > 0.5
        update(g, s, _tile_rows(selk, GRP), v)

    @pl.when(c == n_steps - 1)
    def _():
        gate = jax.nn.sigmoid(gate_ref[...])
        tq = _tile_rows(lax.broadcasted_iota(jnp.int32, (t, 1), 0), GRP)
        lk = lax.broadcasted_iota(jnp.int32, (1, NEW_PAD), 1)
        causal = lk <= tq
        for g in range(N_KV):
            rows = slice(g * gt, (g + 1) * gt)
            q = qn_ref[0, rows, :]
            ksl = slice(g * HEAD_DIM, (g + 1) * HEAD_DIM)
            vsl = slice(KVW + g * HEAD_DIM, KVW + (g + 1) * HEAD_DIM)
            s = _dot_nt(q, new_ref[0, :, ksl].astype(BF16)) * scale
            seln = _tile_rows(seln_ref[0, 0, g * t:(g + 1) * t, 0:1], GRP) > 0.5
            update(g, s, seln & causal, new_ref[0, :, vsl].astype(BF16))
            o_sel = acc_ref[rows, :] * _safe_inv(l_ref[rows, :])
            _gate_and_store(o_ref, g, (ocmp_ref[0, rows, :], o_sel, owin_ref[0, rows, :]),
                            (z0_ref, z1_ref, z2_ref), gate, t)


def _decode_sel(page_table, qn, selc, cache2, new_sel, ocmp, owin, u, gate, e8, bd, t):
    n_pages = page_table.shape[1]
    page = cache2.shape[1]
    n_steps = n_pages // PAGES_PER_STEP
    bps = PAGES_PER_STEP * (page // BLK)
    rows = N_HEADS * t

    def page_spec(k):
        return pl.BlockSpec((1, page, 2 * KVW),
                            lambda b, c, pt: (pt[b, c * PAGES_PER_STEP + k], 0, 1))

    grid_spec = pltpu.PrefetchScalarGridSpec(
        num_scalar_prefetch=1,
        grid=(bd, n_steps),
        in_specs=[pl.BlockSpec((1, rows, HEAD_DIM), lambda b, c, pt: (b, 0, 0)),
                  pl.BlockSpec((1, 1, N_KV * t, bps), lambda b, c, pt: (b, c, 0, 0)),
                  pl.BlockSpec((1, 1, N_KV * t, bps), lambda b, c, pt: (b, n_steps, 0, 0))]
        + [page_spec(k) for k in range(PAGES_PER_STEP)]
        + [pl.BlockSpec((1, NEW_PAD, 2 * KVW), lambda b, c, pt: (b, 0, 0)),
           pl.BlockSpec((1, rows, HEAD_DIM), lambda b, c, pt: (b, 0, 0)),
           pl.BlockSpec((1, rows, HEAD_DIM), lambda b, c, pt: (b, 0, 0)),
           pl.BlockSpec((t, HD), lambda b, c, pt: (b, 1)),
           pl.BlockSpec((t, HD), lambda b, c, pt: (b, 2)),
           pl.BlockSpec((t, HD), lambda b, c, pt: (b, 3)),
           pl.BlockSpec((t, HEAD_DIM), lambda b, c, pt: (b, 0)),
           pl.BlockSpec((bps, bps * BLK), lambda b, c, pt: (0, 0))],
        out_specs=pl.BlockSpec((t, HD), lambda b, c, pt: (b, 0)),
        scratch_shapes=[pltpu.VMEM((rows, 1), F32), pltpu.VMEM((rows, 1), F32),
                        pltpu.VMEM((rows, HEAD_DIM), F32)],
    )
    return pl.pallas_call(
        functools.partial(_decode_sel_kernel, t=t, n_steps=n_steps),
        grid_spec=grid_spec,
        out_shape=jax.ShapeDtypeStruct((bd * t, HD), BF16),
        compiler_params=_cparams(2),
        name="decode_sel",
    )(page_table, qn, selc, selc, *([cache2] * PAGES_PER_STEP), new_sel, ocmp, owin,
      u, u, u, gate, e8)


def _rope_tables(pos):
    half = HEAD_DIM // 2
    inv = ROPE_THETA ** (-jnp.arange(half, dtype=F32) / half)
    ang = pos.astype(F32)[:, None] * inv[None, :]
    c = jnp.cos(ang)
    s = jnp.sin(ang)
    return jnp.concatenate([c, c], axis=-1), jnp.concatenate([-s, s], axis=-1)


def _block_expand(nblocks):
    key_blk = jnp.arange(nblocks * BLK, dtype=jnp.int32) // BLK
    return (key_blk[None, :] == jnp.arange(nblocks, dtype=jnp.int32)[:, None]).astype(BF16)


def _conv_layers(x, prev, seq, g_a, w_in_a, conv_w, w_out_a):
    states = []
    ch = conv_w.shape[-1]
    for layer in range(N_A):
        xn = _rms_cast(x, g_a[layer])
        u = _matmul(xn, w_in_a, layer, 4 * ch)
        y, st = _conv_gate(u, prev[layer], conv_w[layer], seq)
        x = _matmul(y, w_out_a, layer, x.shape[1], res=x)
        states.append(st)
    return x, jnp.stack(states)


def _shared_kv(x, pos, seq, g_kv, w_kv, g_k):
    cos2, sin2 = _rope_tables(pos)
    kv = _matmul(_rms_cast(x, g_kv), w_kv, 0, 6 * KVW)
    rows, win, selb, winb = _kv_post(kv, cos2, sin2, g_k, seq)
    return rows, win, selb, winb, cos2, sin2


def _query_side(x, lb, g_b, w_qz, w_gate):
    xn = _rms_cast(x, g_b[lb])
    u = _matmul(xn, w_qz, lb, 4 * HD)
    gate = _matmul(xn, w_gate[lb], 0, HEAD_DIM)
    return u, gate


def kernel(x_prompt, x_sample, state_conv, cache_kv, state_win, page_table, g_a, w_in_a, conv_w,
           w_out_a, g_kv, w_kv, g_k, w_cmp, g_b, w_qz, g_q, w_out_b):
    batch, seq, d = x_prompt.shape
    bd, t, _ = x_sample.shape
    ch = conv_w.shape[-1]
    n_pool, page = cache_kv.shape[:2]
    n_pages = page_table.shape[1]
    past = n_pages * page
    wbuf = state_win.shape[1]
    assert seq % BLK == 0 and page % BLK == 0 and t <= BLK and n_pages % PAGES_PER_STEP == 0

    w_exp = jnp.repeat(w_cmp.reshape(BLK, 2 * N_KV), HEAD_DIM, axis=1)
    n_gate = N_BRANCH * N_HEADS
    w_gate = jnp.pad(w_qz[:, :, 4 * HD:4 * HD + n_gate], ((0, 0), (0, 0), (0, HEAD_DIM - n_gate)))

    xp = x_prompt.reshape(batch * seq, d)
    xp, conv_p = _conv_layers(xp, jnp.zeros((N_A, batch, CONV_W - 1, ch), F32), seq,
                              g_a, w_in_a, conv_w, w_out_a)
    pos_p = jnp.arange(seq, dtype=jnp.int32)
    rows_p, win_p, selb_p, winb_p, cos_p, sin_p = _shared_kv(xp, pos_p, seq, g_kv, w_kv, g_k)
    kvc_p = _compress(rows_p, w_exp)
    e_p = _block_expand(seq // BLK)
    for lb in range(N_B):
        u, gate = _query_side(xp, lb, g_b, w_qz, w_gate)
        o = _prompt_attn(u, gate, kvc_p, selb_p, winb_p, cos_p, sin_p, g_q[lb:lb + 1],
                         e_p, batch, seq)
        xp = _matmul(o, w_out_b, lb, d, res=xp)
    wn = min(WINDOW, seq)
    y_prompt = xp.reshape(batch, seq, d)
    kv_rows_prompt = rows_p.reshape(batch, seq, 4, N_KV, HEAD_DIM)
    win_prompt = win_p.reshape(batch, seq, 2, N_KV, HEAD_DIM)[:, seq - wn:]

    xs = x_sample.reshape(bd * t, d)
    xs, conv_s = _conv_layers(xs, state_conv, t, g_a, w_in_a, conv_w, w_out_a)
    pos_s = past + jnp.arange(t, dtype=jnp.int32)
    rows_s, win_s, _, _, cos_s, sin_s = _shared_kv(xs, pos_s, t, g_kv, w_kv, g_k)
    cache2 = cache_kv.reshape(n_pool, page, 4 * KVW)
    kvc_past = _compress_pages(cache2, page_table, w_exp)
    rows_s3 = rows_s.reshape(bd, t, 4 * KVW)
    rows_pad = jnp.pad(rows_s3, ((0, 0), (0, NEW_PAD - t), (0, 0)))
    kvc_new = _compress(rows_pad[:, :BLK, :2 * KVW].reshape(bd * BLK, 2 * KVW), w_exp)
    nb_past = past // BLK
    bps = PAGES_PER_STEP * (page // BLK)
    nbp = nb_past + bps
    kvc_all = jnp.concatenate(
        [kvc_past, kvc_new[:, None, :], jnp.zeros((bd, bps - 1, 2 * KVW), F32)], axis=1)
    new_sel = rows_pad[:, :, 2 * KVW:]
    new_win = jnp.pad(win_s.reshape(bd, t, 2 * KVW), ((0, 0), (0, NEW_PAD - t), (0, 0)))
    state_win2 = state_win.reshape(bd, wbuf, 2 * KVW)
    e8 = _block_expand(bps)
    for lb in range(N_B):
        u, gate = _query_side(xs, lb, g_b, w_qz, w_gate)
        qn, sel, ocmp, owin = _decode_front(u, kvc_all, state_win2, new_win, cos_s, sin_s,
                                            g_q[lb:lb + 1], bd, t, past)
        selc = sel.reshape(bd, N_KV * t, nbp // bps, bps).transpose(0, 2, 1, 3)
        o = _decode_sel(page_table, qn, selc, cache2, new_sel, ocmp, owin, u, gate, e8, bd, t)
        xs = _matmul(o, w_out_b, lb, d, res=xs)
    y_sample = xs.reshape(bd, t, d)
    kv_rows_sample = rows_s.reshape(bd, t, 4, N_KV, HEAD_DIM)
    kvw = jnp.concatenate([state_win, win_s.reshape(bd, t, 2, N_KV, HEAD_DIM)], axis=1)
    win_sample = kvw[:, kvw.shape[1] - min(WINDOW, kvw.shape[1]):]

    return (y_prompt, y_sample, conv_p, conv_s, kv_rows_prompt, kv_rows_sample, win_prompt, win_sample)
```

```python
import functools

import jax
import jax.numpy as jnp
from jax import lax
from jax.experimental import pallas as pl
from jax.experimental.pallas import tpu as pltpu

N_A = 2
N_B = 2
CONV_W = 3
N_HEADS = 16
N_KV = 4
GRP = N_HEADS // N_KV
HEAD_DIM = 128
N_BRANCH = 3
BLK = 64
N_SEL = 16
WINDOW = 512
ROPE_THETA = 10000.0
EPS = 1e-6
NEG = -1e30
FORCE = 1e3

HD = N_HEADS * HEAD_DIM
KVW = N_KV * HEAD_DIM
NEW_PAD = 128
KCH = 128
PAGES_PER_STEP = 4
VMEM_LIMIT = 56 * 1024 * 1024

F32 = jnp.float32
BF16 = jnp.bfloat16


def _cparams(n_axes):
    return pltpu.CompilerParams(
        dimension_semantics=("arbitrary",) * n_axes, vmem_limit_bytes=VMEM_LIMIT)


def _dot_nt(a, b):
    return lax.dot_general(a, b, (((1,), (1,)), ((), ())), preferred_element_type=F32)


def _dot(a, b):
    return jnp.dot(a, b, preferred_element_type=F32)


def _pick(n, prefs):
    for p in prefs:
        if n % p == 0:
            return p
    return n


def _rms_cast_kernel(x_ref, g_ref, o_ref):
    x = x_ref[...]
    y = x * lax.rsqrt(jnp.mean(x * x, axis=-1, keepdims=True) + EPS)
    o_ref[...] = (y * g_ref[...]).astype(BF16)


def _rms_cast(x, g):
    n, d = x.shape
    tm = _pick(n, (512, 256, 128, 64, 8))
    return pl.pallas_call(
        _rms_cast_kernel,
        grid=(n // tm,),
        in_specs=[pl.BlockSpec((tm, d), lambda i: (i, 0)),
                  pl.BlockSpec((1, d), lambda i: (0, 0))],
        out_specs=pl.BlockSpec((tm, d), lambda i: (i, 0)),
        out_shape=jax.ShapeDtypeStruct((n, d), BF16),
        compiler_params=_cparams(1),
        name="rms_cast",
    )(x, g.reshape(1, d))


def _mm_kernel(*refs, has_res):
    if has_res:
        x_ref, w_ref, r_ref, o_ref, wb_ref = refs
    else:
        x_ref, w_ref, o_ref, wb_ref = refs

    @pl.when(pl.program_id(1) == 0)
    def _():
        wb_ref[...] = w_ref[...].astype(BF16)

    acc = _dot(x_ref[...], wb_ref[...])
    if has_res:
        acc = r_ref[...] + acc
    o_ref[...] = acc


def _matmul(x, w, layer, m_out, res=None):
    n, k = x.shape
    tm = _pick(n, (1024, 512, 256, 128, 64, 8))
    tn = _pick(m_out, (512, 256, 128))
    if w.ndim == 3:
        w_spec = pl.BlockSpec((None, k, tn), lambda j, i: (layer, 0, j))
    else:
        w_spec = pl.BlockSpec((k, tn), lambda j, i: (0, j))
    in_specs = [pl.BlockSpec((tm, k), lambda j, i: (i, 0)), w_spec]
    args = [x, w]
    if res is not None:
        in_specs.append(pl.BlockSpec((tm, tn), lambda j, i: (i, j)))
        args.append(res)
    return pl.pallas_call(
        functools.partial(_mm_kernel, has_res=res is not None),
        grid=(m_out // tn, n // tm),
        in_specs=in_specs,
        out_specs=pl.BlockSpec((tm, tn), lambda j, i: (i, j)),
        out_shape=jax.ShapeDtypeStruct((n, m_out), F32),
        scratch_shapes=[pltpu.VMEM((k, tn), BF16)],
        compiler_params=_cparams(2),
        name="matmul_res" if res is not None else "matmul",
    )(*args)


def _conv_gate_kernel(b_ref, cg_ref, h_ref, z_ref, prev_ref, w_ref, y_ref, st_ref, carry_ref, *, tps):
    i = pl.program_id(1)
    c = cg_ref[...] * h_ref[...]
    tm = c.shape[0]

    @pl.when(i % tps == 0)
    def _():
        carry_ref[0:2, :] = prev_ref[0]

    p0 = carry_ref[0:1, :]
    p1 = carry_ref[1:2, :]
    row = lax.broadcasted_iota(jnp.int32, c.shape, 0)
    c1 = jnp.where(row == 0, p1, pltpu.roll(c, 1, 0))
    c2 = jnp.where(row == 0, p0, jnp.where(row == 1, p1, pltpu.roll(c, 2, 0)))
    conv = w_ref[0:1, :] * c2 + w_ref[1:2, :] * c1 + w_ref[2:3, :] * c
    y = jax.nn.silu(z_ref[...]) * (b_ref[...] * conv)
    y_ref[...] = y.astype(BF16)
    last = c[tm - 2:tm, :]
    carry_ref[0:2, :] = last
    st_ref[0] = last


def _conv_gate(u, prev, w_conv, seq):
    n = u.shape[0]
    ch = u.shape[1] // 4
    tm = _pick(seq, (256, 128, 64, 8))
    tc = _pick(ch, (512, 256, 128))
    tps = seq // tm
    nc = ch // tc
    nb = n // seq
    return pl.pallas_call(
        functools.partial(_conv_gate_kernel, tps=tps),
        grid=(nc, n // tm),
        in_specs=[pl.BlockSpec((tm, tc), lambda j, i: (i, j)),
                  pl.BlockSpec((tm, tc), lambda j, i: (i, nc + j)),
                  pl.BlockSpec((tm, tc), lambda j, i: (i, 2 * nc + j)),
                  pl.BlockSpec((tm, tc), lambda j, i: (i, 3 * nc + j)),
                  pl.BlockSpec((1, 2, tc), lambda j, i: (i // tps, 0, j)),
                  pl.BlockSpec((CONV_W, tc), lambda j, i: (0, j))],
        out_specs=[pl.BlockSpec((tm, tc), lambda j, i: (i, j)),
                   pl.BlockSpec((1, 2, tc), lambda j, i: (i // tps, 0, j))],
        out_shape=[jax.ShapeDtypeStruct((n, ch), BF16),
                   jax.ShapeDtypeStruct((nb, CONV_W - 1, ch), F32)],
        scratch_shapes=[pltpu.VMEM((8, tc), F32)],
        compiler_params=_cparams(2),
        name="conv_gate",
    )(u, u, u, u, prev, w_conv)


def _norm_rope(x, g, cos2, sin2):
    y = x * lax.rsqrt(jnp.mean(x * x, axis=-1, keepdims=True) + EPS) * g
    return y * cos2 + pltpu.roll(y, HEAD_DIM // 2, 1) * sin2


def _kv_post_kernel(kv_ref, cos_ref, sin_ref, gk_ref, rows_ref, win_ref, *attn_refs):
    cos2 = cos_ref[...]
    sin2 = sin_ref[...]
    tm = kv_ref.shape[0]
    for br in range(N_BRANCH):
        base = br * 2 * KVW
        for g in range(N_KV):
            c0 = base + g * HEAD_DIM
            k = _norm_rope(kv_ref[:, c0:c0 + HEAD_DIM], gk_ref[br:br + 1, :], cos2, sin2)
            if br < 2:
                rows_ref[:, c0:c0 + HEAD_DIM] = k
            else:
                win_ref[:, g * HEAD_DIM:(g + 1) * HEAD_DIM] = k
            if attn_refs and br >= 1:
                attn_refs[2 * (br - 1)][:, g * HEAD_DIM:(g + 1) * HEAD_DIM] = k.astype(BF16)
        v = kv_ref[:, base + KVW:base + 2 * KVW]
        if br < 2:
            rows_ref[:, base + KVW:base + 2 * KVW] = v
        else:
            win_ref[:, KVW:2 * KVW] = v
        if attn_refs and br >= 1:
            vt_ref = attn_refs[2 * (br - 1) + 1]
            for a in range(tm // KCH):
                vt_ref[a] = jnp.transpose(v[a * KCH:(a + 1) * KCH, :]).astype(BF16)


def _kv_post(kv, cos2, sin2, g_k, seq, for_attn):
    n = kv.shape[0]
    tm = _pick(seq, (256, 128, 64, 8))
    tps = seq // tm
    out_specs = [pl.BlockSpec((tm, 4 * KVW), lambda i: (i, 0)),
                 pl.BlockSpec((tm, 2 * KVW), lambda i: (i, 0))]
    out_shape = [jax.ShapeDtypeStruct((n, 4 * KVW), F32),
                 jax.ShapeDtypeStruct((n, 2 * KVW), F32)]
    if for_attn:
        assert tm % KCH == 0
        for _ in range(2):
            out_specs += [pl.BlockSpec((tm, KVW), lambda i: (i, 0)),
                          pl.BlockSpec((tm // KCH, KVW, KCH), lambda i: (i, 0, 0))]
            out_shape += [jax.ShapeDtypeStruct((n, KVW), BF16),
                          jax.ShapeDtypeStruct((n // KCH, KVW, KCH), BF16)]
    return pl.pallas_call(
        _kv_post_kernel,
        grid=(n // tm,),
        in_specs=[pl.BlockSpec((tm, 6 * KVW), lambda i: (i, 0)),
                  pl.BlockSpec((tm, HEAD_DIM), lambda i: (i % tps, 0)),
                  pl.BlockSpec((tm, HEAD_DIM), lambda i: (i % tps, 0)),
                  pl.BlockSpec((N_BRANCH, HEAD_DIM), lambda i: (0, 0))],
        out_specs=out_specs,
        out_shape=out_shape,
        compiler_params=_cparams(1),
        name="kv_post",
    )(kv, cos2, sin2, g_k)


def _compress_kernel(rows_ref, w_ref, o_ref):
    x = rows_ref[...]
    nb = x.shape[0] // BLK
    x = x.reshape(nb, BLK, x.shape[1]) * w_ref[...][None]
    o_ref[...] = jnp.sum(x, axis=1)


def _compress(rows, w_exp):
    n = rows.shape[0]
    nblk = n // BLK
    per = _pick(nblk, (8,))
    return pl.pallas_call(
        _compress_kernel,
        grid=(nblk // per,),
        in_specs=[pl.BlockSpec((per * BLK, 2 * KVW), lambda i: (i, 0)),
                  pl.BlockSpec((BLK, 2 * KVW), lambda i: (0, 0))],
        out_specs=pl.BlockSpec((per, 2 * KVW), lambda i: (i, 0)),
        out_shape=jax.ShapeDtypeStruct((nblk, 2 * KVW), F32),
        compiler_params=_cparams(1),
        name="compress",
    )(rows, w_exp)


def _compress_pages_kernel(pt_ref, *refs):
    page_refs = refs[:PAGES_PER_STEP]
    w_ref = refs[PAGES_PER_STEP]
    o_ref = refs[PAGES_PER_STEP + 1]
    w = w_ref[...]
    for k, pr in enumerate(page_refs):
        x = pr[0]
        nb = x.shape[0] // BLK
        o_ref[0, k * nb:(k + 1) * nb] = jnp.sum(x.reshape(nb, BLK, 2 * N_KV, HEAD_DIM) * w[None], axis=1)


def _compress_pages(cache4, page_table, w_exp3):
    bd, n_pages = page_table.shape
    page = cache4.shape[1]
    bpp = page // BLK
    steps = n_pages // PAGES_PER_STEP
    rows_out = PAGES_PER_STEP * bpp

    def page_spec(k):
        return pl.BlockSpec((1, page, 2 * N_KV, HEAD_DIM),
                            lambda b, c, pt: (pt[b, c * PAGES_PER_STEP + k], 0, 0, 0))

    grid_spec = pltpu.PrefetchScalarGridSpec(
        num_scalar_prefetch=1,
        grid=(bd, steps),
        in_specs=[page_spec(k) for k in range(PAGES_PER_STEP)]
        + [pl.BlockSpec((BLK, 2 * N_KV, HEAD_DIM), lambda b, c, pt: (0, 0, 0))],
        out_specs=pl.BlockSpec((1, rows_out, 2 * N_KV, HEAD_DIM), lambda b, c, pt: (b, c, 0, 0)),
    )
    return pl.pallas_call(
        _compress_pages_kernel,
        grid_spec=grid_spec,
        out_shape=jax.ShapeDtypeStruct((bd, n_pages * bpp, 2 * N_KV, HEAD_DIM), F32),
        compiler_params=_cparams(2),
        name="compress_pages",
    )(page_table, *([cache4] * PAGES_PER_STEP), w_exp3)


def _softmax_parts(s, mask):
    sm = jnp.where(mask, s, NEG)
    m = jnp.max(sm, axis=-1, keepdims=True)
    p = jnp.where(mask, jnp.exp(sm - m), 0.0)
    return p, m, jnp.sum(p, axis=-1, keepdims=True)


def _safe_inv(l):
    return jnp.where(l > 0.0, 1.0 / l, 0.0)


def _topk_mask(score, n_f, k):
    sel = jnp.zeros(score.shape, F32)
    big = jnp.float32(score.shape[-1] + 1)
    for _ in range(k):
        mx = jnp.max(score, axis=-1, keepdims=True)
        idx = jnp.min(jnp.where(score == mx, n_f, big), axis=-1, keepdims=True)
        hit = n_f == idx
        sel = jnp.where(hit, 1.0, sel)
        score = jnp.where(hit, -jnp.inf, score)
    return sel


def _select_blocks(psum, cur, n_i):
    forced = ((n_i == 0) | (n_i == cur) | (n_i == cur - 1)).astype(F32)
    valid = n_i <= cur
    score = jnp.where(valid, psum + FORCE * forced, NEG)
    sel = _topk_mask(score, n_i.astype(F32), min(N_SEL, psum.shape[-1]))
    return jnp.where(valid, sel, 0.0)


def _q_heads(uq_ref, g, gq, cos2, sin2, scale=None):
    hs = []
    for r in range(GRP):
        c0 = (g * GRP + r) * HEAD_DIM
        h = _norm_rope(uq_ref[:, c0:c0 + HEAD_DIM], gq, cos2, sin2)
        hs.append(h if scale is None else h * scale)
    return jnp.concatenate(hs, axis=0).astype(BF16)


def _tile_rows(x, reps):
    return jnp.concatenate([x] * reps, axis=0)


def _gate_and_store(o_ref, g, outs, z_refs, gate, t):
    for r in range(GRP):
        h = g * GRP + r
        acc = None
        for br in range(N_BRANCH):
            zg = jax.nn.silu(z_refs[br][:, h * HEAD_DIM:(h + 1) * HEAD_DIM]) \
                * gate[:, br * N_HEADS + h:br * N_HEADS + h + 1]
            term = outs[br][r * t:(r + 1) * t, :] * zg
            acc = term if acc is None else acc + term
        o_ref[:, h * HEAD_DIM:(h + 1) * HEAD_DIM] = acc.astype(o_ref.dtype)


def _rank_select_t(score_t, n_col, k):
    rank = jnp.zeros(score_t.shape, F32)
    for m in range(score_t.shape[0]):
        row = score_t[m:m + 1, :]
        tie = (n_col > m).astype(F32)
        rank = rank + jnp.where(row > score_t, 1.0, 0.0) + jnp.where(row == score_t, tie, 0.0)
    return jnp.where(rank < k, 1.0, 0.0)


def _prompt_attn_kernel(uq_ref, z0_ref, z1_ref, z2_ref, gate_ref, kvc_ref, ks_ref, vst_ref,
                        kw_ref, vwt_ref, cos_ref, sin_ref, gq_ref, o_ref, s_scr, bias_scr, *, seq):
    i = pl.program_id(1)
    t = BLK
    lanes = GRP * t
    nb = seq // BLK
    nch = seq // KCH
    nwc = min(WINDOW // KCH + 1, nch)
    bpc = KCH // BLK
    cos2 = cos_ref[...]
    sin2 = sin_ref[...]
    gq = gq_ref[...]
    gate = jax.nn.sigmoid(gate_ref[...])
    qpos_r = _tile_rows(i * t + lax.broadcasted_iota(jnp.int32, (t, 1), 0), GRP)
    qpos_l = i * t + (lax.broadcasted_iota(jnp.int32, (1, lanes), 1) & (t - 1))
    qpos_h = qpos_l[:, 0:2 * t]
    cur_h = qpos_h // BLK
    n_row = lax.broadcasted_iota(jnp.int32, (1, nb), 1)
    n_col = lax.broadcasted_iota(jnp.int32, (nb, 1), 0)
    cmask = (n_row * BLK + (BLK - 1)) <= qpos_r
    cmask_t = (n_col * BLK + (BLK - 1)) <= qpos_l
    forced_t = ((n_col == 0) | (n_col == cur_h) | (n_col == cur_h - 1)).astype(F32)
    valid_t = n_col <= cur_h
    k_in_chunk = lax.broadcasted_iota(jnp.int32, (KCH, 1), 0)

    def group_max(x):
        return jnp.max(x.reshape(KCH // 8, 8, lanes), axis=0)

    def group_sum(x):
        return jnp.sum(x.reshape(KCH // 8, 8, lanes), axis=0)

    def finish(lrun, ot):
        l = jnp.sum(lrun, axis=0, keepdims=True)
        return jnp.transpose(ot * _safe_inv(l))

    for g in range(N_KV):
        q = _q_heads(uq_ref, g, gq, cos2, sin2, HEAD_DIM ** -0.5)
        ksl = slice(g * HEAD_DIM, (g + 1) * HEAD_DIM)
        kc = kvc_ref[0, :, ksl].astype(BF16)
        vc = kvc_ref[0, :, KVW + g * HEAD_DIM:KVW + (g + 1) * HEAD_DIM].astype(BF16)

        p, _, l = _softmax_parts(_dot_nt(q, kc), cmask)
        o_cmp = _dot((p * _safe_inv(l)).astype(BF16), vc)

        st = jnp.where(cmask_t, _dot_nt(kc, q), NEG)
        pt = jnp.where(cmask_t, jnp.exp(st - jnp.max(st, axis=0, keepdims=True)), 0.0)
        pt = pt * _safe_inv(jnp.sum(pt, axis=0, keepdims=True))
        half = pt[:, 0:2 * t] + pt[:, 2 * t:4 * t]
        psum_t = half + pltpu.roll(half, t, 1)
        score_t = jnp.where(valid_t, psum_t + FORCE * forced_t, NEG)
        sel_t = jnp.where(valid_t, _rank_select_t(score_t, n_col, min(N_SEL, nb)), 0.0)
        bias_t = (sel_t - 1.0) * (-NEG)
        bias_scr[...] = jnp.concatenate([bias_t, bias_t], axis=1)

        def sel_scores(c):
            k = ks_ref[0, pl.ds(pl.multiple_of(c * KCH, KCH), KCH), ksl]
            bias = [jnp.broadcast_to(bias_scr[pl.ds(c * bpc + j, 1), :], (BLK, lanes)) for j in range(bpc)]
            return _dot_nt(k, q) + jnp.concatenate(bias, axis=0)

        def sel_pass1(c, mrun):
            st = sel_scores(c)
            s_scr[c] = st
            return jnp.maximum(mrun, group_max(st))

        cd = i // bpc
        mrun = lax.fori_loop(0, cd, sel_pass1, jnp.full((8, lanes), NEG, F32))
        st = jnp.where(cd * KCH + k_in_chunk <= qpos_l, sel_scores(cd), NEG)
        s_scr[cd] = st
        m = jnp.max(jnp.maximum(mrun, group_max(st)), axis=0, keepdims=True)

        def sel_pass2(c, carry):
            lrun, ot = carry
            p = jnp.exp(s_scr[c] - m)
            return lrun + group_sum(p), ot + _dot(vst_ref[c, ksl, :], p.astype(BF16))

        lrun, ot = lax.fori_loop(0, cd + 1, sel_pass2,
                                 (jnp.zeros((8, lanes), F32), jnp.zeros((HEAD_DIM, lanes), F32)))
        o_sel = finish(lrun, ot)

        cs = jnp.minimum(jnp.maximum(i - WINDOW // BLK, 0) // bpc, nch - nwc)
        mrun = jnp.full((8, lanes), NEG, F32)
        for j in range(nwc):
            c = cs + j
            k = kw_ref[0, pl.ds(pl.multiple_of(c * KCH, KCH), KCH), ksl]
            dpos = qpos_l - (c * KCH + k_in_chunk)
            st = jnp.where((dpos >= 0) & (dpos < WINDOW), _dot_nt(k, q), NEG)
            s_scr[j] = st
            mrun = jnp.maximum(mrun, group_max(st))
        m = jnp.max(mrun, axis=0, keepdims=True)
        lrun = jnp.zeros((8, lanes), F32)
        ot = jnp.zeros((HEAD_DIM, lanes), F32)
        for j in range(nwc):
            p = jnp.exp(s_scr[j] - m)
            lrun = lrun + group_sum(p)
            ot = ot + _dot(vwt_ref[cs + j, ksl, :], p.astype(BF16))
        o_win = finish(lrun, ot)

        _gate_and_store(o_ref, g, (o_cmp, o_sel, o_win), (z0_ref, z1_ref, z2_ref), gate, t)


def _prompt_attn(u, gate, kvc, ks, vst, kw, vwt, cos2, sin2, gq, batch, seq):
    n = u.shape[0]
    t = BLK
    assert GRP * t == 2 * KCH and seq % KCH == 0
    nqb = seq // t
    nb = seq // BLK
    nch = seq // KCH
    row = lambda b, i: (b * nqb + i, 0)
    keys = pl.BlockSpec((1, seq, KVW), lambda b, i: (b, 0, 0))
    vals_t = pl.BlockSpec((nch, KVW, KCH), lambda b, i: (b, 0, 0))
    return pl.pallas_call(
        functools.partial(_prompt_attn_kernel, seq=seq),
        grid=(batch, nqb),
        in_specs=[pl.BlockSpec((t, HD), row),
                  pl.BlockSpec((t, HD), lambda b, i: (b * nqb + i, 1)),
                  pl.BlockSpec((t, HD), lambda b, i: (b * nqb + i, 2)),
                  pl.BlockSpec((t, HD), lambda b, i: (b * nqb + i, 3)),
                  pl.BlockSpec((t, HEAD_DIM), row),
                  pl.BlockSpec((1, nb, 2 * KVW), lambda b, i: (b, 0, 0)),
                  keys, vals_t, keys, vals_t,
                  pl.BlockSpec((t, HEAD_DIM), lambda b, i: (i, 0)),
                  pl.BlockSpec((t, HEAD_DIM), lambda b, i: (i, 0)),
                  pl.BlockSpec((1, HEAD_DIM), lambda b, i: (0, 0))],
        out_specs=pl.BlockSpec((t, HD), row),
        out_shape=jax.ShapeDtypeStruct((n, HD), BF16),
        scratch_shapes=[pltpu.VMEM((nch, KCH, GRP * t), F32),
                        pltpu.VMEM((nb, GRP * t), F32)],
        compiler_params=_cparams(2),
        name="prompt_attn",
    )(u, u, u, u, gate, kvc.reshape(batch, nb, 2 * KVW), ks.reshape(batch, seq, KVW), vst,
      kw.reshape(batch, seq, KVW), vwt, cos2, sin2, gq)


def _decode_front_kernel(uq_ref, kvc_ref, swin_ref, nwin_ref, cos_ref, sin_ref, gq_ref,
                         qn_ref, sel_ref, ocmp_ref, owin_ref, *, past, t):
    scale = HEAD_DIM ** -0.5
    cos2 = cos_ref[...]
    sin2 = sin_ref[...]
    gq = gq_ref[...]
    nbp = kvc_ref.shape[1]
    wbuf = swin_ref.shape[1]
    qpos = past + lax.broadcasted_iota(jnp.int32, (t, 1), 0)
    qpos_r = _tile_rows(qpos, GRP)
    cur = qpos // BLK
    n_i = lax.broadcasted_iota(jnp.int32, (1, nbp), 1)
    cmask = (n_i * BLK + (BLK - 1)) <= qpos_r
    kwpos_old = (past - wbuf) + lax.broadcasted_iota(jnp.int32, (1, wbuf), 1)
    kwpos_new = past + lax.broadcasted_iota(jnp.int32, (1, NEW_PAD), 1)
    d_old = qpos_r - kwpos_old
    d_new = qpos_r - kwpos_new
    m_old = (d_old >= 0) & (d_old < WINDOW) & (kwpos_old >= 0)
    m_new = (d_new >= 0) & (d_new < WINDOW) & (kwpos_new < past + t)

    for g in range(N_KV):
        q = _q_heads(uq_ref, g, gq, cos2, sin2)
        rows = slice(g * GRP * t, (g + 1) * GRP * t)
        ksl = slice(g * HEAD_DIM, (g + 1) * HEAD_DIM)
        vsl = slice(KVW + g * HEAD_DIM, KVW + (g + 1) * HEAD_DIM)
        qn_ref[0, rows, :] = q

        s = _dot_nt(q, kvc_ref[0, :, ksl].astype(BF16)) * scale
        p, _, l = _softmax_parts(s, cmask)
        p = p * _safe_inv(l)
        ocmp_ref[0, rows, :] = _dot(p.astype(BF16), kvc_ref[0, :, vsl].astype(BF16))

        psum = p[0:t]
        for r in range(1, GRP):
            psum = psum + p[r * t:(r + 1) * t]
        sel_ref[0, g * t:(g + 1) * t, :] = _select_blocks(psum, cur, n_i)

        s_old = _dot_nt(q, swin_ref[0, :, ksl].astype(BF16)) * scale
        s_new = _dot_nt(q, nwin_ref[0, :, ksl].astype(BF16)) * scale
        mx = jnp.maximum(jnp.max(jnp.where(m_old, s_old, NEG), axis=-1, keepdims=True),
                         jnp.max(jnp.where(m_new, s_new, NEG), axis=-1, keepdims=True))
        p_old = jnp.where(m_old, jnp.exp(jnp.where(m_old, s_old, NEG) - mx), 0.0)
        p_new = jnp.where(m_new, jnp.exp(jnp.where(m_new, s_new, NEG) - mx), 0.0)
        l = jnp.sum(p_old, axis=-1, keepdims=True) + jnp.sum(p_new, axis=-1, keepdims=True)
        o = _dot(p_old.astype(BF16), swin_ref[0, :, vsl].astype(BF16)) \
            + _dot(p_new.astype(BF16), nwin_ref[0, :, vsl].astype(BF16))
        owin_ref[0, rows, :] = o * _safe_inv(l)


def _decode_front(u, kvc_all, state_win2, new_win, cos2, sin2, gq, bd, t, past):
    nbp = kvc_all.shape[1]
    wbuf = state_win2.shape[1]
    rows = N_HEADS * t
    return pl.pallas_call(
        functools.partial(_decode_front_kernel, past=past, t=t),
        grid=(bd,),
        in_specs=[pl.BlockSpec((t, HD), lambda b: (b, 0)),
                  pl.BlockSpec((1, nbp, 2 * KVW), lambda b: (b, 0, 0)),
                  pl.BlockSpec((1, wbuf, 2 * KVW), lambda b: (b, 0, 0)),
                  pl.BlockSpec((1, NEW_PAD, 2 * KVW), lambda b: (b, 0, 0)),
                  pl.BlockSpec((t, HEAD_DIM), lambda b: (0, 0)),
                  pl.BlockSpec((t, HEAD_DIM), lambda b: (0, 0)),
                  pl.BlockSpec((1, HEAD_DIM), lambda b: (0, 0))],
        out_specs=[pl.BlockSpec((1, rows, HEAD_DIM), lambda b: (b, 0, 0)),
                   pl.BlockSpec((1, N_KV * t, nbp), lambda b: (b, 0, 0)),
                   pl.BlockSpec((1, rows, HEAD_DIM), lambda b: (b, 0, 0)),
                   pl.BlockSpec((1, rows, HEAD_DIM), lambda b: (b, 0, 0))],
        out_shape=[jax.ShapeDtypeStruct((bd, rows, HEAD_DIM), BF16),
                   jax.ShapeDtypeStruct((bd, N_KV * t, nbp), F32),
                   jax.ShapeDtypeStruct((bd, rows, HEAD_DIM), F32),
                   jax.ShapeDtypeStruct((bd, rows, HEAD_DIM), F32)],
        compiler_params=_cparams(1),
        name="decode_front",
    )(u, kvc_all, state_win2, new_win, cos2, sin2, gq)


def _decode_sel_kernel(pt_ref, *refs, t, n_steps):
    del pt_ref
    np_ = PAGES_PER_STEP
    (qn_ref, selc_ref, seln_ref) = refs[:3]
    page_refs = refs[3:3 + np_]
    (new_ref, ocmp_ref, owin_ref, z0_ref, z1_ref, z2_ref, gate_ref, e_ref,
     o_ref, m_ref, l_ref, acc_ref) = refs[3 + np_:]
    c = pl.program_id(1)
    scale = HEAD_DIM ** -0.5
    gt = GRP * t

    @pl.when(c == 0)
    def _():
        m_ref[...] = jnp.full(m_ref.shape, NEG, F32)
        l_ref[...] = jnp.zeros(l_ref.shape, F32)
        acc_ref[...] = jnp.zeros(acc_ref.shape, F32)

    def update(g, s, mask, v):
        rows = slice(g * gt, (g + 1) * gt)
        sm = jnp.where(mask, s, NEG)
        m_old = m_ref[rows, :]
        m_new = jnp.maximum(m_old, jnp.max(sm, axis=-1, keepdims=True))
        alpha = jnp.exp(m_old - m_new)
        p = jnp.where(mask, jnp.exp(sm - m_new), 0.0)
        l_ref[rows, :] = alpha * l_ref[rows, :] + jnp.sum(p, axis=-1, keepdims=True)
        acc_ref[rows, :] = alpha * acc_ref[rows, :] + _dot(p.astype(BF16), v)
        m_ref[rows, :] = m_new

    for g in range(N_KV):
        q = qn_ref[0, g * gt:(g + 1) * gt, :]
        k = jnp.concatenate([pr[0, :, g, :] for pr in page_refs], axis=0).astype(BF16)
        v = jnp.concatenate([pr[0, :, N_KV + g, :] for pr in page_refs], axis=0).astype(BF16)
        s = _dot_nt(q, k) * scale
        selk = _dot(selc_ref[0, 0, g * t:(g + 1) * t, :].astype(BF16), e_ref[...]) > 0.5
        update(g, s, _tile_rows(selk, GRP), v)

    @pl.when(c == n_steps - 1)
    def _():
        gate = jax.nn.sigmoid(gate_ref[...])
        tq = _tile_rows(lax.broadcasted_iota(jnp.int32, (t, 1), 0), GRP)
        lk = lax.broadcasted_iota(jnp.int32, (1, NEW_PAD), 1)
        causal = lk <= tq
        for g in range(N_KV):
            rows = slice(g * gt, (g + 1) * gt)
            q = qn_ref[0, rows, :]
            ksl = slice(g * HEAD_DIM, (g + 1) * HEAD_DIM)
            vsl = slice(KVW + g * HEAD_DIM, KVW + (g + 1) * HEAD_DIM)
            s = _dot_nt(q, new_ref[0, :, ksl].astype(BF16)) * scale
            seln = _tile_rows(seln_ref[0, 0, g * t:(g + 1) * t, 0:1], GRP) > 0.5
            update(g, s, seln & causal, new_ref[0, :, vsl].astype(BF16))
            o_sel = acc_ref[rows, :] * _safe_inv(l_ref[rows, :])
            _gate_and_store(o_ref, g, (ocmp_ref[0, rows, :], o_sel, owin_ref[0, rows, :]),
                            (z0_ref, z1_ref, z2_ref), gate, t)


def _decode_sel(page_table, qn, selc, cache4, new_sel, ocmp, owin, u, gate, e8, bd, t):
    n_pages = page_table.shape[1]
    page = cache4.shape[1]
    n_steps = n_pages // PAGES_PER_STEP
    bps = PAGES_PER_STEP * (page // BLK)
    rows = N_HEADS * t

    def page_spec(k):
        return pl.BlockSpec((1, page, 2 * N_KV, HEAD_DIM),
                            lambda b, c, pt: (pt[b, c * PAGES_PER_STEP + k], 0, 1, 0))

    grid_spec = pltpu.PrefetchScalarGridSpec(
        num_scalar_prefetch=1,
        grid=(bd, n_steps),
        in_specs=[pl.BlockSpec((1, rows, HEAD_DIM), lambda b, c, pt: (b, 0, 0)),
                  pl.BlockSpec((1, 1, N_KV * t, bps), lambda b, c, pt: (b, c, 0, 0)),
                  pl.BlockSpec((1, 1, N_KV * t, bps), lambda b, c, pt: (b, n_steps, 0, 0))]
        + [page_spec(k) for k in range(PAGES_PER_STEP)]
        + [pl.BlockSpec((1, NEW_PAD, 2 * KVW), lambda b, c, pt: (b, 0, 0)),
           pl.BlockSpec((1, rows, HEAD_DIM), lambda b, c, pt: (b, 0, 0)),
           pl.BlockSpec((1, rows, HEAD_DIM), lambda b, c, pt: (b, 0, 0)),
           pl.BlockSpec((t, HD), lambda b, c, pt: (b, 1)),
           pl.BlockSpec((t, HD), lambda b, c, pt: (b, 2)),
           pl.BlockSpec((t, HD), lambda b, c, pt: (b, 3)),
           pl.BlockSpec((t, HEAD_DIM), lambda b, c, pt: (b, 0)),
           pl.BlockSpec((bps, bps * BLK), lambda b, c, pt: (0, 0))],
        out_specs=pl.BlockSpec((t, HD), lambda b, c, pt: (b, 0)),
        scratch_shapes=[pltpu.VMEM((rows, 1), F32), pltpu.VMEM((rows, 1), F32),
                        pltpu.VMEM((rows, HEAD_DIM), F32)],
    )
    return pl.pallas_call(
        functools.partial(_decode_sel_kernel, t=t, n_steps=n_steps),
        grid_spec=grid_spec,
        out_shape=jax.ShapeDtypeStruct((bd * t, HD), BF16),
        compiler_params=_cparams(2),
        name="decode_sel",
    )(page_table, qn, selc, selc, *([cache4] * PAGES_PER_STEP), new_sel, ocmp, owin,
      u, u, u, gate, e8)


def _rope_tables(pos):
    half = HEAD_DIM // 2
    inv = ROPE_THETA ** (-jnp.arange(half, dtype=F32) / half)
    ang = pos.astype(F32)[:, None] * inv[None, :]
    c = jnp.cos(ang)
    s = jnp.sin(ang)
    return jnp.concatenate([c, c], axis=-1), jnp.concatenate([-s, s], axis=-1)


def _block_expand(nblocks):
    key_blk = jnp.arange(nblocks * BLK, dtype=jnp.int32) // BLK
    return (key_blk[None, :] == jnp.arange(nblocks, dtype=jnp.int32)[:, None]).astype(BF16)


def _conv_layers(x, prev, seq, g_a, w_in_a, conv_w, w_out_a):
    states = []
    ch = conv_w.shape[-1]
    for layer in range(N_A):
        xn = _rms_cast(x, g_a[layer])
        u = _matmul(xn, w_in_a, layer, 4 * ch)
        y, st = _conv_gate(u, prev[layer], conv_w[layer], seq)
        x = _matmul(y, w_out_a, layer, x.shape[1], res=x)
        states.append(st)
    return x, jnp.stack(states)


def _shared_kv(x, pos, seq, g_kv, w_kv, g_k, for_attn):
    cos2, sin2 = _rope_tables(pos)
    kv = _matmul(_rms_cast(x, g_kv), w_kv, 0, 6 * KVW)
    return _kv_post(kv, cos2, sin2, g_k, seq, for_attn), cos2, sin2


def _query_side(x, lb, g_b, w_qz, w_gate):
    xn = _rms_cast(x, g_b[lb])
    u = _matmul(xn, w_qz, lb, 4 * HD)
    gate = _matmul(xn, w_gate[lb], 0, HEAD_DIM)
    return u, gate


def kernel(x_prompt, x_sample, state_conv, cache_kv, state_win, page_table, g_a, w_in_a, conv_w,
           w_out_a, g_kv, w_kv, g_k, w_cmp, g_b, w_qz, g_q, w_out_b):
    batch, seq, d = x_prompt.shape
    bd, t, _ = x_sample.shape
    ch = conv_w.shape[-1]
    n_pool, page = cache_kv.shape[:2]
    n_pages = page_table.shape[1]
    past = n_pages * page
    wbuf = state_win.shape[1]
    assert seq % BLK == 0 and page % BLK == 0 and t <= BLK and n_pages % PAGES_PER_STEP == 0

    w_exp3 = jnp.broadcast_to(w_cmp.reshape(BLK, 2 * N_KV, 1), (BLK, 2 * N_KV, HEAD_DIM))
    w_exp = w_exp3.reshape(BLK, 2 * KVW)
    n_gate = N_BRANCH * N_HEADS
    w_gate = jnp.pad(w_qz[:, :, 4 * HD:4 * HD + n_gate], ((0, 0), (0, 0), (0, HEAD_DIM - n_gate)))

    xp = x_prompt.reshape(batch * seq, d)
    xp, conv_p = _conv_layers(xp, jnp.zeros((N_A, batch, CONV_W - 1, ch), F32), seq,
                              g_a, w_in_a, conv_w, w_out_a)
    pos_p = jnp.arange(seq, dtype=jnp.int32)
    (rows_p, win_p, ks_p, vst_p, kw_p, vwt_p), cos_p, sin_p = _shared_kv(
        xp, pos_p, seq, g_kv, w_kv, g_k, True)
    kvc_p = _compress(rows_p, w_exp)
    for lb in range(N_B):
        u, gate = _query_side(xp, lb, g_b, w_qz, w_gate)
        o = _prompt_attn(u, gate, kvc_p, ks_p, vst_p, kw_p, vwt_p, cos_p, sin_p, g_q[lb:lb + 1],
                         batch, seq)
        xp = _matmul(o, w_out_b, lb, d, res=xp)
    wn = min(WINDOW, seq)
    y_prompt = xp.reshape(batch, seq, d)
    kv_rows_prompt = rows_p.reshape(batch, seq, 4, N_KV, HEAD_DIM)
    win_prompt = win_p.reshape(batch, seq, 2, N_KV, HEAD_DIM)[:, seq - wn:]

    xs = x_sample.reshape(bd * t, d)
    xs, conv_s = _conv_layers(xs, state_conv, t, g_a, w_in_a, conv_w, w_out_a)
    pos_s = past + jnp.arange(t, dtype=jnp.int32)
    (rows_s, win_s), cos_s, sin_s = _shared_kv(xs, pos_s, t, g_kv, w_kv, g_k, False)
    cache4 = cache_kv.reshape(n_pool, page, 4 * N_KV, HEAD_DIM)
    kvc_past = _compress_pages(cache4, page_table, w_exp3).reshape(bd, past // BLK, 2 * KVW)
    rows_s3 = rows_s.reshape(bd, t, 4 * KVW)
    rows_pad = jnp.pad(rows_s3, ((0, 0), (0, NEW_PAD - t), (0, 0)))
    kvc_new = _compress(rows_pad[:, :BLK, :2 * KVW].reshape(bd * BLK, 2 * KVW), w_exp)
    nb_past = past // BLK
    bps = PAGES_PER_STEP * (page // BLK)
    nbp = nb_past + bps
    kvc_all = jnp.concatenate(
        [kvc_past, kvc_new[:, None, :], jnp.zeros((bd, bps - 1, 2 * KVW), F32)], axis=1)
    new_sel = rows_pad[:, :, 2 * KVW:]
    new_win = jnp.pad(win_s.reshape(bd, t, 2 * KVW), ((0, 0), (0, NEW_PAD - t), (0, 0)))
    state_win2 = state_win.reshape(bd, wbuf, 2 * KVW)
    e8 = _block_expand(bps)
    for lb in range(N_B):
        u, gate = _query_side(xs, lb, g_b, w_qz, w_gate)
        qn, sel, ocmp, owin = _decode_front(u, kvc_all, state_win2, new_win, cos_s, sin_s,
                                            g_q[lb:lb + 1], bd, t, past)
        selc = sel.reshape(bd, N_KV * t, nbp // bps, bps).transpose(0, 2, 1, 3)
        o = _decode_sel(page_table, qn, selc, cache4, new_sel, ocmp, owin, u, gate, e8, bd, t)
        xs = _matmul(o, w_out_b, lb, d, res=xs)
    y_sample = xs.reshape(bd, t, d)
    kv_rows_sample = rows_s.reshape(bd, t, 4, N_KV, HEAD_DIM)
    kvw = jnp.concatenate([state_win, win_s.reshape(bd, t, 2, N_KV, HEAD_DIM)], axis=1)
    win_sample = kvw[:, kvw.shape[1] - min(WINDOW, kvw.shape[1]):]

    return (y_prompt, y_sample, conv_p, conv_s, kv_rows_prompt, kv_rows_sample, win_prompt, win_sample)
```

```python
import functools

import jax
import jax.numpy as jnp
from jax import lax
from jax.experimental import pallas as pl
from jax.experimental.pallas import tpu as pltpu

N_A = 2
N_B = 2
CONV_W = 3
N_HEADS = 16
N_KV = 4
GRP = N_HEADS // N_KV
HEAD_DIM = 128
N_BRANCH = 3
BLK = 64
N_SEL = 16
WINDOW = 512
ROPE_THETA = 10000.0
EPS = 1e-6
NEG = -1e30
FORCE = 1e3

HD = N_HEADS * HEAD_DIM
KVW = N_KV * HEAD_DIM
NEW_PAD = 128
KCH = 128
SPAN_KEYS = 512
PAGES_PER_STEP = 8
VMEM_LIMIT = 56 * 1024 * 1024

F32 = jnp.float32
BF16 = jnp.bfloat16


def _cparams(n_axes):
    return pltpu.CompilerParams(
        dimension_semantics=("arbitrary",) * n_axes, vmem_limit_bytes=VMEM_LIMIT)


def _dot_nt(a, b):
    return lax.dot_general(a, b, (((1,), (1,)), ((), ())), preferred_element_type=F32)


def _dot(a, b):
    return jnp.dot(a, b, preferred_element_type=F32)


def _pick(n, prefs):
    for p in prefs:
        if n % p == 0:
            return p
    return n


def _rms_cast_kernel(x_ref, g_ref, o_ref):
    x = x_ref[...]
    y = x * lax.rsqrt(jnp.mean(x * x, axis=-1, keepdims=True) + EPS)
    o_ref[...] = (y * g_ref[...]).astype(BF16)


def _rms_cast(x, g):
    n, d = x.shape
    tm = _pick(n, (512, 256, 128, 64, 8))
    return pl.pallas_call(
        _rms_cast_kernel,
        grid=(n // tm,),
        in_specs=[pl.BlockSpec((tm, d), lambda i: (i, 0)),
                  pl.BlockSpec((1, d), lambda i: (0, 0))],
        out_specs=pl.BlockSpec((tm, d), lambda i: (i, 0)),
        out_shape=jax.ShapeDtypeStruct((n, d), BF16),
        compiler_params=_cparams(1),
        name="rms_cast",
    )(x, g.reshape(1, d))


def _mm_kernel(*refs, has_res):
    if has_res:
        x_ref, w_ref, r_ref, o_ref, wb_ref = refs
    else:
        x_ref, w_ref, o_ref, wb_ref = refs

    @pl.when(pl.program_id(1) == 0)
    def _():
        wb_ref[...] = w_ref[...].astype(BF16)

    acc = _dot(x_ref[...], wb_ref[...])
    if has_res:
        acc = r_ref[...] + acc
    o_ref[...] = acc


def _matmul(x, w, layer, m_out, res=None):
    n, k = x.shape
    tm = _pick(n, (1024, 512, 256, 128, 64, 8))
    tn = _pick(m_out, (512, 256, 128))
    if w.ndim == 3:
        w_spec = pl.BlockSpec((None, k, tn), lambda j, i: (layer, 0, j))
    else:
        w_spec = pl.BlockSpec((k, tn), lambda j, i: (0, j))
    in_specs = [pl.BlockSpec((tm, k), lambda j, i: (i, 0)), w_spec]
    args = [x, w]
    if res is not None:
        in_specs.append(pl.BlockSpec((tm, tn), lambda j, i: (i, j)))
        args.append(res)
    return pl.pallas_call(
        functools.partial(_mm_kernel, has_res=res is not None),
        grid=(m_out // tn, n // tm),
        in_specs=in_specs,
        out_specs=pl.BlockSpec((tm, tn), lambda j, i: (i, j)),
        out_shape=jax.ShapeDtypeStruct((n, m_out), F32),
        scratch_shapes=[pltpu.VMEM((k, tn), BF16)],
        compiler_params=_cparams(2),
        name="matmul_res" if res is not None else "matmul",
    )(*args)


def _conv_gate_kernel(b_ref, cg_ref, h_ref, z_ref, prev_ref, w_ref, y_ref, st_ref, carry_ref, *, tps):
    i = pl.program_id(1)
    c = cg_ref[...] * h_ref[...]
    tm = c.shape[0]

    @pl.when(i % tps == 0)
    def _():
        carry_ref[0:2, :] = prev_ref[0]

    p0 = carry_ref[0:1, :]
    p1 = carry_ref[1:2, :]
    row = lax.broadcasted_iota(jnp.int32, c.shape, 0)
    c1 = jnp.where(row == 0, p1, pltpu.roll(c, 1, 0))
    c2 = jnp.where(row == 0, p0, jnp.where(row == 1, p1, pltpu.roll(c, 2, 0)))
    conv = w_ref[0:1, :] * c2 + w_ref[1:2, :] * c1 + w_ref[2:3, :] * c
    y = jax.nn.silu(z_ref[...]) * (b_ref[...] * conv)
    y_ref[...] = y.astype(BF16)
    last = c[tm - 2:tm, :]
    carry_ref[0:2, :] = last
    st_ref[0] = last


def _conv_gate(u, prev, w_conv, seq):
    n = u.shape[0]
    ch = u.shape[1] // 4
    tm = _pick(seq, (256, 128, 64, 8))
    tc = _pick(ch, (512, 256, 128))
    tps = seq // tm
    nc = ch // tc
    nb = n // seq
    return pl.pallas_call(
        functools.partial(_conv_gate_kernel, tps=tps),
        grid=(nc, n // tm),
        in_specs=[pl.BlockSpec((tm, tc), lambda j, i: (i, j)),
                  pl.BlockSpec((tm, tc), lambda j, i: (i, nc + j)),
                  pl.BlockSpec((tm, tc), lambda j, i: (i, 2 * nc + j)),
                  pl.BlockSpec((tm, tc), lambda j, i: (i, 3 * nc + j)),
                  pl.BlockSpec((1, 2, tc), lambda j, i: (i // tps, 0, j)),
                  pl.BlockSpec((CONV_W, tc), lambda j, i: (0, j))],
        out_specs=[pl.BlockSpec((tm, tc), lambda j, i: (i, j)),
                   pl.BlockSpec((1, 2, tc), lambda j, i: (i // tps, 0, j))],
        out_shape=[jax.ShapeDtypeStruct((n, ch), BF16),
                   jax.ShapeDtypeStruct((nb, CONV_W - 1, ch), F32)],
        scratch_shapes=[pltpu.VMEM((8, tc), F32)],
        compiler_params=_cparams(2),
        name="conv_gate",
    )(u, u, u, u, prev, w_conv)


def _norm_rope(x, g, cos2, sin2):
    y = x * lax.rsqrt(jnp.mean(x * x, axis=-1, keepdims=True) + EPS) * g
    return y * cos2 + pltpu.roll(y, HEAD_DIM // 2, 1) * sin2


def _kv_post_kernel(kv_ref, cos_ref, sin_ref, gk_ref, rows_ref, win_ref, *attn_refs):
    cos2 = cos_ref[...]
    sin2 = sin_ref[...]
    tm = kv_ref.shape[0]
    for br in range(N_BRANCH):
        base = br * 2 * KVW
        for g in range(N_KV):
            c0 = base + g * HEAD_DIM
            k = _norm_rope(kv_ref[:, c0:c0 + HEAD_DIM], gk_ref[br:br + 1, :], cos2, sin2)
            if br < 2:
                rows_ref[:, c0:c0 + HEAD_DIM] = k
            else:
                win_ref[:, g * HEAD_DIM:(g + 1) * HEAD_DIM] = k
            if attn_refs and br >= 1:
                attn_refs[2 * (br - 1)][:, g * HEAD_DIM:(g + 1) * HEAD_DIM] = k.astype(BF16)
        v = kv_ref[:, base + KVW:base + 2 * KVW]
        if br < 2:
            rows_ref[:, base + KVW:base + 2 * KVW] = v
        else:
            win_ref[:, KVW:2 * KVW] = v
        if attn_refs and br >= 1:
            vt_ref = attn_refs[2 * (br - 1) + 1]
            for a in range(tm // KCH):
                vt_ref[a] = jnp.transpose(v[a * KCH:(a + 1) * KCH, :]).astype(BF16)


def _kv_post(kv, cos2, sin2, g_k, seq, for_attn):
    n = kv.shape[0]
    tm = _pick(seq, (256, 128, 64, 8))
    tps = seq // tm
    out_specs = [pl.BlockSpec((tm, 4 * KVW), lambda i: (i, 0)),
                 pl.BlockSpec((tm, 2 * KVW), lambda i: (i, 0))]
    out_shape = [jax.ShapeDtypeStruct((n, 4 * KVW), F32),
                 jax.ShapeDtypeStruct((n, 2 * KVW), F32)]
    if for_attn:
        assert tm % KCH == 0
        for _ in range(2):
            out_specs += [pl.BlockSpec((tm, KVW), lambda i: (i, 0)),
                          pl.BlockSpec((tm // KCH, KVW, KCH), lambda i: (i, 0, 0))]
            out_shape += [jax.ShapeDtypeStruct((n, KVW), BF16),
                          jax.ShapeDtypeStruct((n // KCH, KVW, KCH), BF16)]
    return pl.pallas_call(
        _kv_post_kernel,
        grid=(n // tm,),
        in_specs=[pl.BlockSpec((tm, 6 * KVW), lambda i: (i, 0)),
                  pl.BlockSpec((tm, HEAD_DIM), lambda i: (i % tps, 0)),
                  pl.BlockSpec((tm, HEAD_DIM), lambda i: (i % tps, 0)),
                  pl.BlockSpec((N_BRANCH, HEAD_DIM), lambda i: (0, 0))],
        out_specs=out_specs,
        out_shape=out_shape,
        compiler_params=_cparams(1),
        name="kv_post",
    )(kv, cos2, sin2, g_k)


def _compress_kernel(rows_ref, w_ref, o_ref):
    x = rows_ref[...]
    nb = x.shape[0] // BLK
    x = x.reshape(nb, BLK, x.shape[1]) * w_ref[...][None]
    o_ref[...] = jnp.sum(x, axis=1)


def _compress(rows, w_exp):
    n = rows.shape[0]
    nblk = n // BLK
    per = _pick(nblk, (8,))
    return pl.pallas_call(
        _compress_kernel,
        grid=(nblk // per,),
        in_specs=[pl.BlockSpec((per * BLK, 2 * KVW), lambda i: (i, 0)),
                  pl.BlockSpec((BLK, 2 * KVW), lambda i: (0, 0))],
        out_specs=pl.BlockSpec((per, 2 * KVW), lambda i: (i, 0)),
        out_shape=jax.ShapeDtypeStruct((nblk, 2 * KVW), F32),
        compiler_params=_cparams(1),
        name="compress",
    )(rows, w_exp)


def _compress_pages_kernel(pt_ref, *refs):
    page_refs = refs[:PAGES_PER_STEP]
    w_ref = refs[PAGES_PER_STEP]
    o_ref = refs[PAGES_PER_STEP + 1]
    w = w_ref[...]
    for k, pr in enumerate(page_refs):
        x = pr[0]
        nb = x.shape[0] // BLK
        o_ref[0, k * nb:(k + 1) * nb] = jnp.sum(x.reshape(nb, BLK, 2 * N_KV, HEAD_DIM) * w[None], axis=1)


def _compress_pages(cache4, page_table, w_exp3):
    bd, n_pages = page_table.shape
    page = cache4.shape[1]
    bpp = page // BLK
    steps = n_pages // PAGES_PER_STEP
    rows_out = PAGES_PER_STEP * bpp

    def page_spec(k):
        return pl.BlockSpec((1, page, 2 * N_KV, HEAD_DIM),
                            lambda b, c, pt: (pt[b, c * PAGES_PER_STEP + k], 0, 0, 0))

    grid_spec = pltpu.PrefetchScalarGridSpec(
        num_scalar_prefetch=1,
        grid=(bd, steps),
        in_specs=[page_spec(k) for k in range(PAGES_PER_STEP)]
        + [pl.BlockSpec((BLK, 2 * N_KV, HEAD_DIM), lambda b, c, pt: (0, 0, 0))],
        out_specs=pl.BlockSpec((1, rows_out, 2 * N_KV, HEAD_DIM), lambda b, c, pt: (b, c, 0, 0)),
    )
    return pl.pallas_call(
        _compress_pages_kernel,
        grid_spec=grid_spec,
        out_shape=jax.ShapeDtypeStruct((bd, n_pages * bpp, 2 * N_KV, HEAD_DIM), F32),
        compiler_params=_cparams(2),
        name="compress_pages",
    )(page_table, *([cache4] * PAGES_PER_STEP), w_exp3)


def _softmax_parts(s, mask):
    sm = jnp.where(mask, s, NEG)
    m = jnp.max(sm, axis=-1, keepdims=True)
    p = jnp.where(mask, jnp.exp(sm - m), 0.0)
    return p, m, jnp.sum(p, axis=-1, keepdims=True)


def _safe_inv(l):
    return jnp.where(l > 0.0, 1.0 / l, 0.0)


def _topk_mask(score, n_f, k):
    sel = jnp.zeros(score.shape, F32)
    big = jnp.float32(score.shape[-1] + 1)
    for _ in range(k):
        mx = jnp.max(score, axis=-1, keepdims=True)
        idx = jnp.min(jnp.where(score == mx, n_f, big), axis=-1, keepdims=True)
        hit = n_f == idx
        sel = jnp.where(hit, 1.0, sel)
        score = jnp.where(hit, -jnp.inf, score)
    return sel


def _select_blocks(psum, cur, n_i):
    forced = ((n_i == 0) | (n_i == cur) | (n_i == cur - 1)).astype(F32)
    valid = n_i <= cur
    score = jnp.where(valid, psum + FORCE * forced, NEG)
    sel = _topk_mask(score, n_i.astype(F32), min(N_SEL, psum.shape[-1]))
    return jnp.where(valid, sel, 0.0)


def _q_heads(uq_ref, g, gq, cos2, sin2, scale=None):
    hs = []
    for r in range(GRP):
        c0 = (g * GRP + r) * HEAD_DIM
        h = _norm_rope(uq_ref[:, c0:c0 + HEAD_DIM], gq, cos2, sin2)
        hs.append(h if scale is None else h * scale)
    return jnp.concatenate(hs, axis=0).astype(BF16)


def _tile_rows(x, reps):
    return jnp.concatenate([x] * reps, axis=0)


def _gate_and_store(o_ref, g, outs, z_refs, gate, t):
    for r in range(GRP):
        h = g * GRP + r
        acc = None
        for br in range(N_BRANCH):
            zg = jax.nn.silu(z_refs[br][:, h * HEAD_DIM:(h + 1) * HEAD_DIM]) \
                * gate[:, br * N_HEADS + h:br * N_HEADS + h + 1]
            term = outs[br][r * t:(r + 1) * t, :] * zg
            acc = term if acc is None else acc + term
        o_ref[:, h * HEAD_DIM:(h + 1) * HEAD_DIM] = acc.astype(o_ref.dtype)


def _rank_select_t(score_t, n_col, k):
    rank = jnp.zeros(score_t.shape, F32)
    for m in range(score_t.shape[0]):
        row = score_t[m:m + 1, :]
        tie = (n_col > m).astype(F32)
        rank = rank + jnp.where(row > score_t, 1.0, 0.0) + jnp.where(row == score_t, tie, 0.0)
    return jnp.where(rank < k, 1.0, 0.0)


def _prompt_attn_kernel(uq_ref, z0_ref, z1_ref, z2_ref, gate_ref, kvc_ref, ks_ref, vst_ref,
                        kw_ref, vwt_ref, cos_ref, sin_ref, gq_ref, o_ref,
                        q_scr, s_scr, bias_scr, mrun_scr, lrun_scr, ot_scr, ocmp_scr, *, seq, spk):
    i = pl.program_id(1)
    t = BLK
    lanes = GRP * t
    nb = seq // BLK
    nch = seq // KCH
    nwc = min(WINDOW // KCH + 1, nch)
    bpc = KCH // BLK
    k_in_blk = lax.broadcasted_iota(jnp.int32, (BLK, 1), 0)
    cos2 = cos_ref[...]
    sin2 = sin_ref[...]
    gq = gq_ref[...]
    gate = jax.nn.sigmoid(gate_ref[...])
    qpos_r = _tile_rows(i * t + lax.broadcasted_iota(jnp.int32, (t, 1), 0), GRP)
    qpos_l = i * t + (lax.broadcasted_iota(jnp.int32, (1, lanes), 1) & (t - 1))
    qpos_h = qpos_l[:, 0:2 * t]
    cur_h = qpos_h // BLK
    n_row = lax.broadcasted_iota(jnp.int32, (1, nb), 1)
    n_col = lax.broadcasted_iota(jnp.int32, (nb, 1), 0)
    cmask = (n_row * BLK + (BLK - 1)) <= qpos_r
    cmask_t = (n_col * BLK + (BLK - 1)) <= qpos_l
    forced_t = ((n_col == 0) | (n_col == cur_h) | (n_col == cur_h - 1)).astype(F32)
    valid_t = n_col <= cur_h
    k_in_chunk = lax.broadcasted_iota(jnp.int32, (KCH, 1), 0)

    def group_max(x):
        return jnp.max(x.reshape(KCH // 8, 8, lanes), axis=0)

    def group_sum(x):
        return jnp.sum(x.reshape(KCH // 8, 8, lanes), axis=0)

    def finish(lrun, ot):
        l = jnp.sum(lrun, axis=0, keepdims=True)
        return jnp.transpose(ot * _safe_inv(l))

    for g in range(N_KV):
        q = _q_heads(uq_ref, g, gq, cos2, sin2, HEAD_DIM ** -0.5)
        ksl = slice(g * HEAD_DIM, (g + 1) * HEAD_DIM)
        kc = kvc_ref[0, :, ksl].astype(BF16)
        vc = kvc_ref[0, :, KVW + g * HEAD_DIM:KVW + (g + 1) * HEAD_DIM].astype(BF16)

        p, _, l = _softmax_parts(_dot_nt(q, kc), cmask)
        o_cmp = _dot((p * _safe_inv(l)).astype(BF16), vc)

        st = jnp.where(cmask_t, _dot_nt(kc, q), NEG)
        pt = jnp.where(cmask_t, jnp.exp(st - jnp.max(st, axis=0, keepdims=True)), 0.0)
        pt = pt * _safe_inv(jnp.sum(pt, axis=0, keepdims=True))
        half = pt[:, 0:2 * t] + pt[:, 2 * t:4 * t]
        psum_t = half + pltpu.roll(half, t, 1)
        score_t = jnp.where(valid_t, psum_t + FORCE * forced_t, NEG)
        sel_t = jnp.where(valid_t, _rank_select_t(score_t, n_col, min(N_SEL, nb)), 0.0)
        bias_t = (sel_t - 1.0) * (-NEG)
        bias_scr[g] = jnp.concatenate([bias_t, bias_t], axis=1)

        q_scr[g] = q
        ocmp_scr[g] = o_cmp
        mrun_scr[g] = jnp.full((8, lanes), NEG, F32)
        lrun_scr[g] = jnp.zeros((8, lanes), F32)
        ot_scr[g] = jnp.zeros((HEAD_DIM, lanes), F32)

    def sel_pass1(sc):
        for g in range(N_KV):
            ksl = slice(g * HEAD_DIM, (g + 1) * HEAD_DIM)
            st = _dot_nt(ks_ref[0, sc * spk:(sc + 1) * spk, ksl], q_scr[g])
            mx = mrun_scr[g]
            for j in range(spk // BLK):
                n = sc * (spk // BLK) + j
                blk = st[j * BLK:(j + 1) * BLK] + bias_scr[g, n:n + 1, :]
                blk = jnp.where(n * BLK + k_in_blk <= qpos_l, blk, NEG)
                s_scr[g, n // bpc, (n % bpc) * BLK:(n % bpc + 1) * BLK, :] = blk
                mx = jnp.maximum(mx, jnp.max(blk.reshape(BLK // 8, 8, lanes), axis=0))
            mrun_scr[g] = mx

    def sel_pass2(sc):
        for g in range(N_KV):
            ksl = slice(g * HEAD_DIM, (g + 1) * HEAD_DIM)
            m = jnp.max(mrun_scr[g], axis=0, keepdims=True)
            lrun = lrun_scr[g]
            ot = ot_scr[g]
            for c in range(sc * (spk // KCH), (sc + 1) * (spk // KCH)):
                p = jnp.exp(s_scr[g, c] - m)
                lrun = lrun + group_sum(p)
                ot = ot + _dot(vst_ref[c, ksl, :], p.astype(BF16))
            lrun_scr[g] = lrun
            ot_scr[g] = ot

    for sweep in (sel_pass1, sel_pass2):
        for sc in range(seq // spk):
            if sc == 0:
                sweep(0)
            else:
                pl.when(sc * spk <= i * t)(functools.partial(sweep, sc))

    cs = jnp.minimum(jnp.maximum(i - WINDOW // BLK, 0) // bpc, nch - nwc)
    for g in range(N_KV):
        ksl = slice(g * HEAD_DIM, (g + 1) * HEAD_DIM)
        q = q_scr[g]
        o_sel = finish(lrun_scr[g], ot_scr[g])
        mrun = jnp.full((8, lanes), NEG, F32)
        for j in range(nwc):
            c = cs + j
            k = kw_ref[0, pl.ds(pl.multiple_of(c * KCH, KCH), KCH), ksl]
            dpos = qpos_l - (c * KCH + k_in_chunk)
            st = jnp.where((dpos >= 0) & (dpos < WINDOW), _dot_nt(k, q), NEG)
            s_scr[g, j] = st
            mrun = jnp.maximum(mrun, group_max(st))
        m = jnp.max(mrun, axis=0, keepdims=True)
        lrun = jnp.zeros((8, lanes), F32)
        ot = jnp.zeros((HEAD_DIM, lanes), F32)
        for j in range(nwc):
            p = jnp.exp(s_scr[g, j] - m)
            lrun = lrun + group_sum(p)
            ot = ot + _dot(vwt_ref[cs + j, ksl, :], p.astype(BF16))
        o_win = finish(lrun, ot)

        _gate_and_store(o_ref, g, (ocmp_scr[g], o_sel, o_win), (z0_ref, z1_ref, z2_ref), gate, t)


def _prompt_attn(u, gate, kvc, ks, vst, kw, vwt, cos2, sin2, gq, batch, seq):
    n = u.shape[0]
    t = BLK
    spk = _pick(seq, (SPAN_KEYS, KCH))
    assert GRP * t == 2 * KCH and seq % KCH == 0 and spk % KCH == 0
    nqb = seq // t
    nb = seq // BLK
    nch = seq // KCH
    lanes = GRP * t
    row = lambda b, i: (b * nqb + i, 0)
    keys = pl.BlockSpec((1, seq, KVW), lambda b, i: (b, 0, 0))
    vals_t = pl.BlockSpec((nch, KVW, KCH), lambda b, i: (b, 0, 0))
    return pl.pallas_call(
        functools.partial(_prompt_attn_kernel, seq=seq, spk=spk),
        grid=(batch, nqb),
        in_specs=[pl.BlockSpec((t, HD), row),
                  pl.BlockSpec((t, HD), lambda b, i: (b * nqb + i, 1)),
                  pl.BlockSpec((t, HD), lambda b, i: (b * nqb + i, 2)),
                  pl.BlockSpec((t, HD), lambda b, i: (b * nqb + i, 3)),
                  pl.BlockSpec((t, HEAD_DIM), row),
                  pl.BlockSpec((1, nb, 2 * KVW), lambda b, i: (b, 0, 0)),
                  keys, vals_t, keys, vals_t,
                  pl.BlockSpec((t, HEAD_DIM), lambda b, i: (i, 0)),
                  pl.BlockSpec((t, HEAD_DIM), lambda b, i: (i, 0)),
                  pl.BlockSpec((1, HEAD_DIM), lambda b, i: (0, 0))],
        out_specs=pl.BlockSpec((t, HD), row),
        out_shape=jax.ShapeDtypeStruct((n, HD), BF16),
        scratch_shapes=[pltpu.VMEM((N_KV, lanes, HEAD_DIM), BF16),
                        pltpu.VMEM((N_KV, nch, KCH, lanes), F32),
                        pltpu.VMEM((N_KV, nb, lanes), F32),
                        pltpu.VMEM((N_KV, 8, lanes), F32),
                        pltpu.VMEM((N_KV, 8, lanes), F32),
                        pltpu.VMEM((N_KV, HEAD_DIM, lanes), F32),
                        pltpu.VMEM((N_KV, lanes, HEAD_DIM), F32)],
        compiler_params=_cparams(2),
        name="prompt_attn",
    )(u, u, u, u, gate, kvc.reshape(batch, nb, 2 * KVW), ks.reshape(batch, seq, KVW), vst,
      kw.reshape(batch, seq, KVW), vwt, cos2, sin2, gq)


def _decode_front_kernel(uq_ref, kvc_ref, swin_ref, nwin_ref, cos_ref, sin_ref, gq_ref,
                         qn_ref, sel_ref, ocmp_ref, owin_ref, *, past, t):
    scale = HEAD_DIM ** -0.5
    cos2 = cos_ref[...]
    sin2 = sin_ref[...]
    gq = gq_ref[...]
    nbp = kvc_ref.shape[1]
    wbuf = swin_ref.shape[1]
    qpos = past + lax.broadcasted_iota(jnp.int32, (t, 1), 0)
    qpos_r = _tile_rows(qpos, GRP)
    cur = qpos // BLK
    n_i = lax.broadcasted_iota(jnp.int32, (1, nbp), 1)
    cmask = (n_i * BLK + (BLK - 1)) <= qpos_r
    kwpos_old = (past - wbuf) + lax.broadcasted_iota(jnp.int32, (1, wbuf), 1)
    kwpos_new = past + lax.broadcasted_iota(jnp.int32, (1, NEW_PAD), 1)
    d_old = qpos_r - kwpos_old
    d_new = qpos_r - kwpos_new
    m_old = (d_old >= 0) & (d_old < WINDOW) & (kwpos_old >= 0)
    m_new = (d_new >= 0) & (d_new < WINDOW) & (kwpos_new < past + t)

    for g in range(N_KV):
        q = _q_heads(uq_ref, g, gq, cos2, sin2)
        rows = slice(g * GRP * t, (g + 1) * GRP * t)
        ksl = slice(g * HEAD_DIM, (g + 1) * HEAD_DIM)
        vsl = slice(KVW + g * HEAD_DIM, KVW + (g + 1) * HEAD_DIM)
        qn_ref[0, rows, :] = q

        s = _dot_nt(q, kvc_ref[0, :, ksl].astype(BF16)) * scale
        p, _, l = _softmax_parts(s, cmask)
        p = p * _safe_inv(l)
        ocmp_ref[0, rows, :] = _dot(p.astype(BF16), kvc_ref[0, :, vsl].astype(BF16))

        psum = p[0:t]
        for r in range(1, GRP):
            psum = psum + p[r * t:(r + 1) * t]
        sel_ref[0, g * t:(g + 1) * t, :] = _select_blocks(psum, cur, n_i)

        s_old = _dot_nt(q, swin_ref[0, :, ksl].astype(BF16)) * scale
        s_new = _dot_nt(q, nwin_ref[0, :, ksl].astype(BF16)) * scale
        mx = jnp.maximum(jnp.max(jnp.where(m_old, s_old, NEG), axis=-1, keepdims=True),
                         jnp.max(jnp.where(m_new, s_new, NEG), axis=-1, keepdims=True))
        p_old = jnp.where(m_old, jnp.exp(jnp.where(m_old, s_old, NEG) - mx), 0.0)
        p_new = jnp.where(m_new, jnp.exp(jnp.where(m_new, s_new, NEG) - mx), 0.0)
        l = jnp.sum(p_old, axis=-1, keepdims=True) + jnp.sum(p_new, axis=-1, keepdims=True)
        o = _dot(p_old.astype(BF16), swin_ref[0, :, vsl].astype(BF16)) \
            + _dot(p_new.astype(BF16), nwin_ref[0, :, vsl].astype(BF16))
        owin_ref[0, rows, :] = o * _safe_inv(l)


def _decode_front(u, kvc_all, state_win2, new_win, cos2, sin2, gq, bd, t, past):
    nbp = kvc_all.shape[1]
    wbuf = state_win2.shape[1]
    rows = N_HEADS * t
    return pl.pallas_call(
        functools.partial(_decode_front_kernel, past=past, t=t),
        grid=(bd,),
        in_specs=[pl.BlockSpec((t, HD), lambda b: (b, 0)),
                  pl.BlockSpec((1, nbp, 2 * KVW), lambda b: (b, 0, 0)),
                  pl.BlockSpec((1, wbuf, 2 * KVW), lambda b: (b, 0, 0)),
                  pl.BlockSpec((1, NEW_PAD, 2 * KVW), lambda b: (b, 0, 0)),
                  pl.BlockSpec((t, HEAD_DIM), lambda b: (0, 0)),
                  pl.BlockSpec((t, HEAD_DIM), lambda b: (0, 0)),
                  pl.BlockSpec((1, HEAD_DIM), lambda b: (0, 0))],
        out_specs=[pl.BlockSpec((1, rows, HEAD_DIM), lambda b: (b, 0, 0)),
                   pl.BlockSpec((1, N_KV * t, nbp), lambda b: (b, 0, 0)),
                   pl.BlockSpec((1, rows, HEAD_DIM), lambda b: (b, 0, 0)),
                   pl.BlockSpec((1, rows, HEAD_DIM), lambda b: (b, 0, 0))],
        out_shape=[jax.ShapeDtypeStruct((bd, rows, HEAD_DIM), BF16),
                   jax.ShapeDtypeStruct((bd, N_KV * t, nbp), F32),
                   jax.ShapeDtypeStruct((bd, rows, HEAD_DIM), F32),
                   jax.ShapeDtypeStruct((bd, rows, HEAD_DIM), F32)],
        compiler_params=_cparams(1),
        name="decode_front",
    )(u, kvc_all, state_win2, new_win, cos2, sin2, gq)


def _decode_sel_kernel(pt_ref, *refs, t, n_steps):
    del pt_ref
    np_ = PAGES_PER_STEP
    (qn_ref, selc_ref, seln_ref) = refs[:3]
    page_refs = refs[3:3 + np_]
    (new_ref, ocmp_ref, owin_ref, z0_ref, z1_ref, z2_ref, gate_ref, e_ref,
     o_ref, m_ref, l_ref, acc_ref) = refs[3 + np_:]
    c = pl.program_id(1)
    scale = HEAD_DIM ** -0.5
    gt = GRP * t

    @pl.when(c == 0)
    def _():
        m_ref[...] = jnp.full(m_ref.shape, NEG, F32)
        l_ref[...] = jnp.zeros(l_ref.shape, F32)
        acc_ref[...] = jnp.zeros(acc_ref.shape, F32)

    def update(keys_of, vals_of, sel_rows, extra_mask=None):
        s = jnp.concatenate([_dot_nt(qn_ref[0, g * gt:(g + 1) * gt, :], keys_of(g))
                             for g in range(N_KV)], axis=0) * scale
        mask = jnp.concatenate([sel_rows[g * t:(g + 1) * t] for g in range(N_KV) for _ in range(GRP)],
                               axis=0) > 0.5
        if extra_mask is not None:
            mask = mask & extra_mask
        sm = jnp.where(mask, s, NEG)
        m_old = m_ref[...]
        m_new = jnp.maximum(m_old, jnp.max(sm, axis=-1, keepdims=True))
        alpha = jnp.exp(m_old - m_new)
        p = jnp.where(mask, jnp.exp(sm - m_new), 0.0)
        l_ref[...] = alpha * l_ref[...] + jnp.sum(p, axis=-1, keepdims=True)
        pb = p.astype(BF16)
        pv = jnp.concatenate([_dot(pb[g * gt:(g + 1) * gt], vals_of(g)) for g in range(N_KV)], axis=0)
        acc_ref[...] = alpha * acc_ref[...] + pv
        m_ref[...] = m_new

    def page_rows(j):
        return jnp.concatenate([pr[0, :, j, :] for pr in page_refs], axis=0).astype(BF16)

    sel_keys = _dot(selc_ref[0, 0].astype(BF16), e_ref[...])
    update(page_rows, lambda g: page_rows(N_KV + g), sel_keys)

    @pl.when(c == n_steps - 1)
    def _():
        gate = jax.nn.sigmoid(gate_ref[...])
        tq = _tile_rows(lax.broadcasted_iota(jnp.int32, (t, 1), 0), N_HEADS)
        causal = lax.broadcasted_iota(jnp.int32, (1, NEW_PAD), 1) <= tq
        sel_new = jnp.broadcast_to(seln_ref[0, 0, :, 0:1], (N_KV * t, NEW_PAD))
        update(lambda g: new_ref[0, :, g * HEAD_DIM:(g + 1) * HEAD_DIM].astype(BF16),
               lambda g: new_ref[0, :, KVW + g * HEAD_DIM:KVW + (g + 1) * HEAD_DIM].astype(BF16),
               sel_new, causal)
        o_sel = acc_ref[...] * _safe_inv(l_ref[...])
        for g in range(N_KV):
            rows = slice(g * gt, (g + 1) * gt)
            _gate_and_store(o_ref, g, (ocmp_ref[0, rows, :], o_sel[rows], owin_ref[0, rows, :]),
                            (z0_ref, z1_ref, z2_ref), gate, t)


def _decode_sel(page_table, qn, selc, cache4, new_sel, ocmp, owin, u, gate, e8, bd, t):
    n_pages = page_table.shape[1]
    page = cache4.shape[1]
    n_steps = n_pages // PAGES_PER_STEP
    bps = PAGES_PER_STEP * (page // BLK)
    rows = N_HEADS * t

    def page_spec(k):
        return pl.BlockSpec((1, page, 2 * N_KV, HEAD_DIM),
                            lambda b, c, pt: (pt[b, c * PAGES_PER_STEP + k], 0, 1, 0))

    grid_spec = pltpu.PrefetchScalarGridSpec(
        num_scalar_prefetch=1,
        grid=(bd, n_steps),
        in_specs=[pl.BlockSpec((1, rows, HEAD_DIM), lambda b, c, pt: (b, 0, 0)),
                  pl.BlockSpec((1, 1, N_KV * t, bps), lambda b, c, pt: (b, c, 0, 0)),
                  pl.BlockSpec((1, 1, N_KV * t, bps), lambda b, c, pt: (b, n_steps, 0, 0))]
        + [page_spec(k) for k in range(PAGES_PER_STEP)]
        + [pl.BlockSpec((1, NEW_PAD, 2 * KVW), lambda b, c, pt: (b, 0, 0)),
           pl.BlockSpec((1, rows, HEAD_DIM), lambda b, c, pt: (b, 0, 0)),
           pl.BlockSpec((1, rows, HEAD_DIM), lambda b, c, pt: (b, 0, 0)),
           pl.BlockSpec((t, HD), lambda b, c, pt: (b, 1)),
           pl.BlockSpec((t, HD), lambda b, c, pt: (b, 2)),
           pl.BlockSpec((t, HD), lambda b, c, pt: (b, 3)),
           pl.BlockSpec((t, HEAD_DIM), lambda b, c, pt: (b, 0)),
           pl.BlockSpec((bps, bps * BLK), lambda b, c, pt: (0, 0))],
        out_specs=pl.BlockSpec((t, HD), lambda b, c, pt: (b, 0)),
        scratch_shapes=[pltpu.VMEM((rows, 1), F32), pltpu.VMEM((rows, 1), F32),
                        pltpu.VMEM((rows, HEAD_DIM), F32)],
    )
    return pl.pallas_call(
        functools.partial(_decode_sel_kernel, t=t, n_steps=n_steps),
        grid_spec=grid_spec,
        out_shape=jax.ShapeDtypeStruct((bd * t, HD), BF16),
        compiler_params=_cparams(2),
        name="decode_sel",
    )(page_table, qn, selc, selc, *([cache4] * PAGES_PER_STEP), new_sel, ocmp, owin,
      u, u, u, gate, e8)


def _rope_tables(pos):
    half = HEAD_DIM // 2
    inv = ROPE_THETA ** (-jnp.arange(half, dtype=F32) / half)
    ang = pos.astype(F32)[:, None] * inv[None, :]
    c = jnp.cos(ang)
    s = jnp.sin(ang)
    return jnp.concatenate([c, c], axis=-1), jnp.concatenate([-s, s], axis=-1)


def _block_expand(nblocks):
    key_blk = jnp.arange(nblocks * BLK, dtype=jnp.int32) // BLK
    return (key_blk[None, :] == jnp.arange(nblocks, dtype=jnp.int32)[:, None]).astype(BF16)


def _conv_layers(x, prev, seq, g_a, w_in_a, conv_w, w_out_a):
    states = []
    ch = conv_w.shape[-1]
    for layer in range(N_A):
        xn = _rms_cast(x, g_a[layer])
        u = _matmul(xn, w_in_a, layer, 4 * ch)
        y, st = _conv_gate(u, prev[layer], conv_w[layer], seq)
        x = _matmul(y, w_out_a, layer, x.shape[1], res=x)
        states.append(st)
    return x, jnp.stack(states)


def _shared_kv(x, pos, seq, g_kv, w_kv, g_k, for_attn):
    cos2, sin2 = _rope_tables(pos)
    kv = _matmul(_rms_cast(x, g_kv), w_kv, 0, 6 * KVW)
    return _kv_post(kv, cos2, sin2, g_k, seq, for_attn), cos2, sin2


def _query_side(x, lb, g_b, w_qz, w_gate):
    xn = _rms_cast(x, g_b[lb])
    u = _matmul(xn, w_qz, lb, 4 * HD)
    gate = _matmul(xn, w_gate[lb], 0, HEAD_DIM)
    return u, gate


def kernel(x_prompt, x_sample, state_conv, cache_kv, state_win, page_table, g_a, w_in_a, conv_w,
           w_out_a, g_kv, w_kv, g_k, w_cmp, g_b, w_qz, g_q, w_out_b):
    batch, seq, d = x_prompt.shape
    bd, t, _ = x_sample.shape
    ch = conv_w.shape[-1]
    n_pool, page = cache_kv.shape[:2]
    n_pages = page_table.shape[1]
    past = n_pages * page
    wbuf = state_win.shape[1]
    assert seq % BLK == 0 and page % BLK == 0 and t <= BLK and n_pages % PAGES_PER_STEP == 0

    w_exp3 = jnp.broadcast_to(w_cmp.reshape(BLK, 2 * N_KV, 1), (BLK, 2 * N_KV, HEAD_DIM))
    w_exp = w_exp3.reshape(BLK, 2 * KVW)
    n_gate = N_BRANCH * N_HEADS
    w_gate = jnp.pad(w_qz[:, :, 4 * HD:4 * HD + n_gate], ((0, 0), (0, 0), (0, HEAD_DIM - n_gate)))

    xp = x_prompt.reshape(batch * seq, d)
    xp, conv_p = _conv_layers(xp, jnp.zeros((N_A, batch, CONV_W - 1, ch), F32), seq,
                              g_a, w_in_a, conv_w, w_out_a)
    pos_p = jnp.arange(seq, dtype=jnp.int32)
    (rows_p, win_p, ks_p, vst_p, kw_p, vwt_p), cos_p, sin_p = _shared_kv(
        xp, pos_p, seq, g_kv, w_kv, g_k, True)
    kvc_p = _compress(rows_p, w_exp)
    for lb in range(N_B):
        u, gate = _query_side(xp, lb, g_b, w_qz, w_gate)
        o = _prompt_attn(u, gate, kvc_p, ks_p, vst_p, kw_p, vwt_p, cos_p, sin_p, g_q[lb:lb + 1],
                         batch, seq)
        xp = _matmul(o, w_out_b, lb, d, res=xp)
    wn = min(WINDOW, seq)
    y_prompt = xp.reshape(batch, seq, d)
    kv_rows_prompt = rows_p.reshape(batch, seq, 4, N_KV, HEAD_DIM)
    win_prompt = win_p.reshape(batch, seq, 2, N_KV, HEAD_DIM)[:, seq - wn:]

    xs = x_sample.reshape(bd * t, d)
    xs, conv_s = _conv_layers(xs, state_conv, t, g_a, w_in_a, conv_w, w_out_a)
    pos_s = past + jnp.arange(t, dtype=jnp.int32)
    (rows_s, win_s), cos_s, sin_s = _shared_kv(xs, pos_s, t, g_kv, w_kv, g_k, False)
    cache4 = cache_kv.reshape(n_pool, page, 4 * N_KV, HEAD_DIM)
    kvc_past = _compress_pages(cache4, page_table, w_exp3).reshape(bd, past // BLK, 2 * KVW)
    rows_s3 = rows_s.reshape(bd, t, 4 * KVW)
    rows_pad = jnp.pad(rows_s3, ((0, 0), (0, NEW_PAD - t), (0, 0)))
    kvc_new = _compress(rows_pad[:, :BLK, :2 * KVW].reshape(bd * BLK, 2 * KVW), w_exp)
    nb_past = past // BLK
    bps = PAGES_PER_STEP * (page // BLK)
    nbp = nb_past + bps
    kvc_all = jnp.concatenate(
        [kvc_past, kvc_new[:, None, :], jnp.zeros((bd, bps - 1, 2 * KVW), F32)], axis=1)
    new_sel = rows_pad[:, :, 2 * KVW:]
    new_win = jnp.pad(win_s.reshape(bd, t, 2 * KVW), ((0, 0), (0, NEW_PAD - t), (0, 0)))
    state_win2 = state_win.reshape(bd, wbuf, 2 * KVW)
    e8 = _block_expand(bps)
    for lb in range(N_B):
        u, gate = _query_side(xs, lb, g_b, w_qz, w_gate)
        qn, sel, ocmp, owin = _decode_front(u, kvc_all, state_win2, new_win, cos_s, sin_s,
                                            g_q[lb:lb + 1], bd, t, past)
        selc = sel.reshape(bd, N_KV * t, nbp // bps, bps).transpose(0, 2, 1, 3)
        o = _decode_sel(page_table, qn, selc, cache4, new_sel, ocmp, owin, u, gate, e8, bd, t)
        xs = _matmul(o, w_out_b, lb, d, res=xs)
    y_sample = xs.reshape(bd, t, d)
    kv_rows_sample = rows_s.reshape(bd, t, 4, N_KV, HEAD_DIM)
    kvw = jnp.concatenate([state_win, win_s.reshape(bd, t, 2, N_KV, HEAD_DIM)], axis=1)
    win_sample = kvw[:, kvw.shape[1] - min(WINDOW, kvw.shape[1]):]

    return (y_prompt, y_sample, conv_p, conv_s, kv_rows_prompt, kv_rows_sample, win_prompt, win_sample)
```

```python
import functools

import jax
import jax.numpy as jnp
from jax import lax
from jax.experimental import pallas as pl
from jax.experimental.pallas import tpu as pltpu

N_A = 2
N_B = 2
CONV_W = 3
N_HEADS = 16
N_KV = 4
GRP = N_HEADS // N_KV
HEAD_DIM = 128
N_BRANCH = 3
BLK = 64
N_SEL = 16
WINDOW = 512
ROPE_THETA = 10000.0
EPS = 1e-6
NEG = -1e30
FORCE = 1e3

HD = N_HEADS * HEAD_DIM
KVW = N_KV * HEAD_DIM
NEW_PAD = 128
KCH = 128
SPAN_KEYS = 512
PAGES_PER_STEP = 8
VMEM_LIMIT = 56 * 1024 * 1024

F32 = jnp.float32
BF16 = jnp.bfloat16


def _cparams(n_axes):
    return pltpu.CompilerParams(
        dimension_semantics=("arbitrary",) * n_axes, vmem_limit_bytes=VMEM_LIMIT)


def _dot_nt(a, b):
    return lax.dot_general(a, b, (((1,), (1,)), ((), ())), preferred_element_type=F32)


def _dot(a, b):
    return jnp.dot(a, b, preferred_element_type=F32)


def _pick(n, prefs):
    for p in prefs:
        if n % p == 0:
            return p
    return n


def _rms_cast_kernel(x_ref, g_ref, o_ref):
    x = x_ref[...]
    y = x * lax.rsqrt(jnp.mean(x * x, axis=-1, keepdims=True) + EPS)
    o_ref[...] = (y * g_ref[...]).astype(BF16)


def _rms_cast(x, g):
    n, d = x.shape
    tm = _pick(n, (512, 256, 128, 64, 8))
    return pl.pallas_call(
        _rms_cast_kernel,
        grid=(n // tm,),
        in_specs=[pl.BlockSpec((tm, d), lambda i: (i, 0)),
                  pl.BlockSpec((1, d), lambda i: (0, 0))],
        out_specs=pl.BlockSpec((tm, d), lambda i: (i, 0)),
        out_shape=jax.ShapeDtypeStruct((n, d), BF16),
        compiler_params=_cparams(1),
        name="rms_cast",
    )(x, g.reshape(1, d))


def _mm_kernel(*refs, has_res):
    if has_res:
        x_ref, w_ref, r_ref, o_ref, wb_ref = refs
    else:
        x_ref, w_ref, o_ref, wb_ref = refs

    @pl.when(pl.program_id(1) == 0)
    def _():
        wb_ref[...] = w_ref[...].astype(BF16)

    acc = _dot(x_ref[...], wb_ref[...])
    if has_res:
        acc = r_ref[...] + acc
    o_ref[...] = acc


def _matmul(x, w, layer, m_out, res=None):
    n, k = x.shape
    tm = _pick(n, (1024, 512, 256, 128, 64, 8))
    tn = _pick(m_out, (1024, 512, 256, 128))
    if w.ndim == 3:
        w_spec = pl.BlockSpec((None, k, tn), lambda j, i: (layer, 0, j))
    else:
        w_spec = pl.BlockSpec((k, tn), lambda j, i: (0, j))
    in_specs = [pl.BlockSpec((tm, k), lambda j, i: (i, 0)), w_spec]
    args = [x, w]
    if res is not None:
        in_specs.append(pl.BlockSpec((tm, tn), lambda j, i: (i, j)))
        args.append(res)
    return pl.pallas_call(
        functools.partial(_mm_kernel, has_res=res is not None),
        grid=(m_out // tn, n // tm),
        in_specs=in_specs,
        out_specs=pl.BlockSpec((tm, tn), lambda j, i: (i, j)),
        out_shape=jax.ShapeDtypeStruct((n, m_out), F32),
        scratch_shapes=[pltpu.VMEM((k, tn), BF16)],
        compiler_params=_cparams(2),
        name="matmul_res" if res is not None else "matmul",
    )(*args)


def _mm_res_norm_kernel(x_ref, w_ref, r_ref, *rest, n_norm):
    g_refs = rest[:n_norm]
    o_ref = rest[n_norm]
    xn_refs = rest[n_norm + 1:2 * n_norm + 1]
    wb_ref = rest[2 * n_norm + 1]

    @pl.when(pl.program_id(0) == 0)
    def _():
        wb_ref[...] = w_ref[...].astype(BF16)

    out = r_ref[...] + _dot(x_ref[...], wb_ref[...])
    o_ref[...] = out
    y = out * lax.rsqrt(jnp.mean(out * out, axis=-1, keepdims=True) + EPS)
    for g_ref, xn_ref in zip(g_refs, xn_refs):
        xn_ref[...] = (y * g_ref[...]).astype(BF16)


def _matmul_res_norm(x, w, layer, res, gains):
    n, k = x.shape
    d = res.shape[1]
    tm = _pick(n, (512, 256, 128, 64, 8))
    row = lambda i: (i, 0)
    n_norm = len(gains)
    return pl.pallas_call(
        functools.partial(_mm_res_norm_kernel, n_norm=n_norm),
        grid=(n // tm,),
        in_specs=[pl.BlockSpec((tm, k), row),
                  pl.BlockSpec((None, k, d), lambda i: (layer, 0, 0), pipeline_mode=pl.Buffered(1)),
                  pl.BlockSpec((tm, d), row)]
        + [pl.BlockSpec((1, d), lambda i: (0, 0))] * n_norm,
        out_specs=[pl.BlockSpec((tm, d), row)] * (1 + n_norm),
        out_shape=[jax.ShapeDtypeStruct((n, d), F32)] + [jax.ShapeDtypeStruct((n, d), BF16)] * n_norm,
        scratch_shapes=[pltpu.VMEM((k, d), BF16)],
        compiler_params=_cparams(1),
        name="matmul_res_norm",
    )(x, w, res, *[g.reshape(1, d) for g in gains])


def _conv_mm_kernel(x_ref, wb_ref, wc_ref, wh_ref, wz_ref, prev_ref, cw_ref, y_ref, st_ref,
                    wbf_ref, carry_ref, *, tps):
    i = pl.program_id(1)

    @pl.when(i == 0)
    def _():
        for k, wr in enumerate((wb_ref, wc_ref, wh_ref, wz_ref)):
            wbf_ref[k] = wr[...].astype(BF16)

    x = x_ref[...]
    c = _dot(x, wbf_ref[1]) * _dot(x, wbf_ref[2])
    y_ref[...] = _gated_conv(c, _dot(x, wbf_ref[0]), _dot(x, wbf_ref[3]), i, tps,
                             prev_ref, cw_ref, st_ref, carry_ref).astype(BF16)


def _conv_matmul(xn, w_in, layer, prev, w_conv, seq):
    n, k = xn.shape
    ch = w_in.shape[2] // 4
    tm = _pick(seq, (1024, 512, 256, 128, 64, 8))
    tc = _pick(ch, (256, 128))
    tps = seq // tm
    nc = ch // tc

    def w_spec(part):
        return pl.BlockSpec((None, k, tc), lambda j, i: (layer, 0, part * nc + j))

    return pl.pallas_call(
        functools.partial(_conv_mm_kernel, tps=tps),
        grid=(nc, n // tm),
        in_specs=[pl.BlockSpec((tm, k), lambda j, i: (i, 0))] + [w_spec(p) for p in range(4)]
        + [pl.BlockSpec((1, 2, tc), lambda j, i: (i // tps, 0, j)),
           pl.BlockSpec((CONV_W, tc), lambda j, i: (0, j))],
        out_specs=[pl.BlockSpec((tm, tc), lambda j, i: (i, j)),
                   pl.BlockSpec((1, 2, tc), lambda j, i: (i // tps, 0, j))],
        out_shape=[jax.ShapeDtypeStruct((n, ch), BF16),
                   jax.ShapeDtypeStruct((n // seq, CONV_W - 1, ch), F32)],
        scratch_shapes=[pltpu.VMEM((4, k, tc), BF16), pltpu.VMEM((8, tc), F32)],
        compiler_params=_cparams(2),
        name="conv_matmul",
    )(xn, w_in, w_in, w_in, w_in, prev, w_conv)


def _gated_conv(c, b, z, i, tps, prev_ref, w_ref, st_ref, carry_ref):
    tm = c.shape[0]

    @pl.when(i % tps == 0)
    def _():
        carry_ref[0:2, :] = prev_ref[0]

    p0 = carry_ref[0:1, :]
    p1 = carry_ref[1:2, :]
    row = lax.broadcasted_iota(jnp.int32, c.shape, 0)
    c1 = jnp.where(row == 0, p1, pltpu.roll(c, 1, 0))
    c2 = jnp.where(row == 0, p0, jnp.where(row == 1, p1, pltpu.roll(c, 2, 0)))
    conv = w_ref[0:1, :] * c2 + w_ref[1:2, :] * c1 + w_ref[2:3, :] * c
    last = c[tm - 2:tm, :]
    carry_ref[0:2, :] = last
    st_ref[0] = last
    return jax.nn.silu(z) * (b * conv)


def _conv_gate_kernel(b_ref, cg_ref, h_ref, z_ref, prev_ref, w_ref, y_ref, st_ref, carry_ref, *, tps):
    c = cg_ref[...] * h_ref[...]
    y_ref[...] = _gated_conv(c, b_ref[...], z_ref[...], pl.program_id(1), tps,
                             prev_ref, w_ref, st_ref, carry_ref).astype(BF16)


def _conv_gate(u, prev, w_conv, seq):
    n = u.shape[0]
    ch = u.shape[1] // 4
    tm = _pick(seq, (256, 128, 64, 8))
    tc = _pick(ch, (512, 256, 128))
    tps = seq // tm
    nc = ch // tc
    nb = n // seq
    return pl.pallas_call(
        functools.partial(_conv_gate_kernel, tps=tps),
        grid=(nc, n // tm),
        in_specs=[pl.BlockSpec((tm, tc), lambda j, i: (i, j)),
                  pl.BlockSpec((tm, tc), lambda j, i: (i, nc + j)),
                  pl.BlockSpec((tm, tc), lambda j, i: (i, 2 * nc + j)),
                  pl.BlockSpec((tm, tc), lambda j, i: (i, 3 * nc + j)),
                  pl.BlockSpec((1, 2, tc), lambda j, i: (i // tps, 0, j)),
                  pl.BlockSpec((CONV_W, tc), lambda j, i: (0, j))],
        out_specs=[pl.BlockSpec((tm, tc), lambda j, i: (i, j)),
                   pl.BlockSpec((1, 2, tc), lambda j, i: (i // tps, 0, j))],
        out_shape=[jax.ShapeDtypeStruct((n, ch), BF16),
                   jax.ShapeDtypeStruct((nb, CONV_W - 1, ch), F32)],
        scratch_shapes=[pltpu.VMEM((8, tc), F32)],
        compiler_params=_cparams(2),
        name="conv_gate",
    )(u, u, u, u, prev, w_conv)


def _norm_rope(x, g, cos2, sin2):
    y = x * lax.rsqrt(jnp.mean(x * x, axis=-1, keepdims=True) + EPS) * g
    return y * cos2 + pltpu.roll(y, HEAD_DIM // 2, 1) * sin2


def _kv_post_kernel(kv_ref, cos_ref, sin_ref, gk_ref, rows_ref, win_ref, *attn_refs):
    cos2 = cos_ref[...]
    sin2 = sin_ref[...]
    tm = kv_ref.shape[0]
    for br in range(N_BRANCH):
        base = br * 2 * KVW
        for g in range(N_KV):
            c0 = base + g * HEAD_DIM
            k = _norm_rope(kv_ref[:, c0:c0 + HEAD_DIM], gk_ref[br:br + 1, :], cos2, sin2)
            if br < 2:
                rows_ref[:, c0:c0 + HEAD_DIM] = k
            else:
                win_ref[:, g * HEAD_DIM:(g + 1) * HEAD_DIM] = k
            if attn_refs and br >= 1:
                attn_refs[2 * (br - 1)][:, g * HEAD_DIM:(g + 1) * HEAD_DIM] = k.astype(BF16)
        v = kv_ref[:, base + KVW:base + 2 * KVW]
        if br < 2:
            rows_ref[:, base + KVW:base + 2 * KVW] = v
        else:
            win_ref[:, KVW:2 * KVW] = v
        if attn_refs and br >= 1:
            vt_ref = attn_refs[2 * (br - 1) + 1]
            for a in range(tm // KCH):
                vt_ref[a] = jnp.transpose(v[a * KCH:(a + 1) * KCH, :]).astype(BF16)


def _kv_post(kv, cos2, sin2, g_k, seq, for_attn):
    n = kv.shape[0]
    tm = _pick(seq, (256, 128, 64, 8))
    tps = seq // tm
    out_specs = [pl.BlockSpec((tm, 4 * KVW), lambda i: (i, 0)),
                 pl.BlockSpec((tm, 2 * KVW), lambda i: (i, 0))]
    out_shape = [jax.ShapeDtypeStruct((n, 4 * KVW), F32),
                 jax.ShapeDtypeStruct((n, 2 * KVW), F32)]
    if for_attn:
        assert tm % KCH == 0
        for _ in range(2):
            out_specs += [pl.BlockSpec((tm, KVW), lambda i: (i, 0)),
                          pl.BlockSpec((tm // KCH, KVW, KCH), lambda i: (i, 0, 0))]
            out_shape += [jax.ShapeDtypeStruct((n, KVW), BF16),
                          jax.ShapeDtypeStruct((n // KCH, KVW, KCH), BF16)]
    return pl.pallas_call(
        _kv_post_kernel,
        grid=(n // tm,),
        in_specs=[pl.BlockSpec((tm, 6 * KVW), lambda i: (i, 0)),
                  pl.BlockSpec((tm, HEAD_DIM), lambda i: (i % tps, 0)),
                  pl.BlockSpec((tm, HEAD_DIM), lambda i: (i % tps, 0)),
                  pl.BlockSpec((N_BRANCH, HEAD_DIM), lambda i: (0, 0))],
        out_specs=out_specs,
        out_shape=out_shape,
        compiler_params=_cparams(1),
        name="kv_post",
    )(kv, cos2, sin2, g_k)


def _compress_kernel(rows_ref, w_ref, o_ref):
    x = rows_ref[...]
    nb = x.shape[0] // BLK
    x = x.reshape(nb, BLK, x.shape[1]) * w_ref[...][None]
    o_ref[...] = jnp.sum(x, axis=1)


def _compress(rows, w_exp):
    n = rows.shape[0]
    nblk = n // BLK
    per = _pick(nblk, (8,))
    return pl.pallas_call(
        _compress_kernel,
        grid=(nblk // per,),
        in_specs=[pl.BlockSpec((per * BLK, 2 * KVW), lambda i: (i, 0)),
                  pl.BlockSpec((BLK, 2 * KVW), lambda i: (0, 0))],
        out_specs=pl.BlockSpec((per, 2 * KVW), lambda i: (i, 0)),
        out_shape=jax.ShapeDtypeStruct((nblk, 2 * KVW), F32),
        compiler_params=_cparams(1),
        name="compress",
    )(rows, w_exp)


def _compress_pages_kernel(pt_ref, *refs):
    page_refs = refs[:PAGES_PER_STEP]
    w_ref = refs[PAGES_PER_STEP]
    o_ref = refs[PAGES_PER_STEP + 1]
    w = w_ref[...]
    for k, pr in enumerate(page_refs):
        x = pr[0]
        nb = x.shape[0] // BLK
        o_ref[0, k * nb:(k + 1) * nb] = jnp.sum(x.reshape(nb, BLK, 2 * N_KV, HEAD_DIM) * w[None], axis=1)


def _compress_pages(cache4, page_table, w_exp3):
    bd, n_pages = page_table.shape
    page = cache4.shape[1]
    bpp = page // BLK
    steps = n_pages // PAGES_PER_STEP
    rows_out = PAGES_PER_STEP * bpp

    def page_spec(k):
        return pl.BlockSpec((1, page, 2 * N_KV, HEAD_DIM),
                            lambda b, c, pt: (pt[b, c * PAGES_PER_STEP + k], 0, 0, 0))

    grid_spec = pltpu.PrefetchScalarGridSpec(
        num_scalar_prefetch=1,
        grid=(bd, steps),
        in_specs=[page_spec(k) for k in range(PAGES_PER_STEP)]
        + [pl.BlockSpec((BLK, 2 * N_KV, HEAD_DIM), lambda b, c, pt: (0, 0, 0))],
        out_specs=pl.BlockSpec((1, rows_out, 2 * N_KV, HEAD_DIM), lambda b, c, pt: (b, c, 0, 0)),
    )
    return pl.pallas_call(
        _compress_pages_kernel,
        grid_spec=grid_spec,
        out_shape=jax.ShapeDtypeStruct((bd, n_pages * bpp, 2 * N_KV, HEAD_DIM), F32),
        compiler_params=_cparams(2),
        name="compress_pages",
    )(page_table, *([cache4] * PAGES_PER_STEP), w_exp3)


def _softmax_parts(s, mask):
    sm = jnp.where(mask, s, NEG)
    m = jnp.max(sm, axis=-1, keepdims=True)
    p = jnp.where(mask, jnp.exp(sm - m), 0.0)
    return p, m, jnp.sum(p, axis=-1, keepdims=True)


def _safe_inv(l):
    return jnp.where(l > 0.0, 1.0 / l, 0.0)


def _topk_mask(score, n_f, k):
    sel = jnp.zeros(score.shape, F32)
    big = jnp.float32(score.shape[-1] + 1)
    for _ in range(k):
        mx = jnp.max(score, axis=-1, keepdims=True)
        idx = jnp.min(jnp.where(score == mx, n_f, big), axis=-1, keepdims=True)
        hit = n_f == idx
        sel = jnp.where(hit, 1.0, sel)
        score = jnp.where(hit, -jnp.inf, score)
    return sel


def _select_blocks(psum, cur, n_i):
    forced = ((n_i == 0) | (n_i == cur) | (n_i == cur - 1)).astype(F32)
    valid = n_i <= cur
    score = jnp.where(valid, psum + FORCE * forced, NEG)
    sel = _topk_mask(score, n_i.astype(F32), min(N_SEL, psum.shape[-1]))
    return jnp.where(valid, sel, 0.0)


def _q_heads(uq_ref, g, gq, cos2, sin2, scale=None):
    hs = []
    for r in range(GRP):
        c0 = (g * GRP + r) * HEAD_DIM
        h = _norm_rope(uq_ref[:, c0:c0 + HEAD_DIM], gq, cos2, sin2)
        hs.append(h if scale is None else h * scale)
    return jnp.concatenate(hs, axis=0).astype(BF16)


def _tile_rows(x, reps):
    return jnp.concatenate([x] * reps, axis=0)


def _gate_and_store(o_ref, g, outs, z_refs, gate, t):
    for r in range(GRP):
        h = g * GRP + r
        acc = None
        for br in range(N_BRANCH):
            zg = jax.nn.silu(z_refs[br][:, h * HEAD_DIM:(h + 1) * HEAD_DIM]) \
                * gate[:, br * N_HEADS + h:br * N_HEADS + h + 1]
            term = outs[br][r * t:(r + 1) * t, :] * zg
            acc = term if acc is None else acc + term
        o_ref[:, h * HEAD_DIM:(h + 1) * HEAD_DIM] = acc.astype(o_ref.dtype)


def _rank_select_t(score_t, n_col, k):
    rank = jnp.zeros(score_t.shape, F32)
    for m in range(score_t.shape[0]):
        row = score_t[m:m + 1, :]
        tie = (n_col > m).astype(F32)
        rank = rank + jnp.where(row > score_t, 1.0, 0.0) + jnp.where(row == score_t, tie, 0.0)
    return jnp.where(rank < k, 1.0, 0.0)


def _prompt_attn_kernel(uq_ref, z0_ref, z1_ref, z2_ref, gate_ref, kvc_ref, ks_ref, vst_ref,
                        kw_ref, vwt_ref, cos_ref, sin_ref, gq_ref, o_ref,
                        q_scr, s_scr, bias_scr, mrun_scr, lrun_scr, ot_scr, ocmp_scr, *, seq, spk):
    i = pl.program_id(1)
    t = BLK
    lanes = GRP * t
    nb = seq // BLK
    nch = seq // KCH
    nwc = min(WINDOW // KCH + 1, nch)
    bpc = KCH // BLK
    k_in_blk = lax.broadcasted_iota(jnp.int32, (BLK, 1), 0)
    cos2 = cos_ref[...]
    sin2 = sin_ref[...]
    gq = gq_ref[...]
    gate = jax.nn.sigmoid(gate_ref[...])
    qpos_r = _tile_rows(i * t + lax.broadcasted_iota(jnp.int32, (t, 1), 0), GRP)
    qpos_l = i * t + (lax.broadcasted_iota(jnp.int32, (1, lanes), 1) & (t - 1))
    qpos_h = qpos_l[:, 0:2 * t]
    cur_h = qpos_h // BLK
    n_row = lax.broadcasted_iota(jnp.int32, (1, nb), 1)
    n_col = lax.broadcasted_iota(jnp.int32, (nb, 1), 0)
    cmask = (n_row * BLK + (BLK - 1)) <= qpos_r
    cmask_t = (n_col * BLK + (BLK - 1)) <= qpos_l
    forced_t = ((n_col == 0) | (n_col == cur_h) | (n_col == cur_h - 1)).astype(F32)
    valid_t = n_col <= cur_h
    k_in_chunk = lax.broadcasted_iota(jnp.int32, (KCH, 1), 0)

    def group_max(x):
        return jnp.max(x.reshape(KCH // 8, 8, lanes), axis=0)

    def group_sum(x):
        return jnp.sum(x.reshape(KCH // 8, 8, lanes), axis=0)

    def finish(lrun, ot):
        l = jnp.sum(lrun, axis=0, keepdims=True)
        return jnp.transpose(ot * _safe_inv(l))

    for g in range(N_KV):
        q = _q_heads(uq_ref, g, gq, cos2, sin2, HEAD_DIM ** -0.5)
        ksl = slice(g * HEAD_DIM, (g + 1) * HEAD_DIM)
        kc = kvc_ref[0, :, ksl].astype(BF16)
        vc = kvc_ref[0, :, KVW + g * HEAD_DIM:KVW + (g + 1) * HEAD_DIM].astype(BF16)

        p, _, l = _softmax_parts(_dot_nt(q, kc), cmask)
        o_cmp = _dot((p * _safe_inv(l)).astype(BF16), vc)

        st = jnp.where(cmask_t, _dot_nt(kc, q), NEG)
        pt = jnp.where(cmask_t, jnp.exp(st - jnp.max(st, axis=0, keepdims=True)), 0.0)
        pt = pt * _safe_inv(jnp.sum(pt, axis=0, keepdims=True))
        half = pt[:, 0:2 * t] + pt[:, 2 * t:4 * t]
        psum_t = half + pltpu.roll(half, t, 1)
        score_t = jnp.where(valid_t, psum_t + FORCE * forced_t, NEG)
        sel_t = jnp.where(valid_t, _rank_select_t(score_t, n_col, min(N_SEL, nb)), 0.0)
        bias_t = (sel_t - 1.0) * (-NEG)
        bias_scr[g] = jnp.concatenate([bias_t, bias_t], axis=1)

        q_scr[g] = q
        ocmp_scr[g] = o_cmp
        mrun_scr[g] = jnp.full((8, lanes), NEG, F32)
        lrun_scr[g] = jnp.zeros((8, lanes), F32)
        ot_scr[g] = jnp.zeros((HEAD_DIM, lanes), F32)

    def sel_pass1(sc):
        for g in range(N_KV):
            ksl = slice(g * HEAD_DIM, (g + 1) * HEAD_DIM)
            st = _dot_nt(ks_ref[0, sc * spk:(sc + 1) * spk, ksl], q_scr[g])
            mx = mrun_scr[g]
            for j in range(spk // BLK):
                n = sc * (spk // BLK) + j
                blk = st[j * BLK:(j + 1) * BLK] + bias_scr[g, n:n + 1, :]
                blk = jnp.where(n * BLK + k_in_blk <= qpos_l, blk, NEG)
                s_scr[g, n // bpc, (n % bpc) * BLK:(n % bpc + 1) * BLK, :] = blk
                mx = jnp.maximum(mx, jnp.max(blk.reshape(BLK // 8, 8, lanes), axis=0))
            mrun_scr[g] = mx

    def sel_pass2(sc):
        for g in range(N_KV):
            ksl = slice(g * HEAD_DIM, (g + 1) * HEAD_DIM)
            m = jnp.max(mrun_scr[g], axis=0, keepdims=True)
            lrun = lrun_scr[g]
            ot = ot_scr[g]
            for c in range(sc * (spk // KCH), (sc + 1) * (spk // KCH)):
                p = jnp.exp(s_scr[g, c] - m)
                lrun = lrun + group_sum(p)
                ot = ot + _dot(vst_ref[c, ksl, :], p.astype(BF16))
            lrun_scr[g] = lrun
            ot_scr[g] = ot

    for sweep in (sel_pass1, sel_pass2):
        for sc in range(seq // spk):
            if sc == 0:
                sweep(0)
            else:
                pl.when(sc * spk <= i * t)(functools.partial(sweep, sc))

    cs = jnp.minimum(jnp.maximum(i - WINDOW // BLK, 0) // bpc, nch - nwc)
    for g in range(N_KV):
        ksl = slice(g * HEAD_DIM, (g + 1) * HEAD_DIM)
        q = q_scr[g]
        o_sel = finish(lrun_scr[g], ot_scr[g])
        mrun = jnp.full((8, lanes), NEG, F32)
        for j in range(nwc):
            c = cs + j
            k = kw_ref[0, pl.ds(pl.multiple_of(c * KCH, KCH), KCH), ksl]
            dpos = qpos_l - (c * KCH + k_in_chunk)
            st = jnp.where((dpos >= 0) & (dpos < WINDOW), _dot_nt(k, q), NEG)
            s_scr[g, j] = st
            mrun = jnp.maximum(mrun, group_max(st))
        m = jnp.max(mrun, axis=0, keepdims=True)
        lrun = jnp.zeros((8, lanes), F32)
        ot = jnp.zeros((HEAD_DIM, lanes), F32)
        for j in range(nwc):
            p = jnp.exp(s_scr[g, j] - m)
            lrun = lrun + group_sum(p)
            ot = ot + _dot(vwt_ref[cs + j, ksl, :], p.astype(BF16))
        o_win = finish(lrun, ot)

        _gate_and_store(o_ref, g, (ocmp_scr[g], o_sel, o_win), (z0_ref, z1_ref, z2_ref), gate, t)


def _prompt_attn(u, gate, kvc, ks, vst, kw, vwt, cos2, sin2, gq, batch, seq):
    n = u.shape[0]
    t = BLK
    spk = _pick(seq, (SPAN_KEYS, KCH))
    assert GRP * t == 2 * KCH and seq % KCH == 0 and spk % KCH == 0
    nqb = seq // t
    nb = seq // BLK
    nch = seq // KCH
    lanes = GRP * t
    row = lambda b, i: (b * nqb + i, 0)
    keys = pl.BlockSpec((1, seq, KVW), lambda b, i: (b, 0, 0))
    vals_t = pl.BlockSpec((nch, KVW, KCH), lambda b, i: (b, 0, 0))
    return pl.pallas_call(
        functools.partial(_prompt_attn_kernel, seq=seq, spk=spk),
        grid=(batch, nqb),
        in_specs=[pl.BlockSpec((t, HD), row),
                  pl.BlockSpec((t, HD), lambda b, i: (b * nqb + i, 1)),
                  pl.BlockSpec((t, HD), lambda b, i: (b * nqb + i, 2)),
                  pl.BlockSpec((t, HD), lambda b, i: (b * nqb + i, 3)),
                  pl.BlockSpec((t, HEAD_DIM), row),
                  pl.BlockSpec((1, nb, 2 * KVW), lambda b, i: (b, 0, 0)),
                  keys, vals_t, keys, vals_t,
                  pl.BlockSpec((t, HEAD_DIM), lambda b, i: (i, 0)),
                  pl.BlockSpec((t, HEAD_DIM), lambda b, i: (i, 0)),
                  pl.BlockSpec((1, HEAD_DIM), lambda b, i: (0, 0))],
        out_specs=pl.BlockSpec((t, HD), row),
        out_shape=jax.ShapeDtypeStruct((n, HD), BF16),
        scratch_shapes=[pltpu.VMEM((N_KV, lanes, HEAD_DIM), BF16),
                        pltpu.VMEM((N_KV, nch, KCH, lanes), F32),
                        pltpu.VMEM((N_KV, nb, lanes), F32),
                        pltpu.VMEM((N_KV, 8, lanes), F32),
                        pltpu.VMEM((N_KV, 8, lanes), F32),
                        pltpu.VMEM((N_KV, HEAD_DIM, lanes), F32),
                        pltpu.VMEM((N_KV, lanes, HEAD_DIM), F32)],
        compiler_params=_cparams(2),
        name="prompt_attn",
    )(u, u, u, u, gate, kvc.reshape(batch, nb, 2 * KVW), ks.reshape(batch, seq, KVW), vst,
      kw.reshape(batch, seq, KVW), vwt, cos2, sin2, gq)


def _decode_front_kernel(uq_ref, kvc_ref, swin_ref, nwin_ref, cos_ref, sin_ref, gq_ref,
                         qn_ref, sel_ref, ocmp_ref, owin_ref, *, past, t):
    scale = HEAD_DIM ** -0.5
    cos2 = cos_ref[...]
    sin2 = sin_ref[...]
    gq = gq_ref[...]
    nbp = kvc_ref.shape[1]
    wbuf = swin_ref.shape[1]
    qpos = past + lax.broadcasted_iota(jnp.int32, (t, 1), 0)
    qpos_r = _tile_rows(qpos, GRP)
    cur = qpos // BLK
    n_i = lax.broadcasted_iota(jnp.int32, (1, nbp), 1)
    cmask = (n_i * BLK + (BLK - 1)) <= qpos_r
    kwpos_old = (past - wbuf) + lax.broadcasted_iota(jnp.int32, (1, wbuf), 1)
    kwpos_new = past + lax.broadcasted_iota(jnp.int32, (1, NEW_PAD), 1)
    d_old = qpos_r - kwpos_old
    d_new = qpos_r - kwpos_new
    m_old = (d_old >= 0) & (d_old < WINDOW) & (kwpos_old >= 0)
    m_new = (d_new >= 0) & (d_new < WINDOW) & (kwpos_new < past + t)

    for g in range(N_KV):
        q = _q_heads(uq_ref, g, gq, cos2, sin2)
        rows = slice(g * GRP * t, (g + 1) * GRP * t)
        ksl = slice(g * HEAD_DIM, (g + 1) * HEAD_DIM)
        vsl = slice(KVW + g * HEAD_DIM, KVW + (g + 1) * HEAD_DIM)
        qn_ref[0, rows, :] = q

        s = _dot_nt(q, kvc_ref[0, :, ksl].astype(BF16)) * scale
        p, _, l = _softmax_parts(s, cmask)
        p = p * _safe_inv(l)
        ocmp_ref[0, rows, :] = _dot(p.astype(BF16), kvc_ref[0, :, vsl].astype(BF16))

        psum = p[0:t]
        for r in range(1, GRP):
            psum = psum + p[r * t:(r + 1) * t]
        sel_ref[0, g * t:(g + 1) * t, :] = _select_blocks(psum, cur, n_i)

        s_old = _dot_nt(q, swin_ref[0, :, ksl].astype(BF16)) * scale
        s_new = _dot_nt(q, nwin_ref[0, :, ksl].astype(BF16)) * scale
        mx = jnp.maximum(jnp.max(jnp.where(m_old, s_old, NEG), axis=-1, keepdims=True),
                         jnp.max(jnp.where(m_new, s_new, NEG), axis=-1, keepdims=True))
        p_old = jnp.where(m_old, jnp.exp(jnp.where(m_old, s_old, NEG) - mx), 0.0)
        p_new = jnp.where(m_new, jnp.exp(jnp.where(m_new, s_new, NEG) - mx), 0.0)
        l = jnp.sum(p_old, axis=-1, keepdims=True) + jnp.sum(p_new, axis=-1, keepdims=True)
        o = _dot(p_old.astype(BF16), swin_ref[0, :, vsl].astype(BF16)) \
            + _dot(p_new.astype(BF16), nwin_ref[0, :, vsl].astype(BF16))
        owin_ref[0, rows, :] = o * _safe_inv(l)


def _decode_front(u, kvc_all, state_win2, new_win, cos2, sin2, gq, bd, t, past):
    nbp = kvc_all.shape[1]
    wbuf = state_win2.shape[1]
    rows = N_HEADS * t
    return pl.pallas_call(
        functools.partial(_decode_front_kernel, past=past, t=t),
        grid=(bd,),
        in_specs=[pl.BlockSpec((t, HD), lambda b: (b, 0)),
                  pl.BlockSpec((1, nbp, 2 * KVW), lambda b: (b, 0, 0)),
                  pl.BlockSpec((1, wbuf, 2 * KVW), lambda b: (b, 0, 0)),
                  pl.BlockSpec((1, NEW_PAD, 2 * KVW), lambda b: (b, 0, 0)),
                  pl.BlockSpec((t, HEAD_DIM), lambda b: (0, 0)),
                  pl.BlockSpec((t, HEAD_DIM), lambda b: (0, 0)),
                  pl.BlockSpec((1, HEAD_DIM), lambda b: (0, 0))],
        out_specs=[pl.BlockSpec((1, rows, HEAD_DIM), lambda b: (b, 0, 0)),
                   pl.BlockSpec((1, N_KV * t, nbp), lambda b: (b, 0, 0)),
                   pl.BlockSpec((1, rows, HEAD_DIM), lambda b: (b, 0, 0)),
                   pl.BlockSpec((1, rows, HEAD_DIM), lambda b: (b, 0, 0))],
        out_shape=[jax.ShapeDtypeStruct((bd, rows, HEAD_DIM), BF16),
                   jax.ShapeDtypeStruct((bd, N_KV * t, nbp), F32),
                   jax.ShapeDtypeStruct((bd, rows, HEAD_DIM), F32),
                   jax.ShapeDtypeStruct((bd, rows, HEAD_DIM), F32)],
        compiler_params=_cparams(1),
        name="decode_front",
    )(u, kvc_all, state_win2, new_win, cos2, sin2, gq)


def _decode_sel_kernel(pt_ref, *refs, t, n_steps):
    del pt_ref
    np_ = PAGES_PER_STEP
    (qn_ref, selc_ref, seln_ref) = refs[:3]
    page_refs = refs[3:3 + np_]
    (new_ref, ocmp_ref, owin_ref, z0_ref, z1_ref, z2_ref, gate_ref, e_ref,
     o_ref, m_ref, l_ref, acc_ref) = refs[3 + np_:]
    c = pl.program_id(1)
    scale = HEAD_DIM ** -0.5
    gt = GRP * t

    @pl.when(c == 0)
    def _():
        m_ref[...] = jnp.full(m_ref.shape, NEG, F32)
        l_ref[...] = jnp.zeros(l_ref.shape, F32)
        acc_ref[...] = jnp.zeros(acc_ref.shape, F32)

    def update(keys_of, vals_of, sel_rows, extra_mask=None):
        s = jnp.concatenate([_dot_nt(qn_ref[0, g * gt:(g + 1) * gt, :], keys_of(g))
                             for g in range(N_KV)], axis=0) * scale
        mask = jnp.concatenate([sel_rows[g * t:(g + 1) * t] for g in range(N_KV) for _ in range(GRP)],
                               axis=0) > 0.5
        if extra_mask is not None:
            mask = mask & extra_mask
        sm = jnp.where(mask, s, NEG)
        m_old = m_ref[...]
        m_new = jnp.maximum(m_old, jnp.max(sm, axis=-1, keepdims=True))
        alpha = jnp.exp(m_old - m_new)
        p = jnp.where(mask, jnp.exp(sm - m_new), 0.0)
        l_ref[...] = alpha * l_ref[...] + jnp.sum(p, axis=-1, keepdims=True)
        pb = p.astype(BF16)
        pv = jnp.concatenate([_dot(pb[g * gt:(g + 1) * gt], vals_of(g)) for g in range(N_KV)], axis=0)
        acc_ref[...] = alpha * acc_ref[...] + pv
        m_ref[...] = m_new

    def page_rows(j):
        return jnp.concatenate([pr[0, :, j, :] for pr in page_refs], axis=0).astype(BF16)

    sel_keys = _dot(selc_ref[0, 0].astype(BF16), e_ref[...])
    update(page_rows, lambda g: page_rows(N_KV + g), sel_keys)

    @pl.when(c == n_steps - 1)
    def _():
        gate = jax.nn.sigmoid(gate_ref[...])
        tq = _tile_rows(lax.broadcasted_iota(jnp.int32, (t, 1), 0), N_HEADS)
        causal = lax.broadcasted_iota(jnp.int32, (1, NEW_PAD), 1) <= tq
        sel_new = jnp.broadcast_to(seln_ref[0, 0, :, 0:1], (N_KV * t, NEW_PAD))
        update(lambda g: new_ref[0, :, g * HEAD_DIM:(g + 1) * HEAD_DIM].astype(BF16),
               lambda g: new_ref[0, :, KVW + g * HEAD_DIM:KVW + (g + 1) * HEAD_DIM].astype(BF16),
               sel_new, causal)
        o_sel = acc_ref[...] * _safe_inv(l_ref[...])
        for g in range(N_KV):
            rows = slice(g * gt, (g + 1) * gt)
            _gate_and_store(o_ref, g, (ocmp_ref[0, rows, :], o_sel[rows], owin_ref[0, rows, :]),
                            (z0_ref, z1_ref, z2_ref), gate, t)


def _decode_sel(page_table, qn, selc, cache4, new_sel, ocmp, owin, u, gate, e8, bd, t):
    n_pages = page_table.shape[1]
    page = cache4.shape[1]
    n_steps = n_pages // PAGES_PER_STEP
    bps = PAGES_PER_STEP * (page // BLK)
    rows = N_HEADS * t

    def page_spec(k):
        return pl.BlockSpec((1, page, 2 * N_KV, HEAD_DIM),
                            lambda b, c, pt: (pt[b, c * PAGES_PER_STEP + k], 0, 1, 0))

    grid_spec = pltpu.PrefetchScalarGridSpec(
        num_scalar_prefetch=1,
        grid=(bd, n_steps),
        in_specs=[pl.BlockSpec((1, rows, HEAD_DIM), lambda b, c, pt: (b, 0, 0)),
                  pl.BlockSpec((1, 1, N_KV * t, bps), lambda b, c, pt: (b, c, 0, 0)),
                  pl.BlockSpec((1, 1, N_KV * t, bps), lambda b, c, pt: (b, n_steps, 0, 0))]
        + [page_spec(k) for k in range(PAGES_PER_STEP)]
        + [pl.BlockSpec((1, NEW_PAD, 2 * KVW), lambda b, c, pt: (b, 0, 0)),
           pl.BlockSpec((1, rows, HEAD_DIM), lambda b, c, pt: (b, 0, 0)),
           pl.BlockSpec((1, rows, HEAD_DIM), lambda b, c, pt: (b, 0, 0)),
           pl.BlockSpec((t, HD), lambda b, c, pt: (b, 1)),
           pl.BlockSpec((t, HD), lambda b, c, pt: (b, 2)),
           pl.BlockSpec((t, HD), lambda b, c, pt: (b, 3)),
           pl.BlockSpec((t, HEAD_DIM), lambda b, c, pt: (b, 0)),
           pl.BlockSpec((bps, bps * BLK), lambda b, c, pt: (0, 0))],
        out_specs=pl.BlockSpec((t, HD), lambda b, c, pt: (b, 0)),
        scratch_shapes=[pltpu.VMEM((rows, 1), F32), pltpu.VMEM((rows, 1), F32),
                        pltpu.VMEM((rows, HEAD_DIM), F32)],
    )
    return pl.pallas_call(
        functools.partial(_decode_sel_kernel, t=t, n_steps=n_steps),
        grid_spec=grid_spec,
        out_shape=jax.ShapeDtypeStruct((bd * t, HD), BF16),
        compiler_params=_cparams(2),
        name="decode_sel",
    )(page_table, qn, selc, selc, *([cache4] * PAGES_PER_STEP), new_sel, ocmp, owin,
      u, u, u, gate, e8)


def _rope_tables(pos):
    half = HEAD_DIM // 2
    inv = ROPE_THETA ** (-jnp.arange(half, dtype=F32) / half)
    ang = pos.astype(F32)[:, None] * inv[None, :]
    c = jnp.cos(ang)
    s = jnp.sin(ang)
    return jnp.concatenate([c, c], axis=-1), jnp.concatenate([-s, s], axis=-1)


def _block_expand(nblocks):
    key_blk = jnp.arange(nblocks * BLK, dtype=jnp.int32) // BLK
    return (key_blk[None, :] == jnp.arange(nblocks, dtype=jnp.int32)[:, None]).astype(BF16)


def _conv_layers(x, prev, seq, g_a, w_in_a, conv_w, w_out_a):
    states = []
    ch = conv_w.shape[-1]
    for layer in range(N_A):
        xn = _rms_cast(x, g_a[layer])
        u = _matmul(xn, w_in_a, layer, 4 * ch)
        y, st = _conv_gate(u, prev[layer], conv_w[layer], seq)
        x = _matmul(y, w_out_a, layer, x.shape[1], res=x)
        states.append(st)
    return x, jnp.stack(states)


def _shared_kv(x, pos, seq, g_kv, w_kv, g_k, for_attn):
    cos2, sin2 = _rope_tables(pos)
    kv = _matmul(_rms_cast(x, g_kv), w_kv, 0, 6 * KVW)
    return _kv_post(kv, cos2, sin2, g_k, seq, for_attn), cos2, sin2


def _query_side(x, lb, g_b, w_qz, w_gate):
    xn = _rms_cast(x, g_b[lb])
    u = _matmul(xn, w_qz, lb, 4 * HD)
    gate = _matmul(xn, w_gate[lb], 0, HEAD_DIM)
    return u, gate


def kernel(x_prompt, x_sample, state_conv, cache_kv, state_win, page_table, g_a, w_in_a, conv_w,
           w_out_a, g_kv, w_kv, g_k, w_cmp, g_b, w_qz, g_q, w_out_b):
    batch, seq, d = x_prompt.shape
    bd, t, _ = x_sample.shape
    ch = conv_w.shape[-1]
    n_pool, page = cache_kv.shape[:2]
    n_pages = page_table.shape[1]
    past = n_pages * page
    wbuf = state_win.shape[1]
    assert seq % BLK == 0 and page % BLK == 0 and t <= BLK and n_pages % PAGES_PER_STEP == 0

    w_exp3 = jnp.broadcast_to(w_cmp.reshape(BLK, 2 * N_KV, 1), (BLK, 2 * N_KV, HEAD_DIM))
    w_exp = w_exp3.reshape(BLK, 2 * KVW)
    n_gate = N_BRANCH * N_HEADS
    w_gate = jnp.pad(w_qz[:, :, 4 * HD:4 * HD + n_gate], ((0, 0), (0, 0), (0, HEAD_DIM - n_gate)))

    xp = x_prompt.reshape(batch * seq, d)
    prev0 = jnp.zeros((batch, CONV_W - 1, ch), F32)
    xn = _rms_cast(xp, g_a[0])
    states = []
    for layer in range(N_A):
        y, st = _conv_matmul(xn, w_in_a, layer, prev0, conv_w[layer], seq)
        states.append(st)
        gains = [g_a[layer + 1]] if layer + 1 < N_A else [g_kv, g_b[0]]
        xp, *xns = _matmul_res_norm(y, w_out_a, layer, xp, gains)
        xn = xns[0]
    conv_p = jnp.stack(states)
    xn_kv, xn_q = xns
    pos_p = jnp.arange(seq, dtype=jnp.int32)
    cos_p, sin_p = _rope_tables(pos_p)
    rows_p, win_p, ks_p, vst_p, kw_p, vwt_p = _kv_post(
        _matmul(xn_kv, w_kv, 0, 6 * KVW), cos_p, sin_p, g_k, seq, True)
    kvc_p = _compress(rows_p, w_exp)
    for lb in range(N_B):
        u = _matmul(xn_q, w_qz, lb, 4 * HD)
        gate = _matmul(xn_q, w_gate[lb], 0, HEAD_DIM)
        o = _prompt_attn(u, gate, kvc_p, ks_p, vst_p, kw_p, vwt_p, cos_p, sin_p, g_q[lb:lb + 1],
                         batch, seq)
        if lb + 1 < N_B:
            xp, xn_q = _matmul_res_norm(o, w_out_b, lb, xp, [g_b[lb + 1]])
        else:
            xp = _matmul(o, w_out_b, lb, d, res=xp)
    wn = min(WINDOW, seq)
    y_prompt = xp.reshape(batch, seq, d)
    kv_rows_prompt = rows_p.reshape(batch, seq, 4, N_KV, HEAD_DIM)
    win_prompt = win_p.reshape(batch, seq, 2, N_KV, HEAD_DIM)[:, seq - wn:]

    xs = x_sample.reshape(bd * t, d)
    xs, conv_s = _conv_layers(xs, state_conv, t, g_a, w_in_a, conv_w, w_out_a)
    pos_s = past + jnp.arange(t, dtype=jnp.int32)
    (rows_s, win_s), cos_s, sin_s = _shared_kv(xs, pos_s, t, g_kv, w_kv, g_k, False)
    cache4 = cache_kv.reshape(n_pool, page, 4 * N_KV, HEAD_DIM)
    kvc_past = _compress_pages(cache4, page_table, w_exp3).reshape(bd, past // BLK, 2 * KVW)
    rows_s3 = rows_s.reshape(bd, t, 4 * KVW)
    rows_pad = jnp.pad(rows_s3, ((0, 0), (0, NEW_PAD - t), (0, 0)))
    kvc_new = _compress(rows_pad[:, :BLK, :2 * KVW].reshape(bd * BLK, 2 * KVW), w_exp)
    nb_past = past // BLK
    bps = PAGES_PER_STEP * (page // BLK)
    nbp = nb_past + bps
    kvc_all = jnp.concatenate(
        [kvc_past, kvc_new[:, None, :], jnp.zeros((bd, bps - 1, 2 * KVW), F32)], axis=1)
    new_sel = rows_pad[:, :, 2 * KVW:]
    new_win = jnp.pad(win_s.reshape(bd, t, 2 * KVW), ((0, 0), (0, NEW_PAD - t), (0, 0)))
    state_win2 = state_win.reshape(bd, wbuf, 2 * KVW)
    e8 = _block_expand(bps)
    for lb in range(N_B):
        u, gate = _query_side(xs, lb, g_b, w_qz, w_gate)
        qn, sel, ocmp, owin = _decode_front(u, kvc_all, state_win2, new_win, cos_s, sin_s,
                                            g_q[lb:lb + 1], bd, t, past)
        selc = sel.reshape(bd, N_KV * t, nbp // bps, bps).transpose(0, 2, 1, 3)
        o = _decode_sel(page_table, qn, selc, cache4, new_sel, ocmp, owin, u, gate, e8, bd, t)
        xs = _matmul(o, w_out_b, lb, d, res=xs)
    y_sample = xs.reshape(bd, t, d)
    kv_rows_sample = rows_s.reshape(bd, t, 4, N_KV, HEAD_DIM)
    kvw = jnp.concatenate([state_win, win_s.reshape(bd, t, 2, N_KV, HEAD_DIM)], axis=1)
    win_sample = kvw[:, kvw.shape[1] - min(WINDOW, kvw.shape[1]):]

    return (y_prompt, y_sample, conv_p, conv_s, kv_rows_prompt, kv_rows_sample, win_prompt, win_sample)
```

```python
import functools

import jax
import jax.numpy as jnp
from jax import lax
from jax.experimental import pallas as pl
from jax.experimental.pallas import tpu as pltpu

N_A = 2
N_B = 2
CONV_W = 3
N_HEADS = 16
N_KV = 4
GRP = N_HEADS // N_KV
HEAD_DIM = 128
N_BRANCH = 3
BLK = 64
N_SEL = 16
WINDOW = 512
ROPE_THETA = 10000.0
EPS = 1e-6
NEG = -1e30
FORCE = 1e3

HD = N_HEADS * HEAD_DIM
KVW = N_KV * HEAD_DIM
NEW_PAD = 128
KCH = 128
SPAN_KEYS = 512
PAGES_PER_STEP = 8
VMEM_LIMIT = 56 * 1024 * 1024

F32 = jnp.float32
BF16 = jnp.bfloat16


def _cparams(n_axes):
    return pltpu.CompilerParams(
        dimension_semantics=("arbitrary",) * n_axes, vmem_limit_bytes=VMEM_LIMIT)


def _dot_nt(a, b):
    return lax.dot_general(a, b, (((1,), (1,)), ((), ())), preferred_element_type=F32)


def _dot(a, b):
    return jnp.dot(a, b, preferred_element_type=F32)


def _pick(n, prefs):
    for p in prefs:
        if n % p == 0:
            return p
    return n


def _rms_cast_kernel(x_ref, g_ref, o_ref):
    x = x_ref[...]
    y = x * lax.rsqrt(jnp.mean(x * x, axis=-1, keepdims=True) + EPS)
    o_ref[...] = (y * g_ref[...]).astype(BF16)


def _rms_cast(x, g):
    n, d = x.shape
    tm = _pick(n, (512, 256, 128, 64, 8))
    return pl.pallas_call(
        _rms_cast_kernel,
        grid=(n // tm,),
        in_specs=[pl.BlockSpec((tm, d), lambda i: (i, 0)),
                  pl.BlockSpec((1, d), lambda i: (0, 0))],
        out_specs=pl.BlockSpec((tm, d), lambda i: (i, 0)),
        out_shape=jax.ShapeDtypeStruct((n, d), BF16),
        compiler_params=_cparams(1),
        name="rms_cast",
    )(x, g.reshape(1, d))


def _mm_kernel(*refs, has_res):
    if has_res:
        x_ref, w_ref, r_ref, o_ref, wb_ref = refs
    else:
        x_ref, w_ref, o_ref, wb_ref = refs

    @pl.when(pl.program_id(1) == 0)
    def _():
        wb_ref[...] = w_ref[...].astype(BF16)

    acc = _dot(x_ref[...], wb_ref[...])
    if has_res:
        acc = r_ref[...] + acc
    o_ref[...] = acc


def _matmul(x, w, layer, m_out, res=None):
    n, k = x.shape
    tm = _pick(n, (1024, 512, 256, 128, 64, 8))
    tn = _pick(m_out, (1024, 512, 256, 128))
    if w.ndim == 3:
        w_spec = pl.BlockSpec((None, k, tn), lambda j, i: (layer, 0, j))
    else:
        w_spec = pl.BlockSpec((k, tn), lambda j, i: (0, j))
    in_specs = [pl.BlockSpec((tm, k), lambda j, i: (i, 0)), w_spec]
    args = [x, w]
    if res is not None:
        in_specs.append(pl.BlockSpec((tm, tn), lambda j, i: (i, j)))
        args.append(res)
    return pl.pallas_call(
        functools.partial(_mm_kernel, has_res=res is not None),
        grid=(m_out // tn, n // tm),
        in_specs=in_specs,
        out_specs=pl.BlockSpec((tm, tn), lambda j, i: (i, j)),
        out_shape=jax.ShapeDtypeStruct((n, m_out), F32),
        scratch_shapes=[pltpu.VMEM((k, tn), BF16)],
        compiler_params=_cparams(2),
        name="matmul_res" if res is not None else "matmul",
    )(*args)


def _mm_t_kernel(x_ref, w_ref, o_ref, wb_ref):
    @pl.when(pl.program_id(1) == 0)
    def _():
        wb_ref[...] = jnp.transpose(w_ref[...]).astype(BF16)

    o_ref[...] = _dot(x_ref[...], wb_ref[...])


def _matmul_t(x, w_t, layer, m_out):
    n, k = x.shape
    tm = _pick(n, (1024, 512, 256, 128, 64, 8))
    tn = _pick(m_out, (1024, 512, 256, 128))
    return pl.pallas_call(
        _mm_t_kernel,
        grid=(m_out // tn, n // tm),
        in_specs=[pl.BlockSpec((tm, k), lambda j, i: (i, 0)),
                  pl.BlockSpec((None, tn, k), lambda j, i: (layer, j, 0))],
        out_specs=pl.BlockSpec((tm, tn), lambda j, i: (i, j)),
        out_shape=jax.ShapeDtypeStruct((n, m_out), F32),
        scratch_shapes=[pltpu.VMEM((k, tn), BF16)],
        compiler_params=_cparams(2),
        name="matmul_t",
    )(x, w_t)


def _mm_res_norm_kernel(x_ref, w_ref, r_ref, *rest, n_norm):
    g_refs = rest[:n_norm]
    o_ref = rest[n_norm]
    xn_refs = rest[n_norm + 1:2 * n_norm + 1]
    wb_ref = rest[2 * n_norm + 1]

    @pl.when(pl.program_id(0) == 0)
    def _():
        wb_ref[...] = w_ref[...].astype(BF16)

    out = r_ref[...] + _dot(x_ref[...], wb_ref[...])
    o_ref[...] = out
    y = out * lax.rsqrt(jnp.mean(out * out, axis=-1, keepdims=True) + EPS)
    for g_ref, xn_ref in zip(g_refs, xn_refs):
        xn_ref[...] = (y * g_ref[...]).astype(BF16)


def _matmul_res_norm(x, w, layer, res, gains):
    n, k = x.shape
    d = res.shape[1]
    tm = _pick(n, (512, 256, 128, 64, 8))
    row = lambda i: (i, 0)
    n_norm = len(gains)
    return pl.pallas_call(
        functools.partial(_mm_res_norm_kernel, n_norm=n_norm),
        grid=(n // tm,),
        in_specs=[pl.BlockSpec((tm, k), row),
                  pl.BlockSpec((None, k, d), lambda i: (layer, 0, 0), pipeline_mode=pl.Buffered(1)),
                  pl.BlockSpec((tm, d), row)]
        + [pl.BlockSpec((1, d), lambda i: (0, 0))] * n_norm,
        out_specs=[pl.BlockSpec((tm, d), row)] * (1 + n_norm),
        out_shape=[jax.ShapeDtypeStruct((n, d), F32)] + [jax.ShapeDtypeStruct((n, d), BF16)] * n_norm,
        scratch_shapes=[pltpu.VMEM((k, d), BF16)],
        compiler_params=_cparams(1),
        name="matmul_res_norm",
    )(x, w, res, *[g.reshape(1, d) for g in gains])


def _conv_mm_kernel(x_ref, wb_ref, wc_ref, wh_ref, wz_ref, prev_ref, cw_ref, y_ref, st_ref,
                    wbf_ref, carry_ref, *, tps):
    i = pl.program_id(1)

    @pl.when(i == 0)
    def _():
        for k, wr in enumerate((wb_ref, wc_ref, wh_ref, wz_ref)):
            wbf_ref[k] = wr[...].astype(BF16)

    x = x_ref[...]
    c = _dot(x, wbf_ref[1]) * _dot(x, wbf_ref[2])
    y_ref[...] = _gated_conv(c, _dot(x, wbf_ref[0]), _dot(x, wbf_ref[3]), i, tps,
                             prev_ref, cw_ref, st_ref, carry_ref).astype(BF16)


def _conv_matmul(xn, w_in, layer, prev, w_conv, seq):
    n, k = xn.shape
    ch = w_in.shape[2] // 4
    tm = _pick(seq, (1024, 512, 256, 128, 64, 8))
    tc = _pick(ch, (256, 128))
    tps = seq // tm
    nc = ch // tc

    def w_spec(part):
        return pl.BlockSpec((None, k, tc), lambda j, i: (layer, 0, part * nc + j))

    return pl.pallas_call(
        functools.partial(_conv_mm_kernel, tps=tps),
        grid=(nc, n // tm),
        in_specs=[pl.BlockSpec((tm, k), lambda j, i: (i, 0))] + [w_spec(p) for p in range(4)]
        + [pl.BlockSpec((1, 2, tc), lambda j, i: (i // tps, 0, j)),
           pl.BlockSpec((CONV_W, tc), lambda j, i: (0, j))],
        out_specs=[pl.BlockSpec((tm, tc), lambda j, i: (i, j)),
                   pl.BlockSpec((1, 2, tc), lambda j, i: (i // tps, 0, j))],
        out_shape=[jax.ShapeDtypeStruct((n, ch), BF16),
                   jax.ShapeDtypeStruct((n // seq, CONV_W - 1, ch), F32)],
        scratch_shapes=[pltpu.VMEM((4, k, tc), BF16), pltpu.VMEM((8, tc), F32)],
        compiler_params=_cparams(2),
        name="conv_matmul",
    )(xn, w_in, w_in, w_in, w_in, prev, w_conv)


def _gated_conv(c, b, z, i, tps, prev_ref, w_ref, st_ref, carry_ref):
    tm = c.shape[0]

    @pl.when(i % tps == 0)
    def _():
        carry_ref[0:2, :] = prev_ref[0]

    p0 = carry_ref[0:1, :]
    p1 = carry_ref[1:2, :]
    row = lax.broadcasted_iota(jnp.int32, c.shape, 0)
    c1 = jnp.where(row == 0, p1, pltpu.roll(c, 1, 0))
    c2 = jnp.where(row == 0, p0, jnp.where(row == 1, p1, pltpu.roll(c, 2, 0)))
    conv = w_ref[0:1, :] * c2 + w_ref[1:2, :] * c1 + w_ref[2:3, :] * c
    last = c[tm - 2:tm, :]
    carry_ref[0:2, :] = last
    st_ref[0] = last
    return jax.nn.silu(z) * (b * conv)


def _conv_gate_kernel(b_ref, cg_ref, h_ref, z_ref, prev_ref, w_ref, y_ref, st_ref, carry_ref, *, tps):
    c = cg_ref[...] * h_ref[...]
    y_ref[...] = _gated_conv(c, b_ref[...], z_ref[...], pl.program_id(1), tps,
                             prev_ref, w_ref, st_ref, carry_ref).astype(BF16)


def _conv_gate(u, prev, w_conv, seq):
    n = u.shape[0]
    ch = u.shape[1] // 4
    tm = _pick(seq, (256, 128, 64, 8))
    tc = _pick(ch, (512, 256, 128))
    tps = seq // tm
    nc = ch // tc
    nb = n // seq
    return pl.pallas_call(
        functools.partial(_conv_gate_kernel, tps=tps),
        grid=(nc, n // tm),
        in_specs=[pl.BlockSpec((tm, tc), lambda j, i: (i, j)),
                  pl.BlockSpec((tm, tc), lambda j, i: (i, nc + j)),
                  pl.BlockSpec((tm, tc), lambda j, i: (i, 2 * nc + j)),
                  pl.BlockSpec((tm, tc), lambda j, i: (i, 3 * nc + j)),
                  pl.BlockSpec((1, 2, tc), lambda j, i: (i // tps, 0, j)),
                  pl.BlockSpec((CONV_W, tc), lambda j, i: (0, j))],
        out_specs=[pl.BlockSpec((tm, tc), lambda j, i: (i, j)),
                   pl.BlockSpec((1, 2, tc), lambda j, i: (i // tps, 0, j))],
        out_shape=[jax.ShapeDtypeStruct((n, ch), BF16),
                   jax.ShapeDtypeStruct((nb, CONV_W - 1, ch), F32)],
        scratch_shapes=[pltpu.VMEM((8, tc), F32)],
        compiler_params=_cparams(2),
        name="conv_gate",
    )(u, u, u, u, prev, w_conv)


def _norm_rope(x, g, cos2, sin2):
    y = x * lax.rsqrt(jnp.mean(x * x, axis=-1, keepdims=True) + EPS) * g
    return y * cos2 + pltpu.roll(y, HEAD_DIM // 2, 1) * sin2


def _kv_post_kernel(kv_ref, cos_ref, sin_ref, gk_ref, rows_ref, win_ref, *attn_refs):
    cos2 = cos_ref[...]
    sin2 = sin_ref[...]
    tm = kv_ref.shape[0]
    for br in range(N_BRANCH):
        base = br * 2 * KVW
        for g in range(N_KV):
            c0 = base + g * HEAD_DIM
            k = _norm_rope(kv_ref[:, c0:c0 + HEAD_DIM], gk_ref[br:br + 1, :], cos2, sin2)
            if br < 2:
                rows_ref[:, c0:c0 + HEAD_DIM] = k
            else:
                win_ref[:, g * HEAD_DIM:(g + 1) * HEAD_DIM] = k
            if attn_refs and br >= 1:
                attn_refs[2 * (br - 1)][:, g * HEAD_DIM:(g + 1) * HEAD_DIM] = k.astype(BF16)
        v = kv_ref[:, base + KVW:base + 2 * KVW]
        if br < 2:
            rows_ref[:, base + KVW:base + 2 * KVW] = v
        else:
            win_ref[:, KVW:2 * KVW] = v
        if attn_refs and br >= 1:
            vt_ref = attn_refs[2 * (br - 1) + 1]
            for a in range(tm // KCH):
                vt_ref[a] = jnp.transpose(v[a * KCH:(a + 1) * KCH, :]).astype(BF16)


def _kv_post(kv, cos2, sin2, g_k, seq, for_attn):
    n = kv.shape[0]
    tm = _pick(seq, (256, 128, 64, 8))
    tps = seq // tm
    out_specs = [pl.BlockSpec((tm, 4 * KVW), lambda i: (i, 0)),
                 pl.BlockSpec((tm, 2 * KVW), lambda i: (i, 0))]
    out_shape = [jax.ShapeDtypeStruct((n, 4 * KVW), F32),
                 jax.ShapeDtypeStruct((n, 2 * KVW), F32)]
    if for_attn:
        assert tm % KCH == 0
        for _ in range(2):
            out_specs += [pl.BlockSpec((tm, KVW), lambda i: (i, 0)),
                          pl.BlockSpec((tm // KCH, KVW, KCH), lambda i: (i, 0, 0))]
            out_shape += [jax.ShapeDtypeStruct((n, KVW), BF16),
                          jax.ShapeDtypeStruct((n // KCH, KVW, KCH), BF16)]
    return pl.pallas_call(
        _kv_post_kernel,
        grid=(n // tm,),
        in_specs=[pl.BlockSpec((tm, 6 * KVW), lambda i: (i, 0)),
                  pl.BlockSpec((tm, HEAD_DIM), lambda i: (i % tps, 0)),
                  pl.BlockSpec((tm, HEAD_DIM), lambda i: (i % tps, 0)),
                  pl.BlockSpec((N_BRANCH, HEAD_DIM), lambda i: (0, 0))],
        out_specs=out_specs,
        out_shape=out_shape,
        compiler_params=_cparams(1),
        name="kv_post",
    )(kv, cos2, sin2, g_k)


def _compress_kernel(rows_ref, w_ref, o_ref):
    x = rows_ref[...]
    nb = x.shape[0] // BLK
    x = x.reshape(nb, BLK, x.shape[1]) * w_ref[...][None]
    o_ref[...] = jnp.sum(x, axis=1)


def _compress(rows, w_exp):
    n = rows.shape[0]
    nblk = n // BLK
    per = _pick(nblk, (8,))
    return pl.pallas_call(
        _compress_kernel,
        grid=(nblk // per,),
        in_specs=[pl.BlockSpec((per * BLK, 2 * KVW), lambda i: (i, 0)),
                  pl.BlockSpec((BLK, 2 * KVW), lambda i: (0, 0))],
        out_specs=pl.BlockSpec((per, 2 * KVW), lambda i: (i, 0)),
        out_shape=jax.ShapeDtypeStruct((nblk, 2 * KVW), F32),
        compiler_params=_cparams(1),
        name="compress",
    )(rows, w_exp)


def _compress_pages_kernel(pt_ref, *refs):
    page_refs = refs[:PAGES_PER_STEP]
    w_ref = refs[PAGES_PER_STEP]
    o_ref = refs[PAGES_PER_STEP + 1]
    w = w_ref[...]
    for k, pr in enumerate(page_refs):
        x = pr[0]
        nb = x.shape[0] // BLK
        o_ref[0, k * nb:(k + 1) * nb] = jnp.sum(x.reshape(nb, BLK, 2 * N_KV, HEAD_DIM) * w[None], axis=1)


def _compress_pages(cache4, page_table, w_exp3):
    bd, n_pages = page_table.shape
    page = cache4.shape[1]
    bpp = page // BLK
    steps = n_pages // PAGES_PER_STEP
    rows_out = PAGES_PER_STEP * bpp

    def page_spec(k):
        return pl.BlockSpec((1, page, 2 * N_KV, HEAD_DIM),
                            lambda b, c, pt: (pt[b, c * PAGES_PER_STEP + k], 0, 0, 0))

    grid_spec = pltpu.PrefetchScalarGridSpec(
        num_scalar_prefetch=1,
        grid=(bd, steps),
        in_specs=[page_spec(k) for k in range(PAGES_PER_STEP)]
        + [pl.BlockSpec((BLK, 2 * N_KV, HEAD_DIM), lambda b, c, pt: (0, 0, 0))],
        out_specs=pl.BlockSpec((1, rows_out, 2 * N_KV, HEAD_DIM), lambda b, c, pt: (b, c, 0, 0)),
    )
    return pl.pallas_call(
        _compress_pages_kernel,
        grid_spec=grid_spec,
        out_shape=jax.ShapeDtypeStruct((bd, n_pages * bpp, 2 * N_KV, HEAD_DIM), F32),
        compiler_params=_cparams(2),
        name="compress_pages",
    )(page_table, *([cache4] * PAGES_PER_STEP), w_exp3)


def _softmax_parts(s, mask):
    sm = jnp.where(mask, s, NEG)
    m = jnp.max(sm, axis=-1, keepdims=True)
    p = jnp.where(mask, jnp.exp(sm - m), 0.0)
    return p, m, jnp.sum(p, axis=-1, keepdims=True)


def _safe_inv(l):
    return jnp.where(l > 0.0, 1.0 / l, 0.0)


def _topk_mask(score, n_f, k):
    sel = jnp.zeros(score.shape, F32)
    big = jnp.float32(score.shape[-1] + 1)
    for _ in range(k):
        mx = jnp.max(score, axis=-1, keepdims=True)
        idx = jnp.min(jnp.where(score == mx, n_f, big), axis=-1, keepdims=True)
        hit = n_f == idx
        sel = jnp.where(hit, 1.0, sel)
        score = jnp.where(hit, -jnp.inf, score)
    return sel


def _select_blocks(psum, cur, n_i):
    forced = ((n_i == 0) | (n_i == cur) | (n_i == cur - 1)).astype(F32)
    valid = n_i <= cur
    score = jnp.where(valid, psum + FORCE * forced, NEG)
    sel = _topk_mask(score, n_i.astype(F32), min(N_SEL, psum.shape[-1]))
    return jnp.where(valid, sel, 0.0)


def _q_heads(uq_ref, g, gq, cos2, sin2, scale=None):
    hs = []
    for r in range(GRP):
        c0 = (g * GRP + r) * HEAD_DIM
        h = _norm_rope(uq_ref[:, c0:c0 + HEAD_DIM], gq, cos2, sin2)
        hs.append(h if scale is None else h * scale)
    return jnp.concatenate(hs, axis=0).astype(BF16)


def _tile_rows(x, reps):
    return jnp.concatenate([x] * reps, axis=0)


def _gate_and_store(o_ref, g, outs, z_refs, gate, t):
    for r in range(GRP):
        h = g * GRP + r
        acc = None
        for br in range(N_BRANCH):
            zg = jax.nn.silu(z_refs[br][:, h * HEAD_DIM:(h + 1) * HEAD_DIM]) \
                * gate[:, br * N_HEADS + h:br * N_HEADS + h + 1]
            term = outs[br][r * t:(r + 1) * t, :] * zg
            acc = term if acc is None else acc + term
        o_ref[:, h * HEAD_DIM:(h + 1) * HEAD_DIM] = acc.astype(o_ref.dtype)


def _rank_select_t(score_t, n_col, k):
    rank = jnp.zeros(score_t.shape, F32)
    for m in range(score_t.shape[0]):
        row = score_t[m:m + 1, :]
        tie = (n_col > m).astype(F32)
        rank = rank + jnp.where(row > score_t, 1.0, 0.0) + jnp.where(row == score_t, tie, 0.0)
    return jnp.where(rank < k, 1.0, 0.0)


def _prompt_attn_kernel(uq_ref, z0_ref, z1_ref, z2_ref, gate_ref, kvc_ref, ks_ref, vst_ref,
                        kw_ref, vwt_ref, cos_ref, sin_ref, gq_ref, o_ref,
                        q_scr, s_scr, bias_scr, mrun_scr, lrun_scr, ot_scr, ocmp_scr, *, seq, spk):
    i = pl.program_id(1)
    t = BLK
    lanes = GRP * t
    nb = seq // BLK
    nch = seq // KCH
    nwc = min(WINDOW // KCH + 1, nch)
    bpc = KCH // BLK
    k_in_blk = lax.broadcasted_iota(jnp.int32, (BLK, 1), 0)
    cos2 = cos_ref[...]
    sin2 = sin_ref[...]
    gq = gq_ref[...]
    gate = jax.nn.sigmoid(gate_ref[...])
    qpos_r = _tile_rows(i * t + lax.broadcasted_iota(jnp.int32, (t, 1), 0), GRP)
    qpos_l = i * t + (lax.broadcasted_iota(jnp.int32, (1, lanes), 1) & (t - 1))
    qpos_h = qpos_l[:, 0:2 * t]
    cur_h = qpos_h // BLK
    n_row = lax.broadcasted_iota(jnp.int32, (1, nb), 1)
    n_col = lax.broadcasted_iota(jnp.int32, (nb, 1), 0)
    cmask = (n_row * BLK + (BLK - 1)) <= qpos_r
    cmask_t = (n_col * BLK + (BLK - 1)) <= qpos_l
    forced_t = ((n_col == 0) | (n_col == cur_h) | (n_col == cur_h - 1)).astype(F32)
    valid_t = n_col <= cur_h
    k_in_chunk = lax.broadcasted_iota(jnp.int32, (KCH, 1), 0)

    def group_max(x):
        return jnp.max(x.reshape(KCH // 8, 8, lanes), axis=0)

    def group_sum(x):
        return jnp.sum(x.reshape(KCH // 8, 8, lanes), axis=0)

    def finish(lrun, ot):
        l = jnp.sum(lrun, axis=0, keepdims=True)
        return jnp.transpose(ot * _safe_inv(l))

    for g in range(N_KV):
        q = _q_heads(uq_ref, g, gq, cos2, sin2, HEAD_DIM ** -0.5)
        ksl = slice(g * HEAD_DIM, (g + 1) * HEAD_DIM)
        kc = kvc_ref[0, :, ksl].astype(BF16)
        vc = kvc_ref[0, :, KVW + g * HEAD_DIM:KVW + (g + 1) * HEAD_DIM].astype(BF16)

        p, _, l = _softmax_parts(_dot_nt(q, kc), cmask)
        o_cmp = _dot((p * _safe_inv(l)).astype(BF16), vc)

        st = jnp.where(cmask_t, _dot_nt(kc, q), NEG)
        pt = jnp.where(cmask_t, jnp.exp(st - jnp.max(st, axis=0, keepdims=True)), 0.0)
        pt = pt * _safe_inv(jnp.sum(pt, axis=0, keepdims=True))
        half = pt[:, 0:2 * t] + pt[:, 2 * t:4 * t]
        psum_t = half + pltpu.roll(half, t, 1)
        score_t = jnp.where(valid_t, psum_t + FORCE * forced_t, NEG)
        sel_t = jnp.where(valid_t, _rank_select_t(score_t, n_col, min(N_SEL, nb)), 0.0)
        bias_t = (sel_t - 1.0) * (-NEG)
        bias_scr[g] = jnp.concatenate([bias_t, bias_t], axis=1)

        q_scr[g] = q
        ocmp_scr[g] = o_cmp
        mrun_scr[g] = jnp.full((8, lanes), NEG, F32)
        lrun_scr[g] = jnp.zeros((8, lanes), F32)
        ot_scr[g] = jnp.zeros((HEAD_DIM, lanes), F32)

    def sel_pass1(sc):
        for g in range(N_KV):
            ksl = slice(g * HEAD_DIM, (g + 1) * HEAD_DIM)
            st = _dot_nt(ks_ref[0, sc * spk:(sc + 1) * spk, ksl], q_scr[g])
            mx = mrun_scr[g]
            for j in range(spk // BLK):
                n = sc * (spk // BLK) + j
                blk = st[j * BLK:(j + 1) * BLK] + bias_scr[g, n:n + 1, :]
                blk = jnp.where(n * BLK + k_in_blk <= qpos_l, blk, NEG)
                s_scr[g, n // bpc, (n % bpc) * BLK:(n % bpc + 1) * BLK, :] = blk
                mx = jnp.maximum(mx, jnp.max(blk.reshape(BLK // 8, 8, lanes), axis=0))
            mrun_scr[g] = mx

    def sel_pass2(sc):
        for g in range(N_KV):
            ksl = slice(g * HEAD_DIM, (g + 1) * HEAD_DIM)
            m = jnp.max(mrun_scr[g], axis=0, keepdims=True)
            lrun = lrun_scr[g]
            ot = ot_scr[g]
            for c in range(sc * (spk // KCH), (sc + 1) * (spk // KCH)):
                p = jnp.exp(s_scr[g, c] - m)
                lrun = lrun + group_sum(p)
                ot = ot + _dot(vst_ref[c, ksl, :], p.astype(BF16))
            lrun_scr[g] = lrun
            ot_scr[g] = ot

    for sweep in (sel_pass1, sel_pass2):
        for sc in range(seq // spk):
            if sc == 0:
                sweep(0)
            else:
                pl.when(sc * spk <= i * t)(functools.partial(sweep, sc))

    cs = jnp.minimum(jnp.maximum(i - WINDOW // BLK, 0) // bpc, nch - nwc)
    for g in range(N_KV):
        ksl = slice(g * HEAD_DIM, (g + 1) * HEAD_DIM)
        q = q_scr[g]
        o_sel = finish(lrun_scr[g], ot_scr[g])
        mrun = jnp.full((8, lanes), NEG, F32)
        for j in range(nwc):
            c = cs + j
            k = kw_ref[0, pl.ds(pl.multiple_of(c * KCH, KCH), KCH), ksl]
            dpos = qpos_l - (c * KCH + k_in_chunk)
            st = jnp.where((dpos >= 0) & (dpos < WINDOW), _dot_nt(k, q), NEG)
            s_scr[g, j] = st
            mrun = jnp.maximum(mrun, group_max(st))
        m = jnp.max(mrun, axis=0, keepdims=True)
        lrun = jnp.zeros((8, lanes), F32)
        ot = jnp.zeros((HEAD_DIM, lanes), F32)
        for j in range(nwc):
            p = jnp.exp(s_scr[g, j] - m)
            lrun = lrun + group_sum(p)
            ot = ot + _dot(vwt_ref[cs + j, ksl, :], p.astype(BF16))
        o_win = finish(lrun, ot)

        _gate_and_store(o_ref, g, (ocmp_scr[g], o_sel, o_win), (z0_ref, z1_ref, z2_ref), gate, t)


def _prompt_attn(u, gate, kvc, ks, vst, kw, vwt, cos2, sin2, gq, batch, seq):
    n = u.shape[0]
    t = BLK
    spk = _pick(seq, (SPAN_KEYS, KCH))
    assert GRP * t == 2 * KCH and seq % KCH == 0 and spk % KCH == 0
    nqb = seq // t
    nb = seq // BLK
    nch = seq // KCH
    lanes = GRP * t
    row = lambda b, i: (b * nqb + i, 0)
    keys = pl.BlockSpec((1, seq, KVW), lambda b, i: (b, 0, 0))
    vals_t = pl.BlockSpec((nch, KVW, KCH), lambda b, i: (b, 0, 0))
    return pl.pallas_call(
        functools.partial(_prompt_attn_kernel, seq=seq, spk=spk),
        grid=(batch, nqb),
        in_specs=[pl.BlockSpec((t, HD), row),
                  pl.BlockSpec((t, HD), lambda b, i: (b * nqb + i, 1)),
                  pl.BlockSpec((t, HD), lambda b, i: (b * nqb + i, 2)),
                  pl.BlockSpec((t, HD), lambda b, i: (b * nqb + i, 3)),
                  pl.BlockSpec((t, HEAD_DIM), row),
                  pl.BlockSpec((1, nb, 2 * KVW), lambda b, i: (b, 0, 0)),
                  keys, vals_t, keys, vals_t,
                  pl.BlockSpec((t, HEAD_DIM), lambda b, i: (i, 0)),
                  pl.BlockSpec((t, HEAD_DIM), lambda b, i: (i, 0)),
                  pl.BlockSpec((1, HEAD_DIM), lambda b, i: (0, 0))],
        out_specs=pl.BlockSpec((t, HD), row),
        out_shape=jax.ShapeDtypeStruct((n, HD), BF16),
        scratch_shapes=[pltpu.VMEM((N_KV, lanes, HEAD_DIM), BF16),
                        pltpu.VMEM((N_KV, nch, KCH, lanes), F32),
                        pltpu.VMEM((N_KV, nb, lanes), F32),
                        pltpu.VMEM((N_KV, 8, lanes), F32),
                        pltpu.VMEM((N_KV, 8, lanes), F32),
                        pltpu.VMEM((N_KV, HEAD_DIM, lanes), F32),
                        pltpu.VMEM((N_KV, lanes, HEAD_DIM), F32)],
        compiler_params=_cparams(2),
        name="prompt_attn",
    )(u, u, u, u, gate, kvc.reshape(batch, nb, 2 * KVW), ks.reshape(batch, seq, KVW), vst,
      kw.reshape(batch, seq, KVW), vwt, cos2, sin2, gq)


def _decode_front_kernel(uq_ref, kvc_ref, swin_ref, nwin_ref, cos_ref, sin_ref, gq_ref,
                         qn_ref, sel_ref, ocmp_ref, owin_ref, *, past, t):
    scale = HEAD_DIM ** -0.5
    cos2 = cos_ref[...]
    sin2 = sin_ref[...]
    gq = gq_ref[...]
    nbp = kvc_ref.shape[1]
    wbuf = swin_ref.shape[1]
    qpos = past + lax.broadcasted_iota(jnp.int32, (t, 1), 0)
    qpos_r = _tile_rows(qpos, GRP)
    cur = qpos // BLK
    n_i = lax.broadcasted_iota(jnp.int32, (1, nbp), 1)
    cmask = (n_i * BLK + (BLK - 1)) <= qpos_r
    kwpos_old = (past - wbuf) + lax.broadcasted_iota(jnp.int32, (1, wbuf), 1)
    kwpos_new = past + lax.broadcasted_iota(jnp.int32, (1, NEW_PAD), 1)
    d_old = qpos_r - kwpos_old
    d_new = qpos_r - kwpos_new
    m_old = (d_old >= 0) & (d_old < WINDOW) & (kwpos_old >= 0)
    m_new = (d_new >= 0) & (d_new < WINDOW) & (kwpos_new < past + t)

    for g in range(N_KV):
        q = _q_heads(uq_ref, g, gq, cos2, sin2)
        rows = slice(g * GRP * t, (g + 1) * GRP * t)
        ksl = slice(g * HEAD_DIM, (g + 1) * HEAD_DIM)
        vsl = slice(KVW + g * HEAD_DIM, KVW + (g + 1) * HEAD_DIM)
        qn_ref[0, rows, :] = q

        s = _dot_nt(q, kvc_ref[0, :, ksl].astype(BF16)) * scale
        p, _, l = _softmax_parts(s, cmask)
        p = p * _safe_inv(l)
        ocmp_ref[0, rows, :] = _dot(p.astype(BF16), kvc_ref[0, :, vsl].astype(BF16))

        psum = p[0:t]
        for r in range(1, GRP):
            psum = psum + p[r * t:(r + 1) * t]
        sel_ref[0, g * t:(g + 1) * t, :] = _select_blocks(psum, cur, n_i)

        s_old = _dot_nt(q, swin_ref[0, :, ksl].astype(BF16)) * scale
        s_new = _dot_nt(q, nwin_ref[0, :, ksl].astype(BF16)) * scale
        mx = jnp.maximum(jnp.max(jnp.where(m_old, s_old, NEG), axis=-1, keepdims=True),
                         jnp.max(jnp.where(m_new, s_new, NEG), axis=-1, keepdims=True))
        p_old = jnp.where(m_old, jnp.exp(jnp.where(m_old, s_old, NEG) - mx), 0.0)
        p_new = jnp.where(m_new, jnp.exp(jnp.where(m_new, s_new, NEG) - mx), 0.0)
        l = jnp.sum(p_old, axis=-1, keepdims=True) + jnp.sum(p_new, axis=-1, keepdims=True)
        o = _dot(p_old.astype(BF16), swin_ref[0, :, vsl].astype(BF16)) \
            + _dot(p_new.astype(BF16), nwin_ref[0, :, vsl].astype(BF16))
        owin_ref[0, rows, :] = o * _safe_inv(l)


def _decode_front(u, kvc_all, state_win2, new_win, cos2, sin2, gq, bd, t, past):
    nbp = kvc_all.shape[1]
    wbuf = state_win2.shape[1]
    rows = N_HEADS * t
    return pl.pallas_call(
        functools.partial(_decode_front_kernel, past=past, t=t),
        grid=(bd,),
        in_specs=[pl.BlockSpec((t, HD), lambda b: (b, 0)),
                  pl.BlockSpec((1, nbp, 2 * KVW), lambda b: (b, 0, 0)),
                  pl.BlockSpec((1, wbuf, 2 * KVW), lambda b: (b, 0, 0)),
                  pl.BlockSpec((1, NEW_PAD, 2 * KVW), lambda b: (b, 0, 0)),
                  pl.BlockSpec((t, HEAD_DIM), lambda b: (0, 0)),
                  pl.BlockSpec((t, HEAD_DIM), lambda b: (0, 0)),
                  pl.BlockSpec((1, HEAD_DIM), lambda b: (0, 0))],
        out_specs=[pl.BlockSpec((1, rows, HEAD_DIM), lambda b: (b, 0, 0)),
                   pl.BlockSpec((1, N_KV * t, nbp), lambda b: (b, 0, 0)),
                   pl.BlockSpec((1, rows, HEAD_DIM), lambda b: (b, 0, 0)),
                   pl.BlockSpec((1, rows, HEAD_DIM), lambda b: (b, 0, 0))],
        out_shape=[jax.ShapeDtypeStruct((bd, rows, HEAD_DIM), BF16),
                   jax.ShapeDtypeStruct((bd, N_KV * t, nbp), F32),
                   jax.ShapeDtypeStruct((bd, rows, HEAD_DIM), F32),
                   jax.ShapeDtypeStruct((bd, rows, HEAD_DIM), F32)],
        compiler_params=_cparams(1),
        name="decode_front",
    )(u, kvc_all, state_win2, new_win, cos2, sin2, gq)


def _decode_sel_kernel(pt_ref, *refs, t, n_steps):
    del pt_ref
    np_ = PAGES_PER_STEP
    (qn_ref, selc_ref, seln_ref) = refs[:3]
    k_refs = refs[3:3 + np_]
    v_refs = refs[3 + np_:3 + 2 * np_]
    (new_ref, ocmp_ref, owin_ref, z0_ref, z1_ref, z2_ref, gate_ref, e_ref,
     o_ref, m_ref, l_ref, acc_ref) = refs[3 + 2 * np_:]
    c = pl.program_id(1)
    scale = HEAD_DIM ** -0.5
    gt = GRP * t

    @pl.when(c == 0)
    def _():
        m_ref[...] = jnp.full(m_ref.shape, NEG, F32)
        l_ref[...] = jnp.zeros(l_ref.shape, F32)
        acc_ref[...] = jnp.zeros(acc_ref.shape, F32)

    def online_update(s, mask, pv_of):
        sm = jnp.where(mask, s, NEG)
        m_old = m_ref[...]
        m_new = jnp.maximum(m_old, jnp.max(sm, axis=-1, keepdims=True))
        alpha = jnp.exp(m_old - m_new)
        p = jnp.where(mask, jnp.exp(sm - m_new), 0.0)
        l_ref[...] = alpha * l_ref[...] + jnp.sum(p, axis=-1, keepdims=True)
        acc_ref[...] = alpha * acc_ref[...] + pv_of(p.astype(BF16))
        m_ref[...] = m_new

    def cache_rows(page_refs):
        rows = [pr[0, :, 0].reshape(pr.shape[1] * N_KV, HEAD_DIM) for pr in page_refs]
        return jnp.concatenate(rows, axis=0).astype(BF16)

    s = _dot_nt(qn_ref[0], cache_rows(k_refs)) * scale
    vx = cache_rows(v_refs)
    mask = jnp.concatenate(
        [_dot(selc_ref[0, 0, g * t:(g + 1) * t, :].astype(BF16), e_ref[g])
         for g in range(N_KV) for _ in range(GRP)], axis=0) > 0.5
    online_update(s, mask, lambda pb: _dot(pb, vx))

    @pl.when(c == n_steps - 1)
    def _():
        gate = jax.nn.sigmoid(gate_ref[...])
        tq = _tile_rows(lax.broadcasted_iota(jnp.int32, (t, 1), 0), N_HEADS)
        causal = lax.broadcasted_iota(jnp.int32, (1, NEW_PAD), 1) <= tq
        sel_new = jnp.concatenate([seln_ref[0, 0, g * t:(g + 1) * t, 0:1]
                                   for g in range(N_KV) for _ in range(GRP)], axis=0) > 0.5
        s_new = jnp.concatenate(
            [_dot_nt(qn_ref[0, g * gt:(g + 1) * gt, :],
                     new_ref[0, :, g * HEAD_DIM:(g + 1) * HEAD_DIM].astype(BF16))
             for g in range(N_KV)], axis=0) * scale

        def pv_new(pb):
            return jnp.concatenate(
                [_dot(pb[g * gt:(g + 1) * gt],
                      new_ref[0, :, KVW + g * HEAD_DIM:KVW + (g + 1) * HEAD_DIM].astype(BF16))
                 for g in range(N_KV)], axis=0)

        online_update(s_new, sel_new & causal, pv_new)
        o_sel = acc_ref[...] * _safe_inv(l_ref[...])
        for g in range(N_KV):
            rows = slice(g * gt, (g + 1) * gt)
            _gate_and_store(o_ref, g, (ocmp_ref[0, rows, :], o_sel[rows], owin_ref[0, rows, :]),
                            (z0_ref, z1_ref, z2_ref), gate, t)


def _decode_sel(page_table, qn, selc, cache_kv, new_sel, ocmp, owin, u, gate, e4, bd, t):
    n_pages = page_table.shape[1]
    page = cache_kv.shape[1]
    n_steps = n_pages // PAGES_PER_STEP
    bps = PAGES_PER_STEP * (page // BLK)
    rows = N_HEADS * t

    def page_spec(part, k):
        return pl.BlockSpec((1, page, 1, N_KV, HEAD_DIM),
                            lambda b, c, pt: (pt[b, c * PAGES_PER_STEP + k], 0, part, 0, 0))

    grid_spec = pltpu.PrefetchScalarGridSpec(
        num_scalar_prefetch=1,
        grid=(bd, n_steps),
        in_specs=[pl.BlockSpec((1, rows, HEAD_DIM), lambda b, c, pt: (b, 0, 0)),
                  pl.BlockSpec((1, 1, N_KV * t, bps), lambda b, c, pt: (b, c, 0, 0)),
                  pl.BlockSpec((1, 1, N_KV * t, bps), lambda b, c, pt: (b, n_steps, 0, 0))]
        + [page_spec(2, k) for k in range(PAGES_PER_STEP)]
        + [page_spec(3, k) for k in range(PAGES_PER_STEP)]
        + [pl.BlockSpec((1, NEW_PAD, 2 * KVW), lambda b, c, pt: (b, 0, 0)),
           pl.BlockSpec((1, rows, HEAD_DIM), lambda b, c, pt: (b, 0, 0)),
           pl.BlockSpec((1, rows, HEAD_DIM), lambda b, c, pt: (b, 0, 0)),
           pl.BlockSpec((t, HD), lambda b, c, pt: (b, 1)),
           pl.BlockSpec((t, HD), lambda b, c, pt: (b, 2)),
           pl.BlockSpec((t, HD), lambda b, c, pt: (b, 3)),
           pl.BlockSpec((t, HEAD_DIM), lambda b, c, pt: (b, 0)),
           pl.BlockSpec((N_KV, bps, bps * BLK * N_KV), lambda b, c, pt: (0, 0, 0))],
        out_specs=pl.BlockSpec((t, HD), lambda b, c, pt: (b, 0)),
        scratch_shapes=[pltpu.VMEM((rows, 1), F32), pltpu.VMEM((rows, 1), F32),
                        pltpu.VMEM((rows, HEAD_DIM), F32)],
    )
    return pl.pallas_call(
        functools.partial(_decode_sel_kernel, t=t, n_steps=n_steps),
        grid_spec=grid_spec,
        out_shape=jax.ShapeDtypeStruct((bd * t, HD), BF16),
        compiler_params=_cparams(2),
        name="decode_sel",
    )(page_table, qn, selc, selc, *([cache_kv] * (2 * PAGES_PER_STEP)), new_sel, ocmp, owin,
      u, u, u, gate, e4)


def _rope_tables(pos):
    half = HEAD_DIM // 2
    inv = ROPE_THETA ** (-jnp.arange(half, dtype=F32) / half)
    ang = pos.astype(F32)[:, None] * inv[None, :]
    c = jnp.cos(ang)
    s = jnp.sin(ang)
    return jnp.concatenate([c, c], axis=-1), jnp.concatenate([-s, s], axis=-1)


def _block_group_expand(nblocks):
    col = jnp.arange(nblocks * BLK * N_KV, dtype=jnp.int32)
    in_block = (col // N_KV) // BLK == jnp.arange(nblocks, dtype=jnp.int32)[:, None]
    in_group = col % N_KV == jnp.arange(N_KV, dtype=jnp.int32)[:, None]
    return (in_block[None, :, :] & in_group[:, None, :]).astype(BF16)


def _conv_layers(x, prev, seq, g_a, w_in_a, conv_w, w_out_a):
    states = []
    ch = conv_w.shape[-1]
    for layer in range(N_A):
        xn = _rms_cast(x, g_a[layer])
        u = _matmul(xn, w_in_a, layer, 4 * ch)
        y, st = _conv_gate(u, prev[layer], conv_w[layer], seq)
        x = _matmul(y, w_out_a, layer, x.shape[1], res=x)
        states.append(st)
    return x, jnp.stack(states)


def _shared_kv(x, pos, seq, g_kv, w_kv, g_k, for_attn):
    cos2, sin2 = _rope_tables(pos)
    kv = _matmul(_rms_cast(x, g_kv), w_kv, 0, 6 * KVW)
    return _kv_post(kv, cos2, sin2, g_k, seq, for_attn), cos2, sin2


def _query_side(x, lb, g_b, w_qz_t, w_gate_t):
    xn = _rms_cast(x, g_b[lb])
    return _matmul_t(xn, w_qz_t, lb, 4 * HD), _matmul_t(xn, w_gate_t, lb, HEAD_DIM)


def kernel(x_prompt, x_sample, state_conv, cache_kv, state_win, page_table, g_a, w_in_a, conv_w,
           w_out_a, g_kv, w_kv, g_k, w_cmp, g_b, w_qz, g_q, w_out_b):
    batch, seq, d = x_prompt.shape
    bd, t, _ = x_sample.shape
    ch = conv_w.shape[-1]
    n_pool, page = cache_kv.shape[:2]
    n_pages = page_table.shape[1]
    past = n_pages * page
    wbuf = state_win.shape[1]
    assert seq % BLK == 0 and page % BLK == 0 and t <= BLK and n_pages % PAGES_PER_STEP == 0

    w_exp3 = jnp.broadcast_to(w_cmp.reshape(BLK, 2 * N_KV, 1), (BLK, 2 * N_KV, HEAD_DIM))
    w_exp = w_exp3.reshape(BLK, 2 * KVW)
    n_gate = N_BRANCH * N_HEADS
    w_qz_t = jnp.swapaxes(w_qz, 1, 2)
    w_gate_t = jnp.pad(w_qz_t[:, 4 * HD:4 * HD + n_gate, :], ((0, 0), (0, HEAD_DIM - n_gate), (0, 0)))

    xp = x_prompt.reshape(batch * seq, d)
    prev0 = jnp.zeros((batch, CONV_W - 1, ch), F32)
    xn = _rms_cast(xp, g_a[0])
    states = []
    for layer in range(N_A):
        y, st = _conv_matmul(xn, w_in_a, layer, prev0, conv_w[layer], seq)
        states.append(st)
        gains = [g_a[layer + 1]] if layer + 1 < N_A else [g_kv, g_b[0]]
        xp, *xns = _matmul_res_norm(y, w_out_a, layer, xp, gains)
        xn = xns[0]
    conv_p = jnp.stack(states)
    xn_kv, xn_q = xns
    pos_p = jnp.arange(seq, dtype=jnp.int32)
    cos_p, sin_p = _rope_tables(pos_p)
    rows_p, win_p, ks_p, vst_p, kw_p, vwt_p = _kv_post(
        _matmul(xn_kv, w_kv, 0, 6 * KVW), cos_p, sin_p, g_k, seq, True)
    kvc_p = _compress(rows_p, w_exp)
    for lb in range(N_B):
        u = _matmul_t(xn_q, w_qz_t, lb, 4 * HD)
        gate = _matmul_t(xn_q, w_gate_t, lb, HEAD_DIM)
        o = _prompt_attn(u, gate, kvc_p, ks_p, vst_p, kw_p, vwt_p, cos_p, sin_p, g_q[lb:lb + 1],
                         batch, seq)
        if lb + 1 < N_B:
            xp, xn_q = _matmul_res_norm(o, w_out_b, lb, xp, [g_b[lb + 1]])
        else:
            xp = _matmul(o, w_out_b, lb, d, res=xp)
    wn = min(WINDOW, seq)
    y_prompt = xp.reshape(batch, seq, d)
    kv_rows_prompt = rows_p.reshape(batch, seq, 4, N_KV, HEAD_DIM)
    win_prompt = win_p.reshape(batch, seq, 2, N_KV, HEAD_DIM)[:, seq - wn:]

    xs = x_sample.reshape(bd * t, d)
    xs, conv_s = _conv_layers(xs, state_conv, t, g_a, w_in_a, conv_w, w_out_a)
    pos_s = past + jnp.arange(t, dtype=jnp.int32)
    (rows_s, win_s), cos_s, sin_s = _shared_kv(xs, pos_s, t, g_kv, w_kv, g_k, False)
    cache4 = cache_kv.reshape(n_pool, page, 4 * N_KV, HEAD_DIM)
    kvc_past = _compress_pages(cache4, page_table, w_exp3).reshape(bd, past // BLK, 2 * KVW)
    rows_s3 = rows_s.reshape(bd, t, 4 * KVW)
    rows_pad = jnp.pad(rows_s3, ((0, 0), (0, NEW_PAD - t), (0, 0)))
    kvc_new = _compress(rows_pad[:, :BLK, :2 * KVW].reshape(bd * BLK, 2 * KVW), w_exp)
    nb_past = past // BLK
    bps = PAGES_PER_STEP * (page // BLK)
    nbp = nb_past + bps
    kvc_all = jnp.concatenate(
        [kvc_past, kvc_new[:, None, :], jnp.zeros((bd, bps - 1, 2 * KVW), F32)], axis=1)
    new_sel = rows_pad[:, :, 2 * KVW:]
    new_win = jnp.pad(win_s.reshape(bd, t, 2 * KVW), ((0, 0), (0, NEW_PAD - t), (0, 0)))
    state_win2 = state_win.reshape(bd, wbuf, 2 * KVW)
    e4 = _block_group_expand(bps)
    for lb in range(N_B):
        u, gate = _query_side(xs, lb, g_b, w_qz_t, w_gate_t)
        qn, sel, ocmp, owin = _decode_front(u, kvc_all, state_win2, new_win, cos_s, sin_s,
                                            g_q[lb:lb + 1], bd, t, past)
        selc = sel.reshape(bd, N_KV * t, nbp // bps, bps).transpose(0, 2, 1, 3)
        o = _decode_sel(page_table, qn, selc, cache_kv, new_sel, ocmp, owin, u, gate, e4, bd, t)
        xs = _matmul(o, w_out_b, lb, d, res=xs)
    y_sample = xs.reshape(bd, t, d)
    kv_rows_sample = rows_s.reshape(bd, t, 4, N_KV, HEAD_DIM)
    kvw = jnp.concatenate([state_win, win_s.reshape(bd, t, 2, N_KV, HEAD_DIM)], axis=1)
    win_sample = kvw[:, kvw.shape[1] - min(WINDOW, kvw.shape[1]):]

    return (y_prompt, y_sample, conv_p, conv_s, kv_rows_prompt, kv_rows_sample, win_prompt, win_sample)
```

```python
import functools

import jax
import jax.numpy as jnp
from jax import lax
from jax.experimental import pallas as pl
from jax.experimental.pallas import tpu as pltpu

N_A = 2
N_B = 2
CONV_W = 3
N_HEADS = 16
N_KV = 4
GRP = N_HEADS // N_KV
HEAD_DIM = 128
N_BRANCH = 3
BLK = 64
N_SEL = 16
WINDOW = 512
ROPE_THETA = 10000.0
EPS = 1e-6
NEG = -1e30
FORCE = 1e3
LOG2E = 1.4426950408889634

HD = N_HEADS * HEAD_DIM
KVW = N_KV * HEAD_DIM
NEW_PAD = 128
KCH = 128
SPAN_KEYS = 512
PAGES_PER_STEP = 8
VMEM_LIMIT = 56 * 1024 * 1024

F32 = jnp.float32
BF16 = jnp.bfloat16


def _cparams(n_axes):
    return pltpu.CompilerParams(
        dimension_semantics=("arbitrary",) * n_axes, vmem_limit_bytes=VMEM_LIMIT)


def _dot_nt(a, b):
    return lax.dot_general(a, b, (((1,), (1,)), ((), ())), preferred_element_type=F32)


def _dot(a, b):
    return jnp.dot(a, b, preferred_element_type=F32)


def _pick(n, prefs):
    for p in prefs:
        if n % p == 0:
            return p
    return n


def _rms_cast_kernel(x_ref, g_ref, o_ref):
    x = x_ref[...]
    y = x * lax.rsqrt(jnp.mean(x * x, axis=-1, keepdims=True) + EPS)
    o_ref[...] = (y * g_ref[...]).astype(BF16)


def _rms_cast(x, g):
    n, d = x.shape
    tm = _pick(n, (512, 256, 128, 64, 8))
    return pl.pallas_call(
        _rms_cast_kernel,
        grid=(n // tm,),
        in_specs=[pl.BlockSpec((tm, d), lambda i: (i, 0)),
                  pl.BlockSpec((1, d), lambda i: (0, 0))],
        out_specs=pl.BlockSpec((tm, d), lambda i: (i, 0)),
        out_shape=jax.ShapeDtypeStruct((n, d), BF16),
        compiler_params=_cparams(1),
        name="rms_cast",
    )(x, g.reshape(1, d))


def _mm_kernel(*refs, has_res):
    if has_res:
        x_ref, w_ref, r_ref, o_ref, wb_ref = refs
    else:
        x_ref, w_ref, o_ref, wb_ref = refs

    @pl.when(pl.program_id(1) == 0)
    def _():
        wb_ref[...] = w_ref[...].astype(BF16)

    acc = _dot(x_ref[...], wb_ref[...])
    if has_res:
        acc = r_ref[...] + acc
    o_ref[...] = acc


def _matmul(x, w, layer, m_out, res=None):
    n, k = x.shape
    tm = _pick(n, (1024, 512, 256, 128, 64, 8))
    tn = _pick(m_out, (1024, 512, 256, 128))
    if w.ndim == 3:
        w_spec = pl.BlockSpec((None, k, tn), lambda j, i: (layer, 0, j))
    else:
        w_spec = pl.BlockSpec((k, tn), lambda j, i: (0, j))
    in_specs = [pl.BlockSpec((tm, k), lambda j, i: (i, 0)), w_spec]
    args = [x, w]
    if res is not None:
        in_specs.append(pl.BlockSpec((tm, tn), lambda j, i: (i, j)))
        args.append(res)
    return pl.pallas_call(
        functools.partial(_mm_kernel, has_res=res is not None),
        grid=(m_out // tn, n // tm),
        in_specs=in_specs,
        out_specs=pl.BlockSpec((tm, tn), lambda j, i: (i, j)),
        out_shape=jax.ShapeDtypeStruct((n, m_out), F32),
        scratch_shapes=[pltpu.VMEM((k, tn), BF16)],
        compiler_params=_cparams(2),
        name="matmul_res" if res is not None else "matmul",
    )(*args)


def _mm_t_kernel(x_ref, w_ref, o_ref, wb_ref):
    @pl.when(pl.program_id(1) == 0)
    def _():
        wb_ref[...] = jnp.transpose(w_ref[...]).astype(BF16)

    o_ref[...] = _dot(x_ref[...], wb_ref[...])


def _matmul_t(x, w_t, layer, m_out):
    n, k = x.shape
    tm = _pick(n, (1024, 512, 256, 128, 64, 8))
    tn = _pick(m_out, (1024, 512, 256, 128))
    return pl.pallas_call(
        _mm_t_kernel,
        grid=(m_out // tn, n // tm),
        in_specs=[pl.BlockSpec((tm, k), lambda j, i: (i, 0)),
                  pl.BlockSpec((None, tn, k), lambda j, i: (layer, j, 0))],
        out_specs=pl.BlockSpec((tm, tn), lambda j, i: (i, j)),
        out_shape=jax.ShapeDtypeStruct((n, m_out), F32),
        scratch_shapes=[pltpu.VMEM((k, tn), BF16)],
        compiler_params=_cparams(2),
        name="matmul_t",
    )(x, w_t)


def _mm_res_norm_kernel(x_ref, w_ref, r_ref, *rest, n_norm):
    g_refs = rest[:n_norm]
    o_ref = rest[n_norm]
    xn_refs = rest[n_norm + 1:2 * n_norm + 1]
    wb_ref = rest[2 * n_norm + 1]

    @pl.when(pl.program_id(0) == 0)
    def _():
        wb_ref[...] = w_ref[...].astype(BF16)

    out = r_ref[...] + _dot(x_ref[...], wb_ref[...])
    o_ref[...] = out
    y = out * lax.rsqrt(jnp.mean(out * out, axis=-1, keepdims=True) + EPS)
    for g_ref, xn_ref in zip(g_refs, xn_refs):
        xn_ref[...] = (y * g_ref[...]).astype(BF16)


def _matmul_res_norm(x, w, layer, res, gains):
    n, k = x.shape
    d = res.shape[1]
    tm = _pick(n, (512, 256, 128, 64, 8))
    row = lambda i: (i, 0)
    n_norm = len(gains)
    return pl.pallas_call(
        functools.partial(_mm_res_norm_kernel, n_norm=n_norm),
        grid=(n // tm,),
        in_specs=[pl.BlockSpec((tm, k), row),
                  pl.BlockSpec((None, k, d), lambda i: (layer, 0, 0), pipeline_mode=pl.Buffered(1)),
                  pl.BlockSpec((tm, d), row)]
        + [pl.BlockSpec((1, d), lambda i: (0, 0))] * n_norm,
        out_specs=[pl.BlockSpec((tm, d), row)] * (1 + n_norm),
        out_shape=[jax.ShapeDtypeStruct((n, d), F32)] + [jax.ShapeDtypeStruct((n, d), BF16)] * n_norm,
        scratch_shapes=[pltpu.VMEM((k, d), BF16)],
        compiler_params=_cparams(1),
        name="matmul_res_norm",
    )(x, w, res, *[g.reshape(1, d) for g in gains])


def _conv_mm_kernel(x_ref, wb_ref, wc_ref, wh_ref, wz_ref, prev_ref, cw_ref, y_ref, st_ref,
                    wbf_ref, carry_ref, *, tps):
    i = pl.program_id(1)

    @pl.when(i == 0)
    def _():
        for k, wr in enumerate((wb_ref, wc_ref, wh_ref, wz_ref)):
            wbf_ref[k] = wr[...].astype(BF16)

    x = x_ref[...]
    c = _dot(x, wbf_ref[1]) * _dot(x, wbf_ref[2])
    y_ref[...] = _gated_conv(c, _dot(x, wbf_ref[0]), _dot(x, wbf_ref[3]), i, tps,
                             prev_ref, cw_ref, st_ref, carry_ref).astype(BF16)


def _conv_matmul(xn, w_in, layer, prev, w_conv, seq):
    n, k = xn.shape
    ch = w_in.shape[2] // 4
    tm = _pick(seq, (1024, 512, 256, 128, 64, 8))
    tc = _pick(ch, (256, 128))
    tps = seq // tm
    nc = ch // tc

    def w_spec(part):
        return pl.BlockSpec((None, k, tc), lambda j, i: (layer, 0, part * nc + j))

    return pl.pallas_call(
        functools.partial(_conv_mm_kernel, tps=tps),
        grid=(nc, n // tm),
        in_specs=[pl.BlockSpec((tm, k), lambda j, i: (i, 0))] + [w_spec(p) for p in range(4)]
        + [pl.BlockSpec((1, 2, tc), lambda j, i: (i // tps, 0, j)),
           pl.BlockSpec((CONV_W, tc), lambda j, i: (0, j))],
        out_specs=[pl.BlockSpec((tm, tc), lambda j, i: (i, j)),
                   pl.BlockSpec((1, 2, tc), lambda j, i: (i // tps, 0, j))],
        out_shape=[jax.ShapeDtypeStruct((n, ch), BF16),
                   jax.ShapeDtypeStruct((n // seq, CONV_W - 1, ch), F32)],
        scratch_shapes=[pltpu.VMEM((4, k, tc), BF16), pltpu.VMEM((8, tc), F32)],
        compiler_params=_cparams(2),
        name="conv_matmul",
    )(xn, w_in, w_in, w_in, w_in, prev, w_conv)


def _gated_conv(c, b, z, i, tps, prev_ref, w_ref, st_ref, carry_ref):
    tm = c.shape[0]

    @pl.when(i % tps == 0)
    def _():
        carry_ref[0:2, :] = prev_ref[0]

    p0 = carry_ref[0:1, :]
    p1 = carry_ref[1:2, :]
    row = lax.broadcasted_iota(jnp.int32, c.shape, 0)
    c1 = jnp.where(row == 0, p1, pltpu.roll(c, 1, 0))
    c2 = jnp.where(row == 0, p0, jnp.where(row == 1, p1, pltpu.roll(c, 2, 0)))
    conv = w_ref[0:1, :] * c2 + w_ref[1:2, :] * c1 + w_ref[2:3, :] * c
    last = c[tm - 2:tm, :]
    carry_ref[0:2, :] = last
    st_ref[0] = last
    return jax.nn.silu(z) * (b * conv)


def _conv_gate_kernel(b_ref, cg_ref, h_ref, z_ref, prev_ref, w_ref, y_ref, st_ref, carry_ref, *, tps):
    c = cg_ref[...] * h_ref[...]
    y_ref[...] = _gated_conv(c, b_ref[...], z_ref[...], pl.program_id(1), tps,
                             prev_ref, w_ref, st_ref, carry_ref).astype(BF16)


def _conv_gate(u, prev, w_conv, seq):
    n = u.shape[0]
    ch = u.shape[1] // 4
    tm = _pick(seq, (256, 128, 64, 8))
    tc = _pick(ch, (512, 256, 128))
    tps = seq // tm
    nc = ch // tc
    nb = n // seq
    return pl.pallas_call(
        functools.partial(_conv_gate_kernel, tps=tps),
        grid=(nc, n // tm),
        in_specs=[pl.BlockSpec((tm, tc), lambda j, i: (i, j)),
                  pl.BlockSpec((tm, tc), lambda j, i: (i, nc + j)),
                  pl.BlockSpec((tm, tc), lambda j, i: (i, 2 * nc + j)),
                  pl.BlockSpec((tm, tc), lambda j, i: (i, 3 * nc + j)),
                  pl.BlockSpec((1, 2, tc), lambda j, i: (i // tps, 0, j)),
                  pl.BlockSpec((CONV_W, tc), lambda j, i: (0, j))],
        out_specs=[pl.BlockSpec((tm, tc), lambda j, i: (i, j)),
                   pl.BlockSpec((1, 2, tc), lambda j, i: (i // tps, 0, j))],
        out_shape=[jax.ShapeDtypeStruct((n, ch), BF16),
                   jax.ShapeDtypeStruct((nb, CONV_W - 1, ch), F32)],
        scratch_shapes=[pltpu.VMEM((8, tc), F32)],
        compiler_params=_cparams(2),
        name="conv_gate",
    )(u, u, u, u, prev, w_conv)


def _norm_rope(x, g, cos2, sin2):
    y = x * lax.rsqrt(jnp.mean(x * x, axis=-1, keepdims=True) + EPS) * g
    return y * cos2 + pltpu.roll(y, HEAD_DIM // 2, 1) * sin2


def _kv_post_kernel(kv_ref, cos_ref, sin_ref, gk_ref, rows_ref, win_ref, *attn_refs):
    cos2 = cos_ref[...]
    sin2 = sin_ref[...]
    tm = kv_ref.shape[0]
    for br in range(N_BRANCH):
        base = br * 2 * KVW
        out_ref, row0 = (rows_ref, br * 2 * N_KV) if br < 2 else (win_ref, 0)
        for g in range(N_KV):
            c0 = base + g * HEAD_DIM
            k = _norm_rope(kv_ref[:, c0:c0 + HEAD_DIM], gk_ref[br:br + 1, :], cos2, sin2)
            out_ref[:, row0 + g, :] = k
            out_ref[:, row0 + N_KV + g, :] = kv_ref[:, c0 + KVW:c0 + KVW + HEAD_DIM]
            if attn_refs and br >= 1:
                attn_refs[2 * (br - 1)][:, g * HEAD_DIM:(g + 1) * HEAD_DIM] = k.astype(BF16)
        if attn_refs and br >= 1:
            v = kv_ref[:, base + KVW:base + 2 * KVW]
            vt_ref = attn_refs[2 * (br - 1) + 1]
            for a in range(tm // KCH):
                vt_ref[a] = jnp.transpose(v[a * KCH:(a + 1) * KCH, :]).astype(BF16)


def _kv_post(kv, cos2, sin2, g_k, seq, for_attn):
    n = kv.shape[0]
    tm = _pick(seq, (256, 128, 64, 8))
    tps = seq // tm
    out_specs = [pl.BlockSpec((tm, 4 * N_KV, HEAD_DIM), lambda i: (i, 0, 0)),
                 pl.BlockSpec((tm, 2 * N_KV, HEAD_DIM), lambda i: (i, 0, 0))]
    out_shape = [jax.ShapeDtypeStruct((n, 4 * N_KV, HEAD_DIM), F32),
                 jax.ShapeDtypeStruct((n, 2 * N_KV, HEAD_DIM), F32)]
    if for_attn:
        assert tm % KCH == 0
        for _ in range(2):
            out_specs += [pl.BlockSpec((tm, KVW), lambda i: (i, 0)),
                          pl.BlockSpec((tm // KCH, KVW, KCH), lambda i: (i, 0, 0))]
            out_shape += [jax.ShapeDtypeStruct((n, KVW), BF16),
                          jax.ShapeDtypeStruct((n // KCH, KVW, KCH), BF16)]
    return pl.pallas_call(
        _kv_post_kernel,
        grid=(n // tm,),
        in_specs=[pl.BlockSpec((tm, 6 * KVW), lambda i: (i, 0)),
                  pl.BlockSpec((tm, HEAD_DIM), lambda i: (i % tps, 0)),
                  pl.BlockSpec((tm, HEAD_DIM), lambda i: (i % tps, 0)),
                  pl.BlockSpec((N_BRANCH, HEAD_DIM), lambda i: (0, 0))],
        out_specs=out_specs,
        out_shape=out_shape,
        compiler_params=_cparams(1),
        name="kv_post",
    )(kv, cos2, sin2, g_k)


def _compress_kernel(rows_ref, w_ref, o_ref):
    x = rows_ref[...]
    nb = x.shape[0] // BLK
    x = x.reshape(nb, BLK, x.shape[1]) * w_ref[...][None]
    o_ref[...] = jnp.sum(x, axis=1)


def _compress(rows, w_exp):
    n = rows.shape[0]
    nblk = n // BLK
    per = _pick(nblk, (8,))
    return pl.pallas_call(
        _compress_kernel,
        grid=(nblk // per,),
        in_specs=[pl.BlockSpec((per * BLK, 2 * KVW), lambda i: (i, 0)),
                  pl.BlockSpec((BLK, 2 * KVW), lambda i: (0, 0))],
        out_specs=pl.BlockSpec((per, 2 * KVW), lambda i: (i, 0)),
        out_shape=jax.ShapeDtypeStruct((nblk, 2 * KVW), F32),
        compiler_params=_cparams(1),
        name="compress",
    )(rows, w_exp)


def _block_sums(x, w):
    nb = x.shape[0] // BLK
    return jnp.sum(x.reshape(nb, BLK, 2 * N_KV, HEAD_DIM) * w[None], axis=1)


def _compress_rows_kernel(rows_ref, w_ref, o_ref):
    o_ref[...] = _block_sums(rows_ref[...], w_ref[...])


def _compress_rows(rows3, w_exp3):
    n = rows3.shape[0]
    nblk = n // BLK
    per = _pick(nblk, (8,))
    return pl.pallas_call(
        _compress_rows_kernel,
        grid=(nblk // per,),
        in_specs=[pl.BlockSpec((per * BLK, 2 * N_KV, HEAD_DIM), lambda i: (i, 0, 0)),
                  pl.BlockSpec((BLK, 2 * N_KV, HEAD_DIM), lambda i: (0, 0, 0))],
        out_specs=pl.BlockSpec((per, 2 * N_KV, HEAD_DIM), lambda i: (i, 0, 0)),
        out_shape=jax.ShapeDtypeStruct((nblk, 2 * N_KV, HEAD_DIM), F32),
        compiler_params=_cparams(1),
        name="compress_rows",
    )(rows3, w_exp3)


def _compress_pages_kernel(pt_ref, *refs):
    page_refs = refs[:PAGES_PER_STEP]
    w_ref = refs[PAGES_PER_STEP]
    o_ref = refs[PAGES_PER_STEP + 1]
    w = w_ref[...]
    for k, pr in enumerate(page_refs):
        nb = pr.shape[1] // BLK
        o_ref[0, k * nb:(k + 1) * nb] = _block_sums(pr[0], w)


def _compress_pages(cache4, page_table, w_exp3):
    bd, n_pages = page_table.shape
    page = cache4.shape[1]
    bpp = page // BLK
    steps = n_pages // PAGES_PER_STEP
    rows_out = PAGES_PER_STEP * bpp

    def page_spec(k):
        return pl.BlockSpec((1, page, 2 * N_KV, HEAD_DIM),
                            lambda b, c, pt: (pt[b, c * PAGES_PER_STEP + k], 0, 0, 0))

    grid_spec = pltpu.PrefetchScalarGridSpec(
        num_scalar_prefetch=1,
        grid=(bd, steps),
        in_specs=[page_spec(k) for k in range(PAGES_PER_STEP)]
        + [pl.BlockSpec((BLK, 2 * N_KV, HEAD_DIM), lambda b, c, pt: (0, 0, 0))],
        out_specs=pl.BlockSpec((1, rows_out, 2 * N_KV, HEAD_DIM), lambda b, c, pt: (b, c, 0, 0)),
    )
    return pl.pallas_call(
        _compress_pages_kernel,
        grid_spec=grid_spec,
        out_shape=jax.ShapeDtypeStruct((bd, n_pages * bpp, 2 * N_KV, HEAD_DIM), F32),
        compiler_params=_cparams(2),
        name="compress_pages",
    )(page_table, *([cache4] * PAGES_PER_STEP), w_exp3)


def _softmax_parts(s, mask):
    sm = jnp.where(mask, s, NEG)
    m = jnp.max(sm, axis=-1, keepdims=True)
    p = jnp.where(mask, jnp.exp(sm - m), 0.0)
    return p, m, jnp.sum(p, axis=-1, keepdims=True)


def _safe_inv(l):
    return jnp.where(l > 0.0, 1.0 / l, 0.0)


def _topk_mask(score, n_f, k):
    sel = jnp.zeros(score.shape, F32)
    big = jnp.float32(score.shape[-1] + 1)
    for _ in range(k):
        mx = jnp.max(score, axis=-1, keepdims=True)
        idx = jnp.min(jnp.where(score == mx, n_f, big), axis=-1, keepdims=True)
        hit = n_f == idx
        sel = jnp.where(hit, 1.0, sel)
        score = jnp.where(hit, -jnp.inf, score)
    return sel


def _select_blocks(psum, cur, n_i):
    forced = ((n_i == 0) | (n_i == cur) | (n_i == cur - 1)).astype(F32)
    valid = n_i <= cur
    score = jnp.where(valid, psum + FORCE * forced, NEG)
    sel = _topk_mask(score, n_i.astype(F32), min(N_SEL, psum.shape[-1]))
    return jnp.where(valid, sel, 0.0)


def _q_heads(uq_ref, g, gq, cos2, sin2, scale=None):
    hs = []
    for r in range(GRP):
        c0 = (g * GRP + r) * HEAD_DIM
        h = _norm_rope(uq_ref[:, c0:c0 + HEAD_DIM], gq, cos2, sin2)
        hs.append(h if scale is None else h * scale)
    return jnp.concatenate(hs, axis=0).astype(BF16)


def _tile_rows(x, reps):
    return jnp.concatenate([x] * reps, axis=0)


def _gate_and_store(o_ref, g, outs, z_refs, gate, t):
    for r in range(GRP):
        h = g * GRP + r
        acc = None
        for br in range(N_BRANCH):
            zg = jax.nn.silu(z_refs[br][:, h * HEAD_DIM:(h + 1) * HEAD_DIM]) \
                * gate[:, br * N_HEADS + h:br * N_HEADS + h + 1]
            term = outs[br][r * t:(r + 1) * t, :] * zg
            acc = term if acc is None else acc + term
        o_ref[:, h * HEAD_DIM:(h + 1) * HEAD_DIM] = acc.astype(o_ref.dtype)


def _rank_select_t(score_t, n_col, k):
    rank = jnp.zeros(score_t.shape, F32)
    for m in range(score_t.shape[0]):
        row = score_t[m:m + 1, :]
        tie = (n_col > m).astype(F32)
        rank = rank + jnp.where(row > score_t, 1.0, 0.0) + jnp.where(row == score_t, tie, 0.0)
    return jnp.where(rank < k, 1.0, 0.0)


def _prompt_attn_kernel(uq_ref, z0_ref, z1_ref, z2_ref, gate_ref, kvc_ref, vct_ref, ks_ref, vst_ref,
                        kw_ref, vwt_ref, cos_ref, sin_ref, gq_ref, o_ref,
                        q_scr, s_scr, bias_scr, mrun_scr, lrun_scr, ot_scr, ocmp_scr, *, seq, spk):
    i = pl.program_id(1)
    t = BLK
    lanes = GRP * t
    nb = seq // BLK
    nch = seq // KCH
    nwc = min(WINDOW // KCH + 1, nch)
    bpc = KCH // BLK
    k_in_blk = lax.broadcasted_iota(jnp.int32, (BLK, 1), 0)
    cos2 = cos_ref[...]
    sin2 = sin_ref[...]
    gq = gq_ref[...]
    gate = jax.nn.sigmoid(gate_ref[...])
    qpos_l = i * t + (lax.broadcasted_iota(jnp.int32, (1, lanes), 1) & (t - 1))
    qpos_h = qpos_l[:, 0:2 * t]
    cur_h = qpos_h // BLK
    n_col = lax.broadcasted_iota(jnp.int32, (nb, 1), 0)
    cmask_t = (n_col * BLK + (BLK - 1)) <= qpos_l
    forced_t = ((n_col == 0) | (n_col == cur_h) | (n_col == cur_h - 1)).astype(F32)
    valid_t = n_col <= cur_h
    k_in_chunk = lax.broadcasted_iota(jnp.int32, (KCH, 1), 0)

    def group_max(x):
        return jnp.max(x.reshape(KCH // 8, 8, lanes), axis=0)

    def group_sum(x):
        return jnp.sum(x.reshape(KCH // 8, 8, lanes), axis=0)

    def finish(lrun, ot):
        l = jnp.sum(lrun, axis=0, keepdims=True)
        return jnp.transpose(ot * _safe_inv(l))

    for g in range(N_KV):
        q = _q_heads(uq_ref, g, gq, cos2, sin2, LOG2E * HEAD_DIM ** -0.5)
        ksl = slice(g * HEAD_DIM, (g + 1) * HEAD_DIM)
        kc = kvc_ref[0, :, ksl].astype(BF16)

        st = jnp.where(cmask_t, _dot_nt(kc, q), NEG)
        pt = jnp.where(cmask_t, jnp.exp2(st - jnp.max(st, axis=0, keepdims=True)), 0.0)
        pt = pt * _safe_inv(jnp.sum(pt, axis=0, keepdims=True))
        ocmp_scr[g] = _dot(vct_ref[0, ksl, :].astype(BF16), pt.astype(BF16))

        half = pt[:, 0:2 * t] + pt[:, 2 * t:4 * t]
        psum_t = half + pltpu.roll(half, t, 1)
        score_t = jnp.where(valid_t, psum_t + FORCE * forced_t, NEG)
        sel_t = jnp.where(valid_t, _rank_select_t(score_t, n_col, min(N_SEL, nb)), 0.0)
        bias_t = (sel_t - 1.0) * (-NEG)
        bias_scr[g] = jnp.concatenate([bias_t, bias_t], axis=1)

        q_scr[g] = q
        mrun_scr[g] = jnp.full((8, lanes), NEG, F32)
        lrun_scr[g] = jnp.zeros((8, lanes), F32)
        ot_scr[g] = jnp.zeros((HEAD_DIM, lanes), F32)

    def sel_pass1(sc, diagonal):
        for g in range(N_KV):
            ksl = slice(g * HEAD_DIM, (g + 1) * HEAD_DIM)
            st = _dot_nt(ks_ref[0, sc * spk:(sc + 1) * spk, ksl], q_scr[g])
            mx = mrun_scr[g]
            for j in range(spk // BLK):
                n = sc * (spk // BLK) + j
                blk = st[j * BLK:(j + 1) * BLK] + bias_scr[g, n:n + 1, :]
                if diagonal:
                    blk = jnp.where(n * BLK + k_in_blk <= qpos_l, blk, NEG)
                s_scr[g, n // bpc, (n % bpc) * BLK:(n % bpc + 1) * BLK, :] = blk
                mx = jnp.maximum(mx, jnp.max(blk.reshape(BLK // 8, 8, lanes), axis=0))
            mrun_scr[g] = mx

    def sel_pass2(sc):
        for g in range(N_KV):
            ksl = slice(g * HEAD_DIM, (g + 1) * HEAD_DIM)
            m = jnp.max(mrun_scr[g], axis=0, keepdims=True)
            lrun = lrun_scr[g]
            ot = ot_scr[g]
            for c in range(sc * (spk // KCH), (sc + 1) * (spk // KCH)):
                p = jnp.exp2(s_scr[g, c] - m)
                lrun = lrun + group_sum(p)
                ot = ot + _dot(vst_ref[c, ksl, :], p.astype(BF16))
            lrun_scr[g] = lrun
            ot_scr[g] = ot

    for sc in range(seq // spk):
        pl.when((sc + 1) * spk <= i * t)(functools.partial(sel_pass1, sc, False))
        pl.when((sc * spk <= i * t) & (i * t < (sc + 1) * spk))(functools.partial(sel_pass1, sc, True))
    for sc in range(seq // spk):
        pl.when(sc * spk <= i * t)(functools.partial(sel_pass2, sc))

    cs = jnp.minimum(jnp.maximum(i - WINDOW // BLK, 0) // bpc, nch - nwc)
    for g in range(N_KV):
        ksl = slice(g * HEAD_DIM, (g + 1) * HEAD_DIM)
        q = q_scr[g]
        o_sel = finish(lrun_scr[g], ot_scr[g])
        mrun = jnp.full((8, lanes), NEG, F32)
        for j in range(nwc):
            c = cs + j
            k = kw_ref[0, pl.ds(pl.multiple_of(c * KCH, KCH), KCH), ksl]
            dpos = qpos_l - (c * KCH + k_in_chunk)
            st = jnp.where((dpos >= 0) & (dpos < WINDOW), _dot_nt(k, q), NEG)
            s_scr[g, j] = st
            mrun = jnp.maximum(mrun, group_max(st))
        m = jnp.max(mrun, axis=0, keepdims=True)
        lrun = jnp.zeros((8, lanes), F32)
        ot = jnp.zeros((HEAD_DIM, lanes), F32)
        for j in range(nwc):
            p = jnp.exp2(s_scr[g, j] - m)
            lrun = lrun + group_sum(p)
            ot = ot + _dot(vwt_ref[cs + j, ksl, :], p.astype(BF16))
        o_win = finish(lrun, ot)

        _gate_and_store(o_ref, g, (jnp.transpose(ocmp_scr[g]), o_sel, o_win),
                        (z0_ref, z1_ref, z2_ref), gate, t)


def _prompt_attn(u, gate, kvc, vct, ks, vst, kw, vwt, cos2, sin2, gq, batch, seq):
    n = u.shape[0]
    t = BLK
    spk = _pick(seq, (SPAN_KEYS, KCH))
    assert GRP * t == 2 * KCH and seq % KCH == 0 and spk % KCH == 0
    nqb = seq // t
    nb = seq // BLK
    nch = seq // KCH
    lanes = GRP * t
    row = lambda b, i: (b * nqb + i, 0)
    keys = pl.BlockSpec((1, seq, KVW), lambda b, i: (b, 0, 0))
    vals_t = pl.BlockSpec((nch, KVW, KCH), lambda b, i: (b, 0, 0))
    return pl.pallas_call(
        functools.partial(_prompt_attn_kernel, seq=seq, spk=spk),
        grid=(batch, nqb),
        in_specs=[pl.BlockSpec((t, HD), row),
                  pl.BlockSpec((t, HD), lambda b, i: (b * nqb + i, 1)),
                  pl.BlockSpec((t, HD), lambda b, i: (b * nqb + i, 2)),
                  pl.BlockSpec((t, HD), lambda b, i: (b * nqb + i, 3)),
                  pl.BlockSpec((t, HEAD_DIM), row),
                  pl.BlockSpec((1, nb, 2 * KVW), lambda b, i: (b, 0, 0)),
                  pl.BlockSpec((1, KVW, nb), lambda b, i: (b, 0, 0)),
                  keys, vals_t, keys, vals_t,
                  pl.BlockSpec((t, HEAD_DIM), lambda b, i: (i, 0)),
                  pl.BlockSpec((t, HEAD_DIM), lambda b, i: (i, 0)),
                  pl.BlockSpec((1, HEAD_DIM), lambda b, i: (0, 0))],
        out_specs=pl.BlockSpec((t, HD), row),
        out_shape=jax.ShapeDtypeStruct((n, HD), BF16),
        scratch_shapes=[pltpu.VMEM((N_KV, lanes, HEAD_DIM), BF16),
                        pltpu.VMEM((N_KV, nch, KCH, lanes), F32),
                        pltpu.VMEM((N_KV, nb, lanes), F32),
                        pltpu.VMEM((N_KV, 8, lanes), F32),
                        pltpu.VMEM((N_KV, 8, lanes), F32),
                        pltpu.VMEM((N_KV, HEAD_DIM, lanes), F32),
                        pltpu.VMEM((N_KV, HEAD_DIM, lanes), F32)],
        compiler_params=_cparams(2),
        name="prompt_attn",
    )(u, u, u, u, gate, kvc.reshape(batch, nb, 2 * KVW), vct, ks.reshape(batch, seq, KVW), vst,
      kw.reshape(batch, seq, KVW), vwt, cos2, sin2, gq)


def _decode_front_kernel(uq_ref, kvc_ref, swin_ref, nwin_ref, cos_ref, sin_ref, gq_ref,
                         qn_ref, sel_ref, ocmp_ref, owin_ref, *, past, t):
    scale = HEAD_DIM ** -0.5
    cos2 = cos_ref[...]
    sin2 = sin_ref[...]
    gq = gq_ref[...]
    nbp = kvc_ref.shape[1]
    wbuf = swin_ref.shape[1]
    qpos = past + lax.broadcasted_iota(jnp.int32, (t, 1), 0)
    qpos_r = _tile_rows(qpos, GRP)
    cur = qpos // BLK
    n_i = lax.broadcasted_iota(jnp.int32, (1, nbp), 1)
    cmask = (n_i * BLK + (BLK - 1)) <= qpos_r
    kwpos_old = (past - wbuf) + lax.broadcasted_iota(jnp.int32, (1, wbuf), 1)
    kwpos_new = past + lax.broadcasted_iota(jnp.int32, (1, NEW_PAD), 1)
    d_old = qpos_r - kwpos_old
    d_new = qpos_r - kwpos_new
    m_old = (d_old >= 0) & (d_old < WINDOW) & (kwpos_old >= 0)
    m_new = (d_new >= 0) & (d_new < WINDOW) & (kwpos_new < past + t)

    for g in range(N_KV):
        q = _q_heads(uq_ref, g, gq, cos2, sin2)
        rows = slice(g * GRP * t, (g + 1) * GRP * t)
        ksl = slice(g * HEAD_DIM, (g + 1) * HEAD_DIM)
        vsl = slice(KVW + g * HEAD_DIM, KVW + (g + 1) * HEAD_DIM)
        qn_ref[0, rows, :] = q

        s = _dot_nt(q, kvc_ref[0, :, ksl].astype(BF16)) * scale
        p, _, l = _softmax_parts(s, cmask)
        p = p * _safe_inv(l)
        ocmp_ref[0, rows, :] = _dot(p.astype(BF16), kvc_ref[0, :, vsl].astype(BF16))

        psum = p[0:t]
        for r in range(1, GRP):
            psum = psum + p[r * t:(r + 1) * t]
        sel_ref[0, g * t:(g + 1) * t, :] = _select_blocks(psum, cur, n_i)

        s_old = _dot_nt(q, swin_ref[0, :, ksl].astype(BF16)) * scale
        s_new = _dot_nt(q, nwin_ref[0, :, ksl].astype(BF16)) * scale
        mx = jnp.maximum(jnp.max(jnp.where(m_old, s_old, NEG), axis=-1, keepdims=True),
                         jnp.max(jnp.where(m_new, s_new, NEG), axis=-1, keepdims=True))
        p_old = jnp.where(m_old, jnp.exp(jnp.where(m_old, s_old, NEG) - mx), 0.0)
        p_new = jnp.where(m_new, jnp.exp(jnp.where(m_new, s_new, NEG) - mx), 0.0)
        l = jnp.sum(p_old, axis=-1, keepdims=True) + jnp.sum(p_new, axis=-1, keepdims=True)
        o = _dot(p_old.astype(BF16), swin_ref[0, :, vsl].astype(BF16)) \
            + _dot(p_new.astype(BF16), nwin_ref[0, :, vsl].astype(BF16))
        owin_ref[0, rows, :] = o * _safe_inv(l)


def _decode_front(u, kvc_all, state_win2, new_win, cos2, sin2, gq, bd, t, past):
    nbp = kvc_all.shape[1]
    wbuf = state_win2.shape[1]
    rows = N_HEADS * t
    return pl.pallas_call(
        functools.partial(_decode_front_kernel, past=past, t=t),
        grid=(bd,),
        in_specs=[pl.BlockSpec((t, HD), lambda b: (b, 0)),
                  pl.BlockSpec((1, nbp, 2 * KVW), lambda b: (b, 0, 0)),
                  pl.BlockSpec((1, wbuf, 2 * KVW), lambda b: (b, 0, 0)),
                  pl.BlockSpec((1, NEW_PAD, 2 * KVW), lambda b: (b, 0, 0)),
                  pl.BlockSpec((t, HEAD_DIM), lambda b: (0, 0)),
                  pl.BlockSpec((t, HEAD_DIM), lambda b: (0, 0)),
                  pl.BlockSpec((1, HEAD_DIM), lambda b: (0, 0))],
        out_specs=[pl.BlockSpec((1, rows, HEAD_DIM), lambda b: (b, 0, 0)),
                   pl.BlockSpec((1, N_KV * t, nbp), lambda b: (b, 0, 0)),
                   pl.BlockSpec((1, rows, HEAD_DIM), lambda b: (b, 0, 0)),
                   pl.BlockSpec((1, rows, HEAD_DIM), lambda b: (b, 0, 0))],
        out_shape=[jax.ShapeDtypeStruct((bd, rows, HEAD_DIM), BF16),
                   jax.ShapeDtypeStruct((bd, N_KV * t, nbp), F32),
                   jax.ShapeDtypeStruct((bd, rows, HEAD_DIM), F32),
                   jax.ShapeDtypeStruct((bd, rows, HEAD_DIM), F32)],
        compiler_params=_cparams(1),
        name="decode_front",
    )(u, kvc_all, state_win2, new_win, cos2, sin2, gq)


def _decode_sel_kernel(pt_ref, *refs, t, n_steps):
    del pt_ref
    np_ = PAGES_PER_STEP
    (qn_ref, selc_ref, seln_ref) = refs[:3]
    k_refs = refs[3:3 + np_]
    v_refs = refs[3 + np_:3 + 2 * np_]
    (new_ref, ocmp_ref, owin_ref, z0_ref, z1_ref, z2_ref, gate_ref, e_ref,
     o_ref, m_ref, l_ref, acc_ref) = refs[3 + 2 * np_:]
    c = pl.program_id(1)
    scale = HEAD_DIM ** -0.5
    gt = GRP * t

    @pl.when(c == 0)
    def _():
        m_ref[...] = jnp.full(m_ref.shape, NEG, F32)
        l_ref[...] = jnp.zeros(l_ref.shape, F32)
        acc_ref[...] = jnp.zeros(acc_ref.shape, F32)

    def online_update(s, mask, pv_of):
        sm = jnp.where(mask, s, NEG)
        m_old = m_ref[...]
        m_new = jnp.maximum(m_old, jnp.max(sm, axis=-1, keepdims=True))
        alpha = jnp.exp(m_old - m_new)
        p = jnp.where(mask, jnp.exp(sm - m_new), 0.0)
        l_ref[...] = alpha * l_ref[...] + jnp.sum(p, axis=-1, keepdims=True)
        acc_ref[...] = alpha * acc_ref[...] + pv_of(p.astype(BF16))
        m_ref[...] = m_new

    def cache_rows(page_refs):
        rows = [pr[0, :, 0].reshape(pr.shape[1] * N_KV, HEAD_DIM) for pr in page_refs]
        return jnp.concatenate(rows, axis=0).astype(BF16)

    s = _dot_nt(qn_ref[0], cache_rows(k_refs)) * scale
    vx = cache_rows(v_refs)
    mask = jnp.concatenate(
        [_dot(selc_ref[0, 0, g * t:(g + 1) * t, :].astype(BF16), e_ref[g])
         for g in range(N_KV) for _ in range(GRP)], axis=0) > 0.5
    online_update(s, mask, lambda pb: _dot(pb, vx))

    @pl.when(c == n_steps - 1)
    def _():
        gate = jax.nn.sigmoid(gate_ref[...])
        tq = _tile_rows(lax.broadcasted_iota(jnp.int32, (t, 1), 0), N_HEADS)
        causal = lax.broadcasted_iota(jnp.int32, (1, NEW_PAD), 1) <= tq
        sel_new = jnp.concatenate([seln_ref[0, 0, g * t:(g + 1) * t, 0:1]
                                   for g in range(N_KV) for _ in range(GRP)], axis=0) > 0.5
        s_new = jnp.concatenate(
            [_dot_nt(qn_ref[0, g * gt:(g + 1) * gt, :],
                     new_ref[0, :, g * HEAD_DIM:(g + 1) * HEAD_DIM].astype(BF16))
             for g in range(N_KV)], axis=0) * scale

        def pv_new(pb):
            return jnp.concatenate(
                [_dot(pb[g * gt:(g + 1) * gt],
                      new_ref[0, :, KVW + g * HEAD_DIM:KVW + (g + 1) * HEAD_DIM].astype(BF16))
                 for g in range(N_KV)], axis=0)

        online_update(s_new, sel_new & causal, pv_new)
        o_sel = acc_ref[...] * _safe_inv(l_ref[...])
        for g in range(N_KV):
            rows = slice(g * gt, (g + 1) * gt)
            _gate_and_store(o_ref, g, (ocmp_ref[0, rows, :], o_sel[rows], owin_ref[0, rows, :]),
                            (z0_ref, z1_ref, z2_ref), gate, t)


def _decode_sel(page_table, qn, selc, cache_kv, new_sel, ocmp, owin, u, gate, e4, bd, t):
    n_pages = page_table.shape[1]
    page = cache_kv.shape[1]
    n_steps = n_pages // PAGES_PER_STEP
    bps = PAGES_PER_STEP * (page // BLK)
    rows = N_HEADS * t

    def page_spec(part, k):
        return pl.BlockSpec((1, page, 1, N_KV, HEAD_DIM),
                            lambda b, c, pt: (pt[b, c * PAGES_PER_STEP + k], 0, part, 0, 0))

    grid_spec = pltpu.PrefetchScalarGridSpec(
        num_scalar_prefetch=1,
        grid=(bd, n_steps),
        in_specs=[pl.BlockSpec((1, rows, HEAD_DIM), lambda b, c, pt: (b, 0, 0)),
                  pl.BlockSpec((1, 1, N_KV * t, bps), lambda b, c, pt: (b, c, 0, 0)),
                  pl.BlockSpec((1, 1, N_KV * t, bps), lambda b, c, pt: (b, n_steps, 0, 0))]
        + [page_spec(2, k) for k in range(PAGES_PER_STEP)]
        + [page_spec(3, k) for k in range(PAGES_PER_STEP)]
        + [pl.BlockSpec((1, NEW_PAD, 2 * KVW), lambda b, c, pt: (b, 0, 0)),
           pl.BlockSpec((1, rows, HEAD_DIM), lambda b, c, pt: (b, 0, 0)),
           pl.BlockSpec((1, rows, HEAD_DIM), lambda b, c, pt: (b, 0, 0)),
           pl.BlockSpec((t, HD), lambda b, c, pt: (b, 1)),
           pl.BlockSpec((t, HD), lambda b, c, pt: (b, 2)),
           pl.BlockSpec((t, HD), lambda b, c, pt: (b, 3)),
           pl.BlockSpec((t, HEAD_DIM), lambda b, c, pt: (b, 0)),
           pl.BlockSpec((N_KV, bps, bps * BLK * N_KV), lambda b, c, pt: (0, 0, 0))],
        out_specs=pl.BlockSpec((t, HD), lambda b, c, pt: (b, 0)),
        scratch_shapes=[pltpu.VMEM((rows, 1), F32), pltpu.VMEM((rows, 1), F32),
                        pltpu.VMEM((rows, HEAD_DIM), F32)],
    )
    return pl.pallas_call(
        functools.partial(_decode_sel_kernel, t=t, n_steps=n_steps),
        grid_spec=grid_spec,
        out_shape=jax.ShapeDtypeStruct((bd * t, HD), BF16),
        compiler_params=_cparams(2),
        name="decode_sel",
    )(page_table, qn, selc, selc, *([cache_kv] * (2 * PAGES_PER_STEP)), new_sel, ocmp, owin,
      u, u, u, gate, e4)


def _rope_tables(pos):
    half = HEAD_DIM // 2
    inv = ROPE_THETA ** (-jnp.arange(half, dtype=F32) / half)
    ang = pos.astype(F32)[:, None] * inv[None, :]
    c = jnp.cos(ang)
    s = jnp.sin(ang)
    return jnp.concatenate([c, c], axis=-1), jnp.concatenate([-s, s], axis=-1)


def _block_group_expand(nblocks):
    col = jnp.arange(nblocks * BLK * N_KV, dtype=jnp.int32)
    in_block = (col // N_KV) // BLK == jnp.arange(nblocks, dtype=jnp.int32)[:, None]
    in_group = col % N_KV == jnp.arange(N_KV, dtype=jnp.int32)[:, None]
    return (in_block[None, :, :] & in_group[:, None, :]).astype(BF16)


def _conv_layers(x, prev, seq, g_a, w_in_a, conv_w, w_out_a):
    states = []
    ch = conv_w.shape[-1]
    for layer in range(N_A):
        xn = _rms_cast(x, g_a[layer])
        u = _matmul(xn, w_in_a, layer, 4 * ch)
        y, st = _conv_gate(u, prev[layer], conv_w[layer], seq)
        x = _matmul(y, w_out_a, layer, x.shape[1], res=x)
        states.append(st)
    return x, jnp.stack(states)


def _shared_kv(x, pos, seq, g_kv, w_kv, g_k, for_attn):
    cos2, sin2 = _rope_tables(pos)
    kv = _matmul(_rms_cast(x, g_kv), w_kv, 0, 6 * KVW)
    return _kv_post(kv, cos2, sin2, g_k, seq, for_attn), cos2, sin2


def _query_side(x, lb, g_b, w_qz_t, w_gate_t):
    xn = _rms_cast(x, g_b[lb])
    return _matmul_t(xn, w_qz_t, lb, 4 * HD), _matmul_t(xn, w_gate_t, lb, HEAD_DIM)


def kernel(x_prompt, x_sample, state_conv, cache_kv, state_win, page_table, g_a, w_in_a, conv_w,
           w_out_a, g_kv, w_kv, g_k, w_cmp, g_b, w_qz, g_q, w_out_b):
    batch, seq, d = x_prompt.shape
    bd, t, _ = x_sample.shape
    ch = conv_w.shape[-1]
    n_pool, page = cache_kv.shape[:2]
    n_pages = page_table.shape[1]
    past = n_pages * page
    wbuf = state_win.shape[1]
    assert seq % BLK == 0 and page % BLK == 0 and t <= BLK and n_pages % PAGES_PER_STEP == 0

    w_exp3 = jnp.broadcast_to(w_cmp.reshape(BLK, 2 * N_KV, 1), (BLK, 2 * N_KV, HEAD_DIM))
    w_exp = w_exp3.reshape(BLK, 2 * KVW)
    n_gate = N_BRANCH * N_HEADS
    w_qz_t = jnp.swapaxes(w_qz, 1, 2)
    w_gate_t = jnp.pad(w_qz_t[:, 4 * HD:4 * HD + n_gate, :], ((0, 0), (0, HEAD_DIM - n_gate), (0, 0)))

    xp = x_prompt.reshape(batch * seq, d)
    prev0 = jnp.zeros((batch, CONV_W - 1, ch), F32)
    xn = _rms_cast(xp, g_a[0])
    states = []
    for layer in range(N_A):
        y, st = _conv_matmul(xn, w_in_a, layer, prev0, conv_w[layer], seq)
        states.append(st)
        gains = [g_a[layer + 1]] if layer + 1 < N_A else [g_kv, g_b[0]]
        xp, *xns = _matmul_res_norm(y, w_out_a, layer, xp, gains)
        xn = xns[0]
    conv_p = jnp.stack(states)
    xn_kv, xn_q = xns
    pos_p = jnp.arange(seq, dtype=jnp.int32)
    cos_p, sin_p = _rope_tables(pos_p)
    rows_p, win_p, ks_p, vst_p, kw_p, vwt_p = _kv_post(
        _matmul(xn_kv, w_kv, 0, 6 * KVW), cos_p, sin_p, g_k, seq, True)
    kvc_p = _compress_rows(rows_p, w_exp3).reshape(batch * (seq // BLK), 2 * KVW)
    nb_p = seq // BLK
    vct_p = kvc_p.reshape(batch, nb_p, 2, KVW)[:, :, 1].transpose(0, 2, 1)
    for lb in range(N_B):
        u = _matmul_t(xn_q, w_qz_t, lb, 4 * HD)
        gate = _matmul_t(xn_q, w_gate_t, lb, HEAD_DIM)
        o = _prompt_attn(u, gate, kvc_p, vct_p, ks_p, vst_p, kw_p, vwt_p, cos_p, sin_p, g_q[lb:lb + 1],
                         batch, seq)
        if lb + 1 < N_B:
            xp, xn_q = _matmul_res_norm(o, w_out_b, lb, xp, [g_b[lb + 1]])
        else:
            xp = _matmul(o, w_out_b, lb, d, res=xp)
    wn = min(WINDOW, seq)
    y_prompt = xp.reshape(batch, seq, d)
    kv_rows_prompt = rows_p.reshape(batch, seq, 4, N_KV, HEAD_DIM)
    win_prompt = win_p.reshape(batch, seq, 2, N_KV, HEAD_DIM)[:, seq - wn:]

    xs = x_sample.reshape(bd * t, d)
    xs, conv_s = _conv_layers(xs, state_conv, t, g_a, w_in_a, conv_w, w_out_a)
    pos_s = past + jnp.arange(t, dtype=jnp.int32)
    (rows_s, win_s), cos_s, sin_s = _shared_kv(xs, pos_s, t, g_kv, w_kv, g_k, False)
    cache4 = cache_kv.reshape(n_pool, page, 4 * N_KV, HEAD_DIM)
    kvc_past = _compress_pages(cache4, page_table, w_exp3).reshape(bd, past // BLK, 2 * KVW)
    rows_s3 = rows_s.reshape(bd, t, 4 * KVW)
    rows_pad = jnp.pad(rows_s3, ((0, 0), (0, NEW_PAD - t), (0, 0)))
    kvc_new = _compress(rows_pad[:, :BLK, :2 * KVW].reshape(bd * BLK, 2 * KVW), w_exp)
    nb_past = past // BLK
    bps = PAGES_PER_STEP * (page // BLK)
    nbp = nb_past + bps
    kvc_all = jnp.concatenate(
        [kvc_past, kvc_new[:, None, :], jnp.zeros((bd, bps - 1, 2 * KVW), F32)], axis=1)
    new_sel = rows_pad[:, :, 2 * KVW:]
    new_win = jnp.pad(win_s.reshape(bd, t, 2 * KVW), ((0, 0), (0, NEW_PAD - t), (0, 0)))
    state_win2 = state_win.reshape(bd, wbuf, 2 * KVW)
    e4 = _block_group_expand(bps)
    for lb in range(N_B):
        u, gate = _query_side(xs, lb, g_b, w_qz_t, w_gate_t)
        qn, sel, ocmp, owin = _decode_front(u, kvc_all, state_win2, new_win, cos_s, sin_s,
                                            g_q[lb:lb + 1], bd, t, past)
        selc = sel.reshape(bd, N_KV * t, nbp // bps, bps).transpose(0, 2, 1, 3)
        o = _decode_sel(page_table, qn, selc, cache_kv, new_sel, ocmp, owin, u, gate, e4, bd, t)
        xs = _matmul(o, w_out_b, lb, d, res=xs)
    y_sample = xs.reshape(bd, t, d)
    kv_rows_sample = rows_s.reshape(bd, t, 4, N_KV, HEAD_DIM)
    kvw = jnp.concatenate([state_win, win_s.reshape(bd, t, 2, N_KV, HEAD_DIM)], axis=1)
    win_sample = kvw[:, kvw.shape[1] - min(WINDOW, kvw.shape[1]):]

    return (y_prompt, y_sample, conv_p, conv_s, kv_rows_prompt, kv_rows_sample, win_prompt, win_sample)
```

```python
import functools

import jax
import jax.numpy as jnp
from jax import lax
from jax.experimental import pallas as pl
from jax.experimental.pallas import tpu as pltpu

N_A = 2
N_B = 2
CONV_W = 3
N_HEADS = 16
N_KV = 4
GRP = N_HEADS // N_KV
HEAD_DIM = 128
N_BRANCH = 3
BLK = 64
N_SEL = 16
WINDOW = 512
ROPE_THETA = 10000.0
EPS = 1e-6
NEG = -1e30
FORCE = 1e3
LOG2E = 1.4426950408889634

HD = N_HEADS * HEAD_DIM
KVW = N_KV * HEAD_DIM
NEW_PAD = 128
KCH = 128
SPAN_KEYS = 512
PAGES_PER_STEP = 8
VMEM_LIMIT = 56 * 1024 * 1024

F32 = jnp.float32
BF16 = jnp.bfloat16


def _cparams(n_axes):
    return pltpu.CompilerParams(
        dimension_semantics=("arbitrary",) * n_axes, vmem_limit_bytes=VMEM_LIMIT)


def _dot_nt(a, b):
    return lax.dot_general(a, b, (((1,), (1,)), ((), ())), preferred_element_type=F32)


def _dot(a, b):
    return jnp.dot(a, b, preferred_element_type=F32)


def _pick(n, prefs):
    for p in prefs:
        if n % p == 0:
            return p
    return n


def _rms_cast_kernel(x_ref, g_ref, o_ref):
    x = x_ref[...]
    y = x * lax.rsqrt(jnp.mean(x * x, axis=-1, keepdims=True) + EPS)
    o_ref[...] = (y * g_ref[...]).astype(BF16)


def _rms_cast(x, g):
    n, d = x.shape
    tm = _pick(n, (512, 256, 128, 64, 8))
    return pl.pallas_call(
        _rms_cast_kernel,
        grid=(n // tm,),
        in_specs=[pl.BlockSpec((tm, d), lambda i: (i, 0)),
                  pl.BlockSpec((1, d), lambda i: (0, 0))],
        out_specs=pl.BlockSpec((tm, d), lambda i: (i, 0)),
        out_shape=jax.ShapeDtypeStruct((n, d), BF16),
        compiler_params=_cparams(1),
        name="rms_cast",
    )(x, g.reshape(1, d))


def _main_tile(nt):
    return lambda i: jnp.minimum(i, nt - 1)


def _mm_kernel(*refs, has_res, w_is_t, nt):
    it = iter(refs)
    x_ref, xs_ref, w_ref = next(it), next(it), next(it)
    r_ref, rs_ref = (next(it), next(it)) if has_res else (None, None)
    o_ref, os_ref, wb_ref = next(it), next(it), next(it)
    i = pl.program_id(1)

    @pl.when(i == 0)
    def _():
        w = w_ref[...]
        wb_ref[...] = (jnp.transpose(w) if w_is_t else w).astype(BF16)

    def project(xr, rr, orf):
        acc = _dot(xr[...], wb_ref[...])
        orf[...] = acc if rr is None else rr[...] + acc

    pl.when(i < nt)(functools.partial(project, x_ref, r_ref, o_ref))
    pl.when(i == nt)(functools.partial(project, xs_ref, rs_ref, os_ref))


def _matmul(x, xs, w, layer, m_out, res=None, res_s=None, w_is_t=False):
    n, k = x.shape
    ns = xs.shape[0]
    tm = _pick(n, (1024, 512, 256, 128, 64, 8))
    tn = _pick(m_out, (1024, 512, 256, 128))
    nt = n // tm
    main = _main_tile(nt)
    if w_is_t:
        w_spec = pl.BlockSpec((None, tn, k), lambda j, i: (layer, j, 0))
    else:
        w_spec = pl.BlockSpec((None, k, tn), lambda j, i: (layer, 0, j))
    in_specs = [pl.BlockSpec((tm, k), lambda j, i: (main(i), 0)),
                pl.BlockSpec((ns, k), lambda j, i: (0, 0)), w_spec]
    args = [x, xs, w]
    if res is not None:
        in_specs += [pl.BlockSpec((tm, tn), lambda j, i: (main(i), j)),
                     pl.BlockSpec((ns, tn), lambda j, i: (0, j))]
        args += [res, res_s]
    return pl.pallas_call(
        functools.partial(_mm_kernel, has_res=res is not None, w_is_t=w_is_t, nt=nt),
        grid=(m_out // tn, nt + 1),
        in_specs=in_specs,
        out_specs=[pl.BlockSpec((tm, tn), lambda j, i: (main(i), j)),
                   pl.BlockSpec((ns, tn), lambda j, i: (0, j))],
        out_shape=[jax.ShapeDtypeStruct((n, m_out), F32), jax.ShapeDtypeStruct((ns, m_out), F32)],
        scratch_shapes=[pltpu.VMEM((k, tn), BF16)],
        compiler_params=_cparams(2),
        name="matmul",
    )(*args)


def _mm_res_norm_kernel(*refs, n_norm, nt):
    x_ref, xs_ref, w_ref, r_ref, rs_ref = refs[:5]
    g_refs = refs[5:5 + n_norm]
    o_ref, os_ref = refs[5 + n_norm:7 + n_norm]
    xn_refs = refs[7 + n_norm:7 + 3 * n_norm]
    wb_ref = refs[7 + 3 * n_norm]
    i = pl.program_id(0)

    @pl.when(i == 0)
    def _():
        wb_ref[...] = w_ref[...].astype(BF16)

    def project(xr, rr, orf, xnrs):
        out = rr[...] + _dot(xr[...], wb_ref[...])
        orf[...] = out
        y = out * lax.rsqrt(jnp.mean(out * out, axis=-1, keepdims=True) + EPS)
        for g_ref, xn_ref in zip(g_refs, xnrs):
            xn_ref[...] = (y * g_ref[...]).astype(BF16)

    pl.when(i < nt)(functools.partial(project, x_ref, r_ref, o_ref, xn_refs[0::2]))
    pl.when(i == nt)(functools.partial(project, xs_ref, rs_ref, os_ref, xn_refs[1::2]))


def _matmul_res_norm(x, xs, w, layer, res, res_s, gains):
    n, k = x.shape
    ns = xs.shape[0]
    d = res.shape[1]
    tm = _pick(n, (512, 256, 128, 64, 8))
    nt = n // tm
    main = _main_tile(nt)
    row = lambda i: (main(i), 0)
    side = lambda i: (0, 0)
    n_norm = len(gains)
    outs = pl.pallas_call(
        functools.partial(_mm_res_norm_kernel, n_norm=n_norm, nt=nt),
        grid=(nt + 1,),
        in_specs=[pl.BlockSpec((tm, k), row), pl.BlockSpec((ns, k), side),
                  pl.BlockSpec((None, k, d), lambda i: (layer, 0, 0), pipeline_mode=pl.Buffered(1)),
                  pl.BlockSpec((tm, d), row), pl.BlockSpec((ns, d), side)]
        + [pl.BlockSpec((1, d), side)] * n_norm,
        out_specs=[pl.BlockSpec((tm, d), row), pl.BlockSpec((ns, d), side)]
        + [pl.BlockSpec((tm, d), row), pl.BlockSpec((ns, d), side)] * n_norm,
        out_shape=[jax.ShapeDtypeStruct((n, d), F32), jax.ShapeDtypeStruct((ns, d), F32)]
        + [jax.ShapeDtypeStruct((n, d), BF16), jax.ShapeDtypeStruct((ns, d), BF16)] * n_norm,
        scratch_shapes=[pltpu.VMEM((k, d), BF16)],
        compiler_params=_cparams(1),
        name="matmul_res_norm",
    )(x, xs, w, res, res_s, *[g.reshape(1, d) for g in gains])
    return outs[0], outs[1], [(outs[2 + 2 * j], outs[3 + 2 * j]) for j in range(n_norm)]


def _conv_taps(c, c1, c2, w_ref):
    return w_ref[0:1, :] * c2 + w_ref[1:2, :] * c1 + w_ref[2:3, :] * c


def _conv_mm_kernel(x_ref, xs_ref, wb_ref, wc_ref, wh_ref, wz_ref, prev_ref, p1_ref, p2_ref, cw_ref,
                    y_ref, st_ref, ys_ref, cs_ref, wbf_ref, carry_ref, *, tps, nt, side_seq):
    i = pl.program_id(1)

    @pl.when(i == 0)
    def _():
        for k, wr in enumerate((wb_ref, wc_ref, wh_ref, wz_ref)):
            wbf_ref[k] = wr[...].astype(BF16)

    def parts(xr):
        x = xr[...]
        return (_dot(x, wbf_ref[0]), _dot(x, wbf_ref[1]) * _dot(x, wbf_ref[2]), _dot(x, wbf_ref[3]))

    @pl.when(i < nt)
    def _():
        b, c, z = parts(x_ref)
        tm = c.shape[0]

        @pl.when(i % tps == 0)
        def _():
            carry_ref[0:2, :] = prev_ref[0]

        p0 = carry_ref[0:1, :]
        p1 = carry_ref[1:2, :]
        row = lax.broadcasted_iota(jnp.int32, c.shape, 0)
        c1 = jnp.where(row == 0, p1, pltpu.roll(c, 1, 0))
        c2 = jnp.where(row == 0, p0, jnp.where(row == 1, p1, pltpu.roll(c, 2, 0)))
        y_ref[...] = (jax.nn.silu(z) * (b * _conv_taps(c, c1, c2, cw_ref))).astype(BF16)
        last = c[tm - 2:tm, :]
        carry_ref[0:2, :] = last
        st_ref[0] = last

    @pl.when(i == nt)
    def _():
        b, c, z = parts(xs_ref)
        pos = lax.broadcasted_iota(jnp.int32, c.shape, 0) & (side_seq - 1)
        c1 = jnp.where(pos == 0, p1_ref[...], pltpu.roll(c, 1, 0))
        c2 = jnp.where(pos < 2, p2_ref[...], pltpu.roll(c, 2, 0))
        ys_ref[...] = (jax.nn.silu(z) * (b * _conv_taps(c, c1, c2, cw_ref))).astype(BF16)
        cs_ref[...] = c


def _conv_matmul(xn, xs, w_in, layer, prev, prev_s, w_conv, seq, side_seq):
    n, k = xn.shape
    ns = xs.shape[0]
    ch = w_in.shape[2] // 4
    tm = _pick(seq, (1024, 512, 256, 128, 64, 8))
    tc = _pick(ch, (256, 128))
    tps = seq // tm
    nc = ch // tc
    nt = n // tm
    main = _main_tile(nt)
    assert side_seq >= CONV_W - 1 and side_seq & (side_seq - 1) == 0
    p2 = jnp.pad(prev_s, ((0, 0), (0, side_seq - 2), (0, 0))).reshape(ns, ch)
    p1 = jnp.pad(prev_s[:, 1:2], ((0, 0), (0, side_seq - 1), (0, 0))).reshape(ns, ch)

    def w_spec(part):
        return pl.BlockSpec((None, k, tc), lambda j, i: (layer, 0, part * nc + j))

    side = pl.BlockSpec((ns, tc), lambda j, i: (0, j))
    return pl.pallas_call(
        functools.partial(_conv_mm_kernel, tps=tps, nt=nt, side_seq=side_seq),
        grid=(nc, nt + 1),
        in_specs=[pl.BlockSpec((tm, k), lambda j, i: (main(i), 0)),
                  pl.BlockSpec((ns, k), lambda j, i: (0, 0))] + [w_spec(p) for p in range(4)]
        + [pl.BlockSpec((1, 2, tc), lambda j, i: (main(i) // tps, 0, j)), side, side,
           pl.BlockSpec((CONV_W, tc), lambda j, i: (0, j))],
        out_specs=[pl.BlockSpec((tm, tc), lambda j, i: (main(i), j)),
                   pl.BlockSpec((1, 2, tc), lambda j, i: (main(i) // tps, 0, j)), side, side],
        out_shape=[jax.ShapeDtypeStruct((n, ch), BF16),
                   jax.ShapeDtypeStruct((n // seq, CONV_W - 1, ch), F32),
                   jax.ShapeDtypeStruct((ns, ch), BF16),
                   jax.ShapeDtypeStruct((ns, ch), F32)],
        scratch_shapes=[pltpu.VMEM((4, k, tc), BF16), pltpu.VMEM((8, tc), F32)],
        compiler_params=_cparams(2),
        name="conv_matmul",
    )(xn, xs, w_in, w_in, w_in, w_in, prev, p1, p2, w_conv)


def _norm_rope(x, g, cos2, sin2):
    y = x * lax.rsqrt(jnp.mean(x * x, axis=-1, keepdims=True) + EPS) * g
    return y * cos2 + pltpu.roll(y, HEAD_DIM // 2, 1) * sin2


def _kv_post_kernel(kv_ref, cos_ref, sin_ref, gk_ref, rows_ref, win_ref, *attn_refs):
    cos2 = cos_ref[...]
    sin2 = sin_ref[...]
    tm = kv_ref.shape[0]
    for br in range(N_BRANCH):
        base = br * 2 * KVW
        out_ref, row0 = (rows_ref, br * 2 * N_KV) if br < 2 else (win_ref, 0)
        for g in range(N_KV):
            c0 = base + g * HEAD_DIM
            k = _norm_rope(kv_ref[:, c0:c0 + HEAD_DIM], gk_ref[br:br + 1, :], cos2, sin2)
            out_ref[:, row0 + g, :] = k
            out_ref[:, row0 + N_KV + g, :] = kv_ref[:, c0 + KVW:c0 + KVW + HEAD_DIM]
            if attn_refs and br >= 1:
                attn_refs[2 * (br - 1)][:, g * HEAD_DIM:(g + 1) * HEAD_DIM] = k.astype(BF16)
        if attn_refs and br >= 1:
            v = kv_ref[:, base + KVW:base + 2 * KVW]
            vt_ref = attn_refs[2 * (br - 1) + 1]
            for a in range(tm // KCH):
                vt_ref[a] = jnp.transpose(v[a * KCH:(a + 1) * KCH, :]).astype(BF16)


def _kv_post(kv, cos2, sin2, g_k, seq, for_attn):
    n = kv.shape[0]
    tm = _pick(seq, (256, 128, 64, 8))
    tps = seq // tm
    out_specs = [pl.BlockSpec((tm, 4 * N_KV, HEAD_DIM), lambda i: (i, 0, 0)),
                 pl.BlockSpec((tm, 2 * N_KV, HEAD_DIM), lambda i: (i, 0, 0))]
    out_shape = [jax.ShapeDtypeStruct((n, 4 * N_KV, HEAD_DIM), F32),
                 jax.ShapeDtypeStruct((n, 2 * N_KV, HEAD_DIM), F32)]
    if for_attn:
        assert tm % KCH == 0
        for _ in range(2):
            out_specs += [pl.BlockSpec((tm, KVW), lambda i: (i, 0)),
                          pl.BlockSpec((tm // KCH, KVW, KCH), lambda i: (i, 0, 0))]
            out_shape += [jax.ShapeDtypeStruct((n, KVW), BF16),
                          jax.ShapeDtypeStruct((n // KCH, KVW, KCH), BF16)]
    return pl.pallas_call(
        _kv_post_kernel,
        grid=(n // tm,),
        in_specs=[pl.BlockSpec((tm, 6 * KVW), lambda i: (i, 0)),
                  pl.BlockSpec((tm, HEAD_DIM), lambda i: (i % tps, 0)),
                  pl.BlockSpec((tm, HEAD_DIM), lambda i: (i % tps, 0)),
                  pl.BlockSpec((N_BRANCH, HEAD_DIM), lambda i: (0, 0))],
        out_specs=out_specs,
        out_shape=out_shape,
        compiler_params=_cparams(1),
        name="kv_post",
    )(kv, cos2, sin2, g_k)


def _compress_kernel(rows_ref, w_ref, o_ref):
    x = rows_ref[...]
    nb = x.shape[0] // BLK
    x = x.reshape(nb, BLK, x.shape[1]) * w_ref[...][None]
    o_ref[...] = jnp.sum(x, axis=1)


def _compress(rows, w_exp):
    n = rows.shape[0]
    nblk = n // BLK
    per = _pick(nblk, (8,))
    return pl.pallas_call(
        _compress_kernel,
        grid=(nblk // per,),
        in_specs=[pl.BlockSpec((per * BLK, 2 * KVW), lambda i: (i, 0)),
                  pl.BlockSpec((BLK, 2 * KVW), lambda i: (0, 0))],
        out_specs=pl.BlockSpec((per, 2 * KVW), lambda i: (i, 0)),
        out_shape=jax.ShapeDtypeStruct((nblk, 2 * KVW), F32),
        compiler_params=_cparams(1),
        name="compress",
    )(rows, w_exp)


def _block_sums(x, w):
    nb = x.shape[0] // BLK
    return jnp.sum(x.reshape(nb, BLK, 2 * N_KV, HEAD_DIM) * w[None], axis=1)


def _compress_rows_kernel(rows_ref, w_ref, o_ref):
    o_ref[...] = _block_sums(rows_ref[...], w_ref[...])


def _compress_rows(rows3, w_exp3):
    n = rows3.shape[0]
    nblk = n // BLK
    per = _pick(nblk, (8,))
    return pl.pallas_call(
        _compress_rows_kernel,
        grid=(nblk // per,),
        in_specs=[pl.BlockSpec((per * BLK, 2 * N_KV, HEAD_DIM), lambda i: (i, 0, 0)),
                  pl.BlockSpec((BLK, 2 * N_KV, HEAD_DIM), lambda i: (0, 0, 0))],
        out_specs=pl.BlockSpec((per, 2 * N_KV, HEAD_DIM), lambda i: (i, 0, 0)),
        out_shape=jax.ShapeDtypeStruct((nblk, 2 * N_KV, HEAD_DIM), F32),
        compiler_params=_cparams(1),
        name="compress_rows",
    )(rows3, w_exp3)


def _compress_pages_kernel(pt_ref, *refs):
    page_refs = refs[:PAGES_PER_STEP]
    w_ref = refs[PAGES_PER_STEP]
    o_ref = refs[PAGES_PER_STEP + 1]
    w = w_ref[...]
    for k, pr in enumerate(page_refs):
        nb = pr.shape[1] // BLK
        o_ref[0, k * nb:(k + 1) * nb] = _block_sums(pr[0], w)


def _compress_pages(cache4, page_table, w_exp3):
    bd, n_pages = page_table.shape
    page = cache4.shape[1]
    bpp = page // BLK
    steps = n_pages // PAGES_PER_STEP
    rows_out = PAGES_PER_STEP * bpp

    def page_spec(k):
        return pl.BlockSpec((1, page, 2 * N_KV, HEAD_DIM),
                            lambda b, c, pt: (pt[b, c * PAGES_PER_STEP + k], 0, 0, 0))

    grid_spec = pltpu.PrefetchScalarGridSpec(
        num_scalar_prefetch=1,
        grid=(bd, steps),
        in_specs=[page_spec(k) for k in range(PAGES_PER_STEP)]
        + [pl.BlockSpec((BLK, 2 * N_KV, HEAD_DIM), lambda b, c, pt: (0, 0, 0))],
        out_specs=pl.BlockSpec((1, rows_out, 2 * N_KV, HEAD_DIM), lambda b, c, pt: (b, c, 0, 0)),
    )
    return pl.pallas_call(
        _compress_pages_kernel,
        grid_spec=grid_spec,
        out_shape=jax.ShapeDtypeStruct((bd, n_pages * bpp, 2 * N_KV, HEAD_DIM), F32),
        compiler_params=_cparams(2),
        name="compress_pages",
    )(page_table, *([cache4] * PAGES_PER_STEP), w_exp3)


def _softmax_parts(s, mask):
    sm = jnp.where(mask, s, NEG)
    m = jnp.max(sm, axis=-1, keepdims=True)
    p = jnp.where(mask, jnp.exp(sm - m), 0.0)
    return p, m, jnp.sum(p, axis=-1, keepdims=True)


def _safe_inv(l):
    return jnp.where(l > 0.0, 1.0 / l, 0.0)


def _topk_mask(score, n_f, k):
    sel = jnp.zeros(score.shape, F32)
    big = jnp.float32(score.shape[-1] + 1)
    for _ in range(k):
        mx = jnp.max(score, axis=-1, keepdims=True)
        idx = jnp.min(jnp.where(score == mx, n_f, big), axis=-1, keepdims=True)
        hit = n_f == idx
        sel = jnp.where(hit, 1.0, sel)
        score = jnp.where(hit, -jnp.inf, score)
    return sel


def _select_blocks(psum, cur, n_i):
    forced = ((n_i == 0) | (n_i == cur) | (n_i == cur - 1)).astype(F32)
    valid = n_i <= cur
    score = jnp.where(valid, psum + FORCE * forced, NEG)
    sel = _topk_mask(score, n_i.astype(F32), min(N_SEL, psum.shape[-1]))
    return jnp.where(valid, sel, 0.0)


def _q_heads(uq_ref, g, gq, cos2, sin2, scale=None):
    hs = []
    for r in range(GRP):
        c0 = (g * GRP + r) * HEAD_DIM
        h = _norm_rope(uq_ref[:, c0:c0 + HEAD_DIM], gq, cos2, sin2)
        hs.append(h if scale is None else h * scale)
    return jnp.concatenate(hs, axis=0).astype(BF16)


def _tile_rows(x, reps):
    return jnp.concatenate([x] * reps, axis=0)


def _gate_and_store(o_ref, g, outs, z_refs, gate, t):
    for r in range(GRP):
        h = g * GRP + r
        acc = None
        for br in range(N_BRANCH):
            zg = jax.nn.silu(z_refs[br][:, h * HEAD_DIM:(h + 1) * HEAD_DIM]) \
                * gate[:, br * N_HEADS + h:br * N_HEADS + h + 1]
            term = outs[br][r * t:(r + 1) * t, :] * zg
            acc = term if acc is None else acc + term
        o_ref[:, h * HEAD_DIM:(h + 1) * HEAD_DIM] = acc.astype(o_ref.dtype)


def _rank_select_t(score_t, n_col, k):
    rank = jnp.zeros(score_t.shape, F32)
    for m in range(score_t.shape[0]):
        row = score_t[m:m + 1, :]
        tie = (n_col > m).astype(F32)
        rank = rank + jnp.where(row > score_t, 1.0, 0.0) + jnp.where(row == score_t, tie, 0.0)
    return jnp.where(rank < k, 1.0, 0.0)


def _prompt_attn_kernel(uq_ref, z0_ref, z1_ref, z2_ref, gate_ref, kvc_ref, vct_ref, ks_ref, vst_ref,
                        kw_ref, vwt_ref, cos_ref, sin_ref, gq_ref, o_ref,
                        q_scr, s_scr, bias_scr, mrun_scr, lrun_scr, ot_scr, ocmp_scr, *, seq, spk):
    i = pl.program_id(1)
    t = BLK
    lanes = GRP * t
    nb = seq // BLK
    nch = seq // KCH
    nwc = min(WINDOW // KCH + 1, nch)
    bpc = KCH // BLK
    k_in_blk = lax.broadcasted_iota(jnp.int32, (BLK, 1), 0)
    cos2 = cos_ref[...]
    sin2 = sin_ref[...]
    gq = gq_ref[...]
    gate = jax.nn.sigmoid(gate_ref[...])
    qpos_l = i * t + (lax.broadcasted_iota(jnp.int32, (1, lanes), 1) & (t - 1))
    qpos_h = qpos_l[:, 0:2 * t]
    cur_h = qpos_h // BLK
    n_col = lax.broadcasted_iota(jnp.int32, (nb, 1), 0)
    cmask_t = (n_col * BLK + (BLK - 1)) <= qpos_l
    forced_t = ((n_col == 0) | (n_col == cur_h) | (n_col == cur_h - 1)).astype(F32)
    valid_t = n_col <= cur_h
    k_in_chunk = lax.broadcasted_iota(jnp.int32, (KCH, 1), 0)

    def group_max(x):
        return jnp.max(x.reshape(KCH // 8, 8, lanes), axis=0)

    def group_sum(x):
        return jnp.sum(x.reshape(KCH // 8, 8, lanes), axis=0)

    def finish(lrun, ot):
        l = jnp.sum(lrun, axis=0, keepdims=True)
        return jnp.transpose(ot * _safe_inv(l))

    for g in range(N_KV):
        q = _q_heads(uq_ref, g, gq, cos2, sin2, LOG2E * HEAD_DIM ** -0.5)
        ksl = slice(g * HEAD_DIM, (g + 1) * HEAD_DIM)
        kc = kvc_ref[0, :, ksl].astype(BF16)

        st = jnp.where(cmask_t, _dot_nt(kc, q), NEG)
        pt = jnp.where(cmask_t, jnp.exp2(st - jnp.max(st, axis=0, keepdims=True)), 0.0)
        pt = pt * _safe_inv(jnp.sum(pt, axis=0, keepdims=True))
        ocmp_scr[g] = _dot(vct_ref[0, ksl, :].astype(BF16), pt.astype(BF16))

        half = pt[:, 0:2 * t] + pt[:, 2 * t:4 * t]
        psum_t = half + pltpu.roll(half, t, 1)
        score_t = jnp.where(valid_t, psum_t + FORCE * forced_t, NEG)
        sel_t = jnp.where(valid_t, _rank_select_t(score_t, n_col, min(N_SEL, nb)), 0.0)
        bias_t = (sel_t - 1.0) * (-NEG)
        bias_scr[g] = jnp.concatenate([bias_t, bias_t], axis=1)

        q_scr[g] = q
        mrun_scr[g] = jnp.full((8, lanes), NEG, F32)
        lrun_scr[g] = jnp.zeros((8, lanes), F32)
        ot_scr[g] = jnp.zeros((HEAD_DIM, lanes), F32)

    def sel_pass1(sc, diagonal):
        for g in range(N_KV):
            ksl = slice(g * HEAD_DIM, (g + 1) * HEAD_DIM)
            st = _dot_nt(ks_ref[0, sc * spk:(sc + 1) * spk, ksl], q_scr[g])
            mx = mrun_scr[g]
            for j in range(spk // BLK):
                n = sc * (spk // BLK) + j
                blk = st[j * BLK:(j + 1) * BLK] + bias_scr[g, n:n + 1, :]
                if diagonal:
                    blk = jnp.where(n * BLK + k_in_blk <= qpos_l, blk, NEG)
                s_scr[g, n // bpc, (n % bpc) * BLK:(n % bpc + 1) * BLK, :] = blk
                mx = jnp.maximum(mx, jnp.max(blk.reshape(BLK // 8, 8, lanes), axis=0))
            mrun_scr[g] = mx

    def sel_pass2(sc):
        for g in range(N_KV):
            ksl = slice(g * HEAD_DIM, (g + 1) * HEAD_DIM)
            m = jnp.max(mrun_scr[g], axis=0, keepdims=True)
            lrun = lrun_scr[g]
            ot = ot_scr[g]
            for c in range(sc * (spk // KCH), (sc + 1) * (spk // KCH)):
                p = jnp.exp2(s_scr[g, c] - m)
                lrun = lrun + group_sum(p)
                ot = ot + _dot(vst_ref[c, ksl, :], p.astype(BF16))
            lrun_scr[g] = lrun
            ot_scr[g] = ot

    for sc in range(seq // spk):
        pl.when((sc + 1) * spk <= i * t)(functools.partial(sel_pass1, sc, False))
        pl.when((sc * spk <= i * t) & (i * t < (sc + 1) * spk))(functools.partial(sel_pass1, sc, True))
    for sc in range(seq // spk):
        pl.when(sc * spk <= i * t)(functools.partial(sel_pass2, sc))

    cs = jnp.minimum(jnp.maximum(i - WINDOW // BLK, 0) // bpc, nch - nwc)
    for g in range(N_KV):
        ksl = slice(g * HEAD_DIM, (g + 1) * HEAD_DIM)
        q = q_scr[g]
        o_sel = finish(lrun_scr[g], ot_scr[g])
        mrun = jnp.full((8, lanes), NEG, F32)
        for j in range(nwc):
            c = cs + j
            k = kw_ref[0, pl.ds(pl.multiple_of(c * KCH, KCH), KCH), ksl]
            dpos = qpos_l - (c * KCH + k_in_chunk)
            st = jnp.where((dpos >= 0) & (dpos < WINDOW), _dot_nt(k, q), NEG)
            s_scr[g, j] = st
            mrun = jnp.maximum(mrun, group_max(st))
        m = jnp.max(mrun, axis=0, keepdims=True)
        lrun = jnp.zeros((8, lanes), F32)
        ot = jnp.zeros((HEAD_DIM, lanes), F32)
        for j in range(nwc):
            p = jnp.exp2(s_scr[g, j] - m)
            lrun = lrun + group_sum(p)
            ot = ot + _dot(vwt_ref[cs + j, ksl, :], p.astype(BF16))
        o_win = finish(lrun, ot)

        _gate_and_store(o_ref, g, (jnp.transpose(ocmp_scr[g]), o_sel, o_win),
                        (z0_ref, z1_ref, z2_ref), gate, t)


def _prompt_attn(u, gate, kvc, vct, ks, vst, kw, vwt, cos2, sin2, gq, batch, seq):
    n = u.shape[0]
    t = BLK
    spk = _pick(seq, (SPAN_KEYS, KCH))
    assert GRP * t == 2 * KCH and seq % KCH == 0 and spk % KCH == 0
    nqb = seq // t
    nb = seq // BLK
    nch = seq // KCH
    lanes = GRP * t
    row = lambda b, i: (b * nqb + i, 0)
    keys = pl.BlockSpec((1, seq, KVW), lambda b, i: (b, 0, 0))
    vals_t = pl.BlockSpec((nch, KVW, KCH), lambda b, i: (b, 0, 0))
    return pl.pallas_call(
        functools.partial(_prompt_attn_kernel, seq=seq, spk=spk),
        grid=(batch, nqb),
        in_specs=[pl.BlockSpec((t, HD), row),
                  pl.BlockSpec((t, HD), lambda b, i: (b * nqb + i, 1)),
                  pl.BlockSpec((t, HD), lambda b, i: (b * nqb + i, 2)),
                  pl.BlockSpec((t, HD), lambda b, i: (b * nqb + i, 3)),
                  pl.BlockSpec((t, HEAD_DIM), row),
                  pl.BlockSpec((1, nb, 2 * KVW), lambda b, i: (b, 0, 0)),
                  pl.BlockSpec((1, KVW, nb), lambda b, i: (b, 0, 0)),
                  keys, vals_t, keys, vals_t,
                  pl.BlockSpec((t, HEAD_DIM), lambda b, i: (i, 0)),
                  pl.BlockSpec((t, HEAD_DIM), lambda b, i: (i, 0)),
                  pl.BlockSpec((1, HEAD_DIM), lambda b, i: (0, 0))],
        out_specs=pl.BlockSpec((t, HD), row),
        out_shape=jax.ShapeDtypeStruct((n, HD), BF16),
        scratch_shapes=[pltpu.VMEM((N_KV, lanes, HEAD_DIM), BF16),
                        pltpu.VMEM((N_KV, nch, KCH, lanes), F32),
                        pltpu.VMEM((N_KV, nb, lanes), F32),
                        pltpu.VMEM((N_KV, 8, lanes), F32),
                        pltpu.VMEM((N_KV, 8, lanes), F32),
                        pltpu.VMEM((N_KV, HEAD_DIM, lanes), F32),
                        pltpu.VMEM((N_KV, HEAD_DIM, lanes), F32)],
        compiler_params=_cparams(2),
        name="prompt_attn",
    )(u, u, u, u, gate, kvc.reshape(batch, nb, 2 * KVW), vct, ks.reshape(batch, seq, KVW), vst,
      kw.reshape(batch, seq, KVW), vwt, cos2, sin2, gq)


def _decode_front_kernel(uq_ref, kvc_ref, swin_ref, nwin_ref, cos_ref, sin_ref, gq_ref,
                         qn_ref, sel_ref, ocmp_ref, owin_ref, *, past, t):
    scale = HEAD_DIM ** -0.5
    cos2 = cos_ref[...]
    sin2 = sin_ref[...]
    gq = gq_ref[...]
    nbp = kvc_ref.shape[1]
    wbuf = swin_ref.shape[1]
    qpos = past + lax.broadcasted_iota(jnp.int32, (t, 1), 0)
    qpos_r = _tile_rows(qpos, GRP)
    cur = qpos // BLK
    n_i = lax.broadcasted_iota(jnp.int32, (1, nbp), 1)
    cmask = (n_i * BLK + (BLK - 1)) <= qpos_r
    kwpos_old = (past - wbuf) + lax.broadcasted_iota(jnp.int32, (1, wbuf), 1)
    kwpos_new = past + lax.broadcasted_iota(jnp.int32, (1, NEW_PAD), 1)
    d_old = qpos_r - kwpos_old
    d_new = qpos_r - kwpos_new
    m_old = (d_old >= 0) & (d_old < WINDOW) & (kwpos_old >= 0)
    m_new = (d_new >= 0) & (d_new < WINDOW) & (kwpos_new < past + t)

    for g in range(N_KV):
        q = _q_heads(uq_ref, g, gq, cos2, sin2)
        rows = slice(g * GRP * t, (g + 1) * GRP * t)
        ksl = slice(g * HEAD_DIM, (g + 1) * HEAD_DIM)
        vsl = slice(KVW + g * HEAD_DIM, KVW + (g + 1) * HEAD_DIM)
        qn_ref[0, rows, :] = q

        s = _dot_nt(q, kvc_ref[0, :, ksl].astype(BF16)) * scale
        p, _, l = _softmax_parts(s, cmask)
        p = p * _safe_inv(l)
        ocmp_ref[0, rows, :] = _dot(p.astype(BF16), kvc_ref[0, :, vsl].astype(BF16))

        psum = p[0:t]
        for r in range(1, GRP):
            psum = psum + p[r * t:(r + 1) * t]
        sel_ref[0, g * t:(g + 1) * t, :] = _select_blocks(psum, cur, n_i)

        s_old = _dot_nt(q, swin_ref[0, :, ksl].astype(BF16)) * scale
        s_new = _dot_nt(q, nwin_ref[0, :, ksl].astype(BF16)) * scale
        mx = jnp.maximum(jnp.max(jnp.where(m_old, s_old, NEG), axis=-1, keepdims=True),
                         jnp.max(jnp.where(m_new, s_new, NEG), axis=-1, keepdims=True))
        p_old = jnp.where(m_old, jnp.exp(jnp.where(m_old, s_old, NEG) - mx), 0.0)
        p_new = jnp.where(m_new, jnp.exp(jnp.where(m_new, s_new, NEG) - mx), 0.0)
        l = jnp.sum(p_old, axis=-1, keepdims=True) + jnp.sum(p_new, axis=-1, keepdims=True)
        o = _dot(p_old.astype(BF16), swin_ref[0, :, vsl].astype(BF16)) \
            + _dot(p_new.astype(BF16), nwin_ref[0, :, vsl].astype(BF16))
        owin_ref[0, rows, :] = o * _safe_inv(l)


def _decode_front(u, kvc_all, state_win2, new_win, cos2, sin2, gq, bd, t, past):
    nbp = kvc_all.shape[1]
    wbuf = state_win2.shape[1]
    rows = N_HEADS * t
    return pl.pallas_call(
        functools.partial(_decode_front_kernel, past=past, t=t),
        grid=(bd,),
        in_specs=[pl.BlockSpec((t, HD), lambda b: (b, 0)),
                  pl.BlockSpec((1, nbp, 2 * KVW), lambda b: (b, 0, 0)),
                  pl.BlockSpec((1, wbuf, 2 * KVW), lambda b: (b, 0, 0)),
                  pl.BlockSpec((1, NEW_PAD, 2 * KVW), lambda b: (b, 0, 0)),
                  pl.BlockSpec((t, HEAD_DIM), lambda b: (0, 0)),
                  pl.BlockSpec((t, HEAD_DIM), lambda b: (0, 0)),
                  pl.BlockSpec((1, HEAD_DIM), lambda b: (0, 0))],
        out_specs=[pl.BlockSpec((1, rows, HEAD_DIM), lambda b: (b, 0, 0)),
                   pl.BlockSpec((1, N_KV * t, nbp), lambda b: (b, 0, 0)),
                   pl.BlockSpec((1, rows, HEAD_DIM), lambda b: (b, 0, 0)),
                   pl.BlockSpec((1, rows, HEAD_DIM), lambda b: (b, 0, 0))],
        out_shape=[jax.ShapeDtypeStruct((bd, rows, HEAD_DIM), BF16),
                   jax.ShapeDtypeStruct((bd, N_KV * t, nbp), F32),
                   jax.ShapeDtypeStruct((bd, rows, HEAD_DIM), F32),
                   jax.ShapeDtypeStruct((bd, rows, HEAD_DIM), F32)],
        compiler_params=_cparams(1),
        name="decode_front",
    )(u, kvc_all, state_win2, new_win, cos2, sin2, gq)


def _values_product(vx, pb):
    return _dot(pb, vx)


def _decode_sel_kernel(pt_ref, *refs, t, n_steps):
    del pt_ref
    np_ = PAGES_PER_STEP
    (qn_ref, selc_ref, seln_ref) = refs[:3]
    k_refs = refs[3:3 + np_]
    v_refs = refs[3 + np_:3 + 2 * np_]
    (new_ref, ocmp_ref, owin_ref, z0_ref, z1_ref, z2_ref, gate_ref, e_ref,
     o_ref, m_ref, l_ref, acc_ref) = refs[3 + 2 * np_:]
    c = pl.program_id(1)
    scale = HEAD_DIM ** -0.5
    gt = GRP * t

    @pl.when(c == 0)
    def _():
        m_ref[...] = jnp.full(m_ref.shape, NEG, F32)
        l_ref[...] = jnp.zeros(l_ref.shape, F32)
        acc_ref[...] = jnp.zeros(acc_ref.shape, F32)

    def softmax_part(s, mask, pv_of):
        sm = jnp.where(mask, s, NEG)
        m = jnp.max(sm, axis=-1, keepdims=True)
        p = jnp.where(mask, jnp.exp(sm - m), 0.0)
        return m, jnp.sum(p, axis=-1, keepdims=True), pv_of(p.astype(BF16))

    def merge(parts):
        m_old = m_ref[...]
        m_new = m_old
        for m, _, _ in parts:
            m_new = jnp.maximum(m_new, m)
        alpha = jnp.exp(m_old - m_new)
        l = alpha * l_ref[...]
        acc = alpha * acc_ref[...]
        for m, lp, pv in parts:
            w = jnp.exp(m - m_new)
            l = l + w * lp
            acc = acc + w * pv
        l_ref[...] = l
        acc_ref[...] = acc
        m_ref[...] = m_new

    def cache_rows(page_refs):
        rows = [pr[0, :, 0].reshape(pr.shape[1] * N_KV, HEAD_DIM) for pr in page_refs]
        return jnp.concatenate(rows, axis=0).astype(BF16)

    halves = 2 if np_ % 2 == 0 else 1
    pp = np_ // halves
    bph = e_ref.shape[1]
    parts = []
    for h in range(halves):
        vx = cache_rows(v_refs[h * pp:(h + 1) * pp])
        s = _dot_nt(qn_ref[0], cache_rows(k_refs[h * pp:(h + 1) * pp])) * scale
        mask = jnp.concatenate(
            [_dot(selc_ref[0, 0, g * t:(g + 1) * t, h * bph:(h + 1) * bph].astype(BF16), e_ref[g])
             for g in range(N_KV) for _ in range(GRP)], axis=0) > 0.5
        parts.append(softmax_part(s, mask, functools.partial(_values_product, vx)))
    merge(parts)

    @pl.when(c == n_steps - 1)
    def _():
        gate = jax.nn.sigmoid(gate_ref[...])
        tq = _tile_rows(lax.broadcasted_iota(jnp.int32, (t, 1), 0), N_HEADS)
        causal = lax.broadcasted_iota(jnp.int32, (1, NEW_PAD), 1) <= tq
        sel_new = jnp.concatenate([seln_ref[0, 0, g * t:(g + 1) * t, 0:1]
                                   for g in range(N_KV) for _ in range(GRP)], axis=0) > 0.5
        s_new = jnp.concatenate(
            [_dot_nt(qn_ref[0, g * gt:(g + 1) * gt, :],
                     new_ref[0, :, g * HEAD_DIM:(g + 1) * HEAD_DIM].astype(BF16))
             for g in range(N_KV)], axis=0) * scale

        def pv_new(pb):
            return jnp.concatenate(
                [_dot(pb[g * gt:(g + 1) * gt],
                      new_ref[0, :, KVW + g * HEAD_DIM:KVW + (g + 1) * HEAD_DIM].astype(BF16))
                 for g in range(N_KV)], axis=0)

        merge([softmax_part(s_new, sel_new & causal, pv_new)])
        o_sel = acc_ref[...] * _safe_inv(l_ref[...])
        for g in range(N_KV):
            rows = slice(g * gt, (g + 1) * gt)
            _gate_and_store(o_ref, g, (ocmp_ref[0, rows, :], o_sel[rows], owin_ref[0, rows, :]),
                            (z0_ref, z1_ref, z2_ref), gate, t)


def _decode_sel(page_table, qn, selc, cache_kv, new_sel, ocmp, owin, u, gate, e4, bd, t):
    n_pages = page_table.shape[1]
    page = cache_kv.shape[1]
    n_steps = n_pages // PAGES_PER_STEP
    bps = PAGES_PER_STEP * (page // BLK)
    rows = N_HEADS * t

    def page_spec(part, k):
        return pl.BlockSpec((1, page, 1, N_KV, HEAD_DIM),
                            lambda b, c, pt: (pt[b, c * PAGES_PER_STEP + k], 0, part, 0, 0))

    grid_spec = pltpu.PrefetchScalarGridSpec(
        num_scalar_prefetch=1,
        grid=(bd, n_steps),
        in_specs=[pl.BlockSpec((1, rows, HEAD_DIM), lambda b, c, pt: (b, 0, 0)),
                  pl.BlockSpec((1, 1, N_KV * t, bps), lambda b, c, pt: (b, c, 0, 0)),
                  pl.BlockSpec((1, 1, N_KV * t, bps), lambda b, c, pt: (b, n_steps, 0, 0))]
        + [page_spec(2, k) for k in range(PAGES_PER_STEP)]
        + [page_spec(3, k) for k in range(PAGES_PER_STEP)]
        + [pl.BlockSpec((1, NEW_PAD, 2 * KVW), lambda b, c, pt: (b, 0, 0)),
           pl.BlockSpec((1, rows, HEAD_DIM), lambda b, c, pt: (b, 0, 0)),
           pl.BlockSpec((1, rows, HEAD_DIM), lambda b, c, pt: (b, 0, 0)),
           pl.BlockSpec((t, HD), lambda b, c, pt: (b, 1)),
           pl.BlockSpec((t, HD), lambda b, c, pt: (b, 2)),
           pl.BlockSpec((t, HD), lambda b, c, pt: (b, 3)),
           pl.BlockSpec((t, HEAD_DIM), lambda b, c, pt: (b, 0)),
           pl.BlockSpec(e4.shape, lambda b, c, pt: (0, 0, 0))],
        out_specs=pl.BlockSpec((t, HD), lambda b, c, pt: (b, 0)),
        scratch_shapes=[pltpu.VMEM((rows, 1), F32), pltpu.VMEM((rows, 1), F32),
                        pltpu.VMEM((rows, HEAD_DIM), F32)],
    )
    return pl.pallas_call(
        functools.partial(_decode_sel_kernel, t=t, n_steps=n_steps),
        grid_spec=grid_spec,
        out_shape=jax.ShapeDtypeStruct((bd * t, HD), BF16),
        compiler_params=_cparams(2),
        name="decode_sel",
    )(page_table, qn, selc, selc, *([cache_kv] * (2 * PAGES_PER_STEP)), new_sel, ocmp, owin,
      u, u, u, gate, e4)


def _rope_tables(pos):
    half = HEAD_DIM // 2
    inv = ROPE_THETA ** (-jnp.arange(half, dtype=F32) / half)
    ang = pos.astype(F32)[:, None] * inv[None, :]
    c = jnp.cos(ang)
    s = jnp.sin(ang)
    return jnp.concatenate([c, c], axis=-1), jnp.concatenate([-s, s], axis=-1)


def _block_group_expand(nblocks):
    col = jnp.arange(nblocks * BLK * N_KV, dtype=jnp.int32)
    in_block = (col // N_KV) // BLK == jnp.arange(nblocks, dtype=jnp.int32)[:, None]
    in_group = col % N_KV == jnp.arange(N_KV, dtype=jnp.int32)[:, None]
    return (in_block[None, :, :] & in_group[:, None, :]).astype(BF16)


def kernel(x_prompt, x_sample, state_conv, cache_kv, state_win, page_table, g_a, w_in_a, conv_w,
           w_out_a, g_kv, w_kv, g_k, w_cmp, g_b, w_qz, g_q, w_out_b):
    batch, seq, d = x_prompt.shape
    bd, t, _ = x_sample.shape
    ch = conv_w.shape[-1]
    n_pool, page = cache_kv.shape[:2]
    n_pages = page_table.shape[1]
    past = n_pages * page
    wbuf = state_win.shape[1]
    assert seq % BLK == 0 and page % BLK == 0 and t <= BLK and n_pages % PAGES_PER_STEP == 0

    w_exp3 = jnp.broadcast_to(w_cmp.reshape(BLK, 2 * N_KV, 1), (BLK, 2 * N_KV, HEAD_DIM))
    w_exp = w_exp3.reshape(BLK, 2 * KVW)
    n_gate = N_BRANCH * N_HEADS
    w_qz_t = jnp.swapaxes(w_qz, 1, 2)
    w_gate_t = jnp.pad(w_qz_t[:, 4 * HD:4 * HD + n_gate, :], ((0, 0), (0, HEAD_DIM - n_gate), (0, 0)))

    xp = x_prompt.reshape(batch * seq, d)
    xs = x_sample.reshape(bd * t, d)
    prev0 = jnp.zeros((batch, CONV_W - 1, ch), F32)
    xn_p, xn_s = _rms_cast(xp, g_a[0]), _rms_cast(xs, g_a[0])
    conv_p, conv_s = [], []
    for layer in range(N_A):
        y_p, st_p, y_s, c_s = _conv_matmul(xn_p, xn_s, w_in_a, layer, prev0, state_conv[layer],
                                           conv_w[layer], seq, t)
        conv_p.append(st_p)
        conv_s.append(c_s.reshape(bd, t, ch)[:, t - (CONV_W - 1):])
        gains = [g_a[layer + 1]] if layer + 1 < N_A else [g_kv, g_b[0]]
        xp, xs, xns = _matmul_res_norm(y_p, y_s, w_out_a, layer, xp, xs, gains)
        xn_p, xn_s = xns[0]
    conv_p, conv_s = jnp.stack(conv_p), jnp.stack(conv_s)
    (xn_kv_p, xn_kv_s), (xn_q_p, xn_q_s) = xns

    kv_p, kv_s = _matmul(xn_kv_p, xn_kv_s, w_kv[None], 0, 6 * KVW)
    cos_p, sin_p = _rope_tables(jnp.arange(seq, dtype=jnp.int32))
    cos_s, sin_s = _rope_tables(past + jnp.arange(t, dtype=jnp.int32))
    rows_p, win_p, ks_p, vst_p, kw_p, vwt_p = _kv_post(kv_p, cos_p, sin_p, g_k, seq, True)
    rows_s, win_s = _kv_post(kv_s, cos_s, sin_s, g_k, t, False)
    nb_p = seq // BLK
    kvc_p = _compress_rows(rows_p, w_exp3).reshape(batch * nb_p, 2 * KVW)
    vct_p = kvc_p.reshape(batch, nb_p, 2, KVW)[:, :, 1].transpose(0, 2, 1)

    cache4 = cache_kv.reshape(n_pool, page, 4 * N_KV, HEAD_DIM)
    kvc_past = _compress_pages(cache4, page_table, w_exp3).reshape(bd, past // BLK, 2 * KVW)
    rows_s3 = rows_s.reshape(bd, t, 4 * KVW)
    rows_pad = jnp.pad(rows_s3, ((0, 0), (0, NEW_PAD - t), (0, 0)))
    kvc_new = _compress(rows_pad[:, :BLK, :2 * KVW].reshape(bd * BLK, 2 * KVW), w_exp)
    nb_past = past // BLK
    bps = PAGES_PER_STEP * (page // BLK)
    nbp = nb_past + bps
    kvc_all = jnp.concatenate(
        [kvc_past, kvc_new[:, None, :], jnp.zeros((bd, bps - 1, 2 * KVW), F32)], axis=1)
    new_sel = rows_pad[:, :, 2 * KVW:]
    new_win = jnp.pad(win_s.reshape(bd, t, 2 * KVW), ((0, 0), (0, NEW_PAD - t), (0, 0)))
    state_win2 = state_win.reshape(bd, wbuf, 2 * KVW)
    e4 = _block_group_expand(bps // 2 if PAGES_PER_STEP % 2 == 0 else bps)

    for lb in range(N_B):
        u_p, u_s = _matmul(xn_q_p, xn_q_s, w_qz_t, lb, 4 * HD, w_is_t=True)
        gate_p, gate_s = _matmul(xn_q_p, xn_q_s, w_gate_t, lb, HEAD_DIM, w_is_t=True)
        o_p = _prompt_attn(u_p, gate_p, kvc_p, vct_p, ks_p, vst_p, kw_p, vwt_p, cos_p, sin_p,
                           g_q[lb:lb + 1], batch, seq)
        qn, sel, ocmp, owin = _decode_front(u_s, kvc_all, state_win2, new_win, cos_s, sin_s,
                                            g_q[lb:lb + 1], bd, t, past)
        selc = sel.reshape(bd, N_KV * t, nbp // bps, bps).transpose(0, 2, 1, 3)
        o_s = _decode_sel(page_table, qn, selc, cache_kv, new_sel, ocmp, owin, u_s, gate_s, e4, bd, t)
        if lb + 1 < N_B:
            xp, xs, [(xn_q_p, xn_q_s)] = _matmul_res_norm(o_p, o_s, w_out_b, lb, xp, xs, [g_b[lb + 1]])
        else:
            xp, xs = _matmul(o_p, o_s, w_out_b, lb, d, res=xp, res_s=xs)

    wn = min(WINDOW, seq)
    y_prompt = xp.reshape(batch, seq, d)
    y_sample = xs.reshape(bd, t, d)
    kv_rows_prompt = rows_p.reshape(batch, seq, 4, N_KV, HEAD_DIM)
    kv_rows_sample = rows_s.reshape(bd, t, 4, N_KV, HEAD_DIM)
    win_prompt = win_p.reshape(batch, seq, 2, N_KV, HEAD_DIM)[:, seq - wn:]
    kvw = jnp.concatenate([state_win, win_s.reshape(bd, t, 2, N_KV, HEAD_DIM)], axis=1)
    win_sample = kvw[:, kvw.shape[1] - min(WINDOW, kvw.shape[1]):]

    return (y_prompt, y_sample, conv_p, conv_s, kv_rows_prompt, kv_rows_sample, win_prompt, win_sample)
```

```python
import functools

import jax
import jax.numpy as jnp
from jax import lax
from jax.experimental import pallas as pl
from jax.experimental.pallas import tpu as pltpu

N_A = 2
N_B = 2
CONV_W = 3
N_HEADS = 16
N_KV = 4
GRP = N_HEADS // N_KV
HEAD_DIM = 128
N_BRANCH = 3
BLK = 64
N_SEL = 16
WINDOW = 512
ROPE_THETA = 10000.0
EPS = 1e-6
NEG = -1e30
FORCE = 1e3
LOG2E = 1.4426950408889634

HD = N_HEADS * HEAD_DIM
KVW = N_KV * HEAD_DIM
NEW_PAD = 128
KCH = 128
SPAN_KEYS = 512
PAGES_PER_STEP = 16
VMEM_LIMIT = 56 * 1024 * 1024

F32 = jnp.float32
BF16 = jnp.bfloat16


def _cparams(n_axes):
    return pltpu.CompilerParams(
        dimension_semantics=("arbitrary",) * n_axes, vmem_limit_bytes=VMEM_LIMIT)


def _dot_nt(a, b):
    return lax.dot_general(a, b, (((1,), (1,)), ((), ())), preferred_element_type=F32)


def _dot(a, b):
    return jnp.dot(a, b, preferred_element_type=F32)


def _pick(n, prefs):
    for p in prefs:
        if n % p == 0:
            return p
    return n


def _rms_cast_kernel(x_ref, g_ref, o_ref):
    x = x_ref[...]
    y = x * lax.rsqrt(jnp.mean(x * x, axis=-1, keepdims=True) + EPS)
    o_ref[...] = (y * g_ref[...]).astype(BF16)


def _rms_cast(x, g):
    n, d = x.shape
    tm = _pick(n, (512, 256, 128, 64, 8))
    return pl.pallas_call(
        _rms_cast_kernel,
        grid=(n // tm,),
        in_specs=[pl.BlockSpec((tm, d), lambda i: (i, 0)),
                  pl.BlockSpec((1, d), lambda i: (0, 0))],
        out_specs=pl.BlockSpec((tm, d), lambda i: (i, 0)),
        out_shape=jax.ShapeDtypeStruct((n, d), BF16),
        compiler_params=_cparams(1),
        name="rms_cast",
    )(x, g.reshape(1, d))


def _main_tile(nt):
    return lambda i: jnp.minimum(i, nt - 1)


def _mm_kernel(*refs, has_res, w_is_t, nt):
    it = iter(refs)
    x_ref, xs_ref, w_ref = next(it), next(it), next(it)
    r_ref, rs_ref = (next(it), next(it)) if has_res else (None, None)
    o_ref, os_ref, wb_ref = next(it), next(it), next(it)
    i = pl.program_id(1)

    @pl.when(i == 0)
    def _():
        w = w_ref[...]
        wb_ref[...] = (jnp.transpose(w) if w_is_t else w).astype(BF16)

    def project(xr, rr, orf):
        acc = _dot(xr[...], wb_ref[...])
        orf[...] = acc if rr is None else rr[...] + acc

    pl.when(i < nt)(functools.partial(project, x_ref, r_ref, o_ref))
    pl.when(i == nt)(functools.partial(project, xs_ref, rs_ref, os_ref))


def _matmul(x, xs, w, layer, m_out, res=None, res_s=None, w_is_t=False):
    n, k = x.shape
    ns = xs.shape[0]
    tm = _pick(n, (1024, 512, 256, 128, 64, 8))
    tn = _pick(m_out, (1024, 512, 256, 128))
    nt = n // tm
    main = _main_tile(nt)
    if w_is_t:
        w_spec = pl.BlockSpec((None, tn, k), lambda j, i: (layer, j, 0))
    else:
        w_spec = pl.BlockSpec((None, k, tn), lambda j, i: (layer, 0, j))
    in_specs = [pl.BlockSpec((tm, k), lambda j, i: (main(i), 0)),
                pl.BlockSpec((ns, k), lambda j, i: (0, 0)), w_spec]
    args = [x, xs, w]
    if res is not None:
        in_specs += [pl.BlockSpec((tm, tn), lambda j, i: (main(i), j)),
                     pl.BlockSpec((ns, tn), lambda j, i: (0, j))]
        args += [res, res_s]
    return pl.pallas_call(
        functools.partial(_mm_kernel, has_res=res is not None, w_is_t=w_is_t, nt=nt),
        grid=(m_out // tn, nt + 1),
        in_specs=in_specs,
        out_specs=[pl.BlockSpec((tm, tn), lambda j, i: (main(i), j)),
                   pl.BlockSpec((ns, tn), lambda j, i: (0, j))],
        out_shape=[jax.ShapeDtypeStruct((n, m_out), F32), jax.ShapeDtypeStruct((ns, m_out), F32)],
        scratch_shapes=[pltpu.VMEM((k, tn), BF16)],
        compiler_params=_cparams(2),
        name="matmul",
    )(*args)


def _mm_res_norm_kernel(*refs, n_norm, nt):
    x_ref, xs_ref, w_ref, r_ref, rs_ref = refs[:5]
    g_refs = refs[5:5 + n_norm]
    o_ref, os_ref = refs[5 + n_norm:7 + n_norm]
    xn_refs = refs[7 + n_norm:7 + 3 * n_norm]
    wb_ref = refs[7 + 3 * n_norm]
    i = pl.program_id(0)

    @pl.when(i == 0)
    def _():
        wb_ref[...] = w_ref[...].astype(BF16)

    def project(xr, rr, orf, xnrs):
        out = rr[...] + _dot(xr[...], wb_ref[...])
        orf[...] = out
        y = out * lax.rsqrt(jnp.mean(out * out, axis=-1, keepdims=True) + EPS)
        for g_ref, xn_ref in zip(g_refs, xnrs):
            xn_ref[...] = (y * g_ref[...]).astype(BF16)

    pl.when(i < nt)(functools.partial(project, x_ref, r_ref, o_ref, xn_refs[0::2]))
    pl.when(i == nt)(functools.partial(project, xs_ref, rs_ref, os_ref, xn_refs[1::2]))


def _matmul_res_norm(x, xs, w, layer, res, res_s, gains):
    n, k = x.shape
    ns = xs.shape[0]
    d = res.shape[1]
    tm = _pick(n, (512, 256, 128, 64, 8))
    nt = n // tm
    main = _main_tile(nt)
    row = lambda i: (main(i), 0)
    side = lambda i: (0, 0)
    n_norm = len(gains)
    outs = pl.pallas_call(
        functools.partial(_mm_res_norm_kernel, n_norm=n_norm, nt=nt),
        grid=(nt + 1,),
        in_specs=[pl.BlockSpec((tm, k), row), pl.BlockSpec((ns, k), side),
                  pl.BlockSpec((None, k, d), lambda i: (layer, 0, 0), pipeline_mode=pl.Buffered(1)),
                  pl.BlockSpec((tm, d), row), pl.BlockSpec((ns, d), side)]
        + [pl.BlockSpec((1, d), side)] * n_norm,
        out_specs=[pl.BlockSpec((tm, d), row), pl.BlockSpec((ns, d), side)]
        + [pl.BlockSpec((tm, d), row), pl.BlockSpec((ns, d), side)] * n_norm,
        out_shape=[jax.ShapeDtypeStruct((n, d), F32), jax.ShapeDtypeStruct((ns, d), F32)]
        + [jax.ShapeDtypeStruct((n, d), BF16), jax.ShapeDtypeStruct((ns, d), BF16)] * n_norm,
        scratch_shapes=[pltpu.VMEM((k, d), BF16)],
        compiler_params=_cparams(1),
        name="matmul_res_norm",
    )(x, xs, w, res, res_s, *[g.reshape(1, d) for g in gains])
    return outs[0], outs[1], [(outs[2 + 2 * j], outs[3 + 2 * j]) for j in range(n_norm)]


def _conv_taps(c, c1, c2, w_ref):
    return w_ref[0:1, :] * c2 + w_ref[1:2, :] * c1 + w_ref[2:3, :] * c


def _conv_mm_kernel(x_ref, xs_ref, wb_ref, wc_ref, wh_ref, wz_ref, prev_ref, p1_ref, p2_ref, cw_ref,
                    y_ref, st_ref, ys_ref, cs_ref, wbf_ref, carry_ref, *, tps, nt, side_seq):
    i = pl.program_id(1)

    @pl.when(i == 0)
    def _():
        for k, wr in enumerate((wb_ref, wc_ref, wh_ref, wz_ref)):
            wbf_ref[k] = wr[...].astype(BF16)

    def parts(xr):
        x = xr[...]
        return (_dot(x, wbf_ref[0]), _dot(x, wbf_ref[1]) * _dot(x, wbf_ref[2]), _dot(x, wbf_ref[3]))

    @pl.when(i < nt)
    def _():
        b, c, z = parts(x_ref)
        tm = c.shape[0]

        @pl.when(i % tps == 0)
        def _():
            carry_ref[0:2, :] = prev_ref[0]

        p0 = carry_ref[0:1, :]
        p1 = carry_ref[1:2, :]
        row = lax.broadcasted_iota(jnp.int32, c.shape, 0)
        c1 = jnp.where(row == 0, p1, pltpu.roll(c, 1, 0))
        c2 = jnp.where(row == 0, p0, jnp.where(row == 1, p1, pltpu.roll(c, 2, 0)))
        y_ref[...] = (jax.nn.silu(z) * (b * _conv_taps(c, c1, c2, cw_ref))).astype(BF16)
        last = c[tm - 2:tm, :]
        carry_ref[0:2, :] = last
        st_ref[0] = last

    @pl.when(i == nt)
    def _():
        b, c, z = parts(xs_ref)
        pos = lax.broadcasted_iota(jnp.int32, c.shape, 0) & (side_seq - 1)
        c1 = jnp.where(pos == 0, p1_ref[...], pltpu.roll(c, 1, 0))
        c2 = jnp.where(pos < 2, p2_ref[...], pltpu.roll(c, 2, 0))
        ys_ref[...] = (jax.nn.silu(z) * (b * _conv_taps(c, c1, c2, cw_ref))).astype(BF16)
        cs_ref[...] = c


def _conv_matmul(xn, xs, w_in, layer, prev, prev_s, w_conv, seq, side_seq):
    n, k = xn.shape
    ns = xs.shape[0]
    ch = w_in.shape[2] // 4
    tm = _pick(seq, (1024, 512, 256, 128, 64, 8))
    tc = _pick(ch, (256, 128))
    tps = seq // tm
    nc = ch // tc
    nt = n // tm
    main = _main_tile(nt)
    assert side_seq >= CONV_W - 1 and side_seq & (side_seq - 1) == 0
    p2 = jnp.pad(prev_s, ((0, 0), (0, side_seq - 2), (0, 0))).reshape(ns, ch)
    p1 = jnp.pad(prev_s[:, 1:2], ((0, 0), (0, side_seq - 1), (0, 0))).reshape(ns, ch)

    def w_spec(part):
        return pl.BlockSpec((None, k, tc), lambda j, i: (layer, 0, part * nc + j))

    side = pl.BlockSpec((ns, tc), lambda j, i: (0, j))
    return pl.pallas_call(
        functools.partial(_conv_mm_kernel, tps=tps, nt=nt, side_seq=side_seq),
        grid=(nc, nt + 1),
        in_specs=[pl.BlockSpec((tm, k), lambda j, i: (main(i), 0)),
                  pl.BlockSpec((ns, k), lambda j, i: (0, 0))] + [w_spec(p) for p in range(4)]
        + [pl.BlockSpec((1, 2, tc), lambda j, i: (main(i) // tps, 0, j)), side, side,
           pl.BlockSpec((CONV_W, tc), lambda j, i: (0, j))],
        out_specs=[pl.BlockSpec((tm, tc), lambda j, i: (main(i), j)),
                   pl.BlockSpec((1, 2, tc), lambda j, i: (main(i) // tps, 0, j)), side, side],
        out_shape=[jax.ShapeDtypeStruct((n, ch), BF16),
                   jax.ShapeDtypeStruct((n // seq, CONV_W - 1, ch), F32),
                   jax.ShapeDtypeStruct((ns, ch), BF16),
                   jax.ShapeDtypeStruct((ns, ch), F32)],
        scratch_shapes=[pltpu.VMEM((4, k, tc), BF16), pltpu.VMEM((8, tc), F32)],
        compiler_params=_cparams(2),
        name="conv_matmul",
    )(xn, xs, w_in, w_in, w_in, w_in, prev, p1, p2, w_conv)


def _norm_rope(x, g, cos2, sin2):
    y = x * lax.rsqrt(jnp.mean(x * x, axis=-1, keepdims=True) + EPS) * g
    return y * cos2 + pltpu.roll(y, HEAD_DIM // 2, 1) * sin2


def _kv_post_kernel(kv_ref, cos_ref, sin_ref, gk_ref, rows_ref, win_ref, *attn_refs):
    cos2 = cos_ref[...]
    sin2 = sin_ref[...]
    tm = kv_ref.shape[0]
    for br in range(N_BRANCH):
        base = br * 2 * KVW
        out_ref, row0 = (rows_ref, br * 2 * N_KV) if br < 2 else (win_ref, 0)
        for g in range(N_KV):
            c0 = base + g * HEAD_DIM
            k = _norm_rope(kv_ref[:, c0:c0 + HEAD_DIM], gk_ref[br:br + 1, :], cos2, sin2)
            out_ref[:, row0 + g, :] = k
            out_ref[:, row0 + N_KV + g, :] = kv_ref[:, c0 + KVW:c0 + KVW + HEAD_DIM]
            if attn_refs and br >= 1:
                attn_refs[2 * (br - 1)][:, g * HEAD_DIM:(g + 1) * HEAD_DIM] = k.astype(BF16)
        if attn_refs and br >= 1:
            v = kv_ref[:, base + KVW:base + 2 * KVW]
            vt_ref = attn_refs[2 * (br - 1) + 1]
            for a in range(tm // KCH):
                vt_ref[a] = jnp.transpose(v[a * KCH:(a + 1) * KCH, :]).astype(BF16)


def _kv_post(kv, cos2, sin2, g_k, seq, for_attn):
    n = kv.shape[0]
    tm = _pick(seq, (256, 128, 64, 8))
    tps = seq // tm
    out_specs = [pl.BlockSpec((tm, 4 * N_KV, HEAD_DIM), lambda i: (i, 0, 0)),
                 pl.BlockSpec((tm, 2 * N_KV, HEAD_DIM), lambda i: (i, 0, 0))]
    out_shape = [jax.ShapeDtypeStruct((n, 4 * N_KV, HEAD_DIM), F32),
                 jax.ShapeDtypeStruct((n, 2 * N_KV, HEAD_DIM), F32)]
    if for_attn:
        assert tm % KCH == 0
        for _ in range(2):
            out_specs += [pl.BlockSpec((tm, KVW), lambda i: (i, 0)),
                          pl.BlockSpec((tm // KCH, KVW, KCH), lambda i: (i, 0, 0))]
            out_shape += [jax.ShapeDtypeStruct((n, KVW), BF16),
                          jax.ShapeDtypeStruct((n // KCH, KVW, KCH), BF16)]
    return pl.pallas_call(
        _kv_post_kernel,
        grid=(n // tm,),
        in_specs=[pl.BlockSpec((tm, 6 * KVW), lambda i: (i, 0)),
                  pl.BlockSpec((tm, HEAD_DIM), lambda i: (i % tps, 0)),
                  pl.BlockSpec((tm, HEAD_DIM), lambda i: (i % tps, 0)),
                  pl.BlockSpec((N_BRANCH, HEAD_DIM), lambda i: (0, 0))],
        out_specs=out_specs,
        out_shape=out_shape,
        compiler_params=_cparams(1),
        name="kv_post",
    )(kv, cos2, sin2, g_k)


def _compress_kernel(rows_ref, w_ref, o_ref):
    x = rows_ref[...]
    nb = x.shape[0] // BLK
    x = x.reshape(nb, BLK, x.shape[1]) * w_ref[...][None]
    o_ref[...] = jnp.sum(x, axis=1)


def _compress(rows, w_exp):
    n = rows.shape[0]
    nblk = n // BLK
    per = _pick(nblk, (8,))
    return pl.pallas_call(
        _compress_kernel,
        grid=(nblk // per,),
        in_specs=[pl.BlockSpec((per * BLK, 2 * KVW), lambda i: (i, 0)),
                  pl.BlockSpec((BLK, 2 * KVW), lambda i: (0, 0))],
        out_specs=pl.BlockSpec((per, 2 * KVW), lambda i: (i, 0)),
        out_shape=jax.ShapeDtypeStruct((nblk, 2 * KVW), F32),
        compiler_params=_cparams(1),
        name="compress",
    )(rows, w_exp)


def _block_sums(x, w):
    nb = x.shape[0] // BLK
    return jnp.sum(x.reshape(nb, BLK, 2 * N_KV, HEAD_DIM) * w[None], axis=1)


def _compress_rows_kernel(rows_ref, w_ref, o_ref):
    o_ref[...] = _block_sums(rows_ref[...], w_ref[...])


def _compress_rows(rows3, w_exp3):
    n = rows3.shape[0]
    nblk = n // BLK
    per = _pick(nblk, (8,))
    return pl.pallas_call(
        _compress_rows_kernel,
        grid=(nblk // per,),
        in_specs=[pl.BlockSpec((per * BLK, 2 * N_KV, HEAD_DIM), lambda i: (i, 0, 0)),
                  pl.BlockSpec((BLK, 2 * N_KV, HEAD_DIM), lambda i: (0, 0, 0))],
        out_specs=pl.BlockSpec((per, 2 * N_KV, HEAD_DIM), lambda i: (i, 0, 0)),
        out_shape=jax.ShapeDtypeStruct((nblk, 2 * N_KV, HEAD_DIM), F32),
        compiler_params=_cparams(1),
        name="compress_rows",
    )(rows3, w_exp3)


def _compress_pages_kernel(pt_ref, *refs):
    page_refs = refs[:PAGES_PER_STEP]
    w_ref = refs[PAGES_PER_STEP]
    o_ref = refs[PAGES_PER_STEP + 1]
    w = w_ref[...]
    for k, pr in enumerate(page_refs):
        nb = pr.shape[1] // BLK
        o_ref[0, k * nb:(k + 1) * nb] = _block_sums(pr[0], w)


def _compress_pages(cache4, page_table, w_exp3):
    bd, n_pages = page_table.shape
    page = cache4.shape[1]
    bpp = page // BLK
    steps = n_pages // PAGES_PER_STEP
    rows_out = PAGES_PER_STEP * bpp

    def page_spec(k):
        return pl.BlockSpec((1, page, 2 * N_KV, HEAD_DIM),
                            lambda b, c, pt: (pt[b, c * PAGES_PER_STEP + k], 0, 0, 0))

    grid_spec = pltpu.PrefetchScalarGridSpec(
        num_scalar_prefetch=1,
        grid=(bd, steps),
        in_specs=[page_spec(k) for k in range(PAGES_PER_STEP)]
        + [pl.BlockSpec((BLK, 2 * N_KV, HEAD_DIM), lambda b, c, pt: (0, 0, 0))],
        out_specs=pl.BlockSpec((1, rows_out, 2 * N_KV, HEAD_DIM), lambda b, c, pt: (b, c, 0, 0)),
    )
    return pl.pallas_call(
        _compress_pages_kernel,
        grid_spec=grid_spec,
        out_shape=jax.ShapeDtypeStruct((bd, n_pages * bpp, 2 * N_KV, HEAD_DIM), F32),
        compiler_params=_cparams(2),
        name="compress_pages",
    )(page_table, *([cache4] * PAGES_PER_STEP), w_exp3)


def _softmax_parts(s, mask):
    sm = jnp.where(mask, s, NEG)
    m = jnp.max(sm, axis=-1, keepdims=True)
    p = jnp.where(mask, jnp.exp(sm - m), 0.0)
    return p, m, jnp.sum(p, axis=-1, keepdims=True)


def _safe_inv(l):
    return jnp.where(l > 0.0, 1.0 / l, 0.0)


def _topk_mask(score, n_f, k):
    sel = jnp.zeros(score.shape, F32)
    big = jnp.float32(score.shape[-1] + 1)
    for _ in range(k):
        mx = jnp.max(score, axis=-1, keepdims=True)
        idx = jnp.min(jnp.where(score == mx, n_f, big), axis=-1, keepdims=True)
        hit = n_f == idx
        sel = jnp.where(hit, 1.0, sel)
        score = jnp.where(hit, -jnp.inf, score)
    return sel


def _select_blocks(psum, cur, n_i):
    forced = ((n_i == 0) | (n_i == cur) | (n_i == cur - 1)).astype(F32)
    valid = n_i <= cur
    score = jnp.where(valid, psum + FORCE * forced, NEG)
    sel = _topk_mask(score, n_i.astype(F32), min(N_SEL, psum.shape[-1]))
    return jnp.where(valid, sel, 0.0)


def _q_heads(uq_ref, g, gq, cos2, sin2, scale=None):
    hs = []
    for r in range(GRP):
        c0 = (g * GRP + r) * HEAD_DIM
        h = _norm_rope(uq_ref[:, c0:c0 + HEAD_DIM], gq, cos2, sin2)
        hs.append(h if scale is None else h * scale)
    return jnp.concatenate(hs, axis=0).astype(BF16)


def _tile_rows(x, reps):
    return jnp.concatenate([x] * reps, axis=0)


def _gate_and_store(o_ref, g, outs, z_refs, gate, t):
    for r in range(GRP):
        h = g * GRP + r
        acc = None
        for br in range(N_BRANCH):
            zg = jax.nn.silu(z_refs[br][:, h * HEAD_DIM:(h + 1) * HEAD_DIM]) \
                * gate[:, br * N_HEADS + h:br * N_HEADS + h + 1]
            term = outs[br][r * t:(r + 1) * t, :] * zg
            acc = term if acc is None else acc + term
        o_ref[:, h * HEAD_DIM:(h + 1) * HEAD_DIM] = acc.astype(o_ref.dtype)


def _rank_select_t(score_t, n_col, k):
    rank = jnp.zeros(score_t.shape, F32)
    for m in range(score_t.shape[0]):
        row = score_t[m:m + 1, :]
        tie = (n_col > m).astype(F32)
        rank = rank + jnp.where(row > score_t, 1.0, 0.0) + jnp.where(row == score_t, tie, 0.0)
    return jnp.where(rank < k, 1.0, 0.0)


def _prompt_attn_kernel(uq_ref, z0_ref, z1_ref, z2_ref, gate_ref, kvc_ref, vct_ref, ks_ref, vst_ref,
                        kw_ref, vwt_ref, cos_ref, sin_ref, gq_ref, o_ref,
                        q_scr, s_scr, bias_scr, mrun_scr, lrun_scr, ot_scr, ocmp_scr, *, seq, spk):
    i = pl.program_id(1)
    t = BLK
    lanes = GRP * t
    nb = seq // BLK
    nch = seq // KCH
    nwc = min(WINDOW // KCH + 1, nch)
    bpc = KCH // BLK
    k_in_blk = lax.broadcasted_iota(jnp.int32, (BLK, 1), 0)
    cos2 = cos_ref[...]
    sin2 = sin_ref[...]
    gq = gq_ref[...]
    gate = jax.nn.sigmoid(gate_ref[...])
    qpos_l = i * t + (lax.broadcasted_iota(jnp.int32, (1, lanes), 1) & (t - 1))
    qpos_h = qpos_l[:, 0:2 * t]
    cur_h = qpos_h // BLK
    n_col = lax.broadcasted_iota(jnp.int32, (nb, 1), 0)
    cmask_t = (n_col * BLK + (BLK - 1)) <= qpos_l
    forced_t = ((n_col == 0) | (n_col == cur_h) | (n_col == cur_h - 1)).astype(F32)
    valid_t = n_col <= cur_h
    k_in_chunk = lax.broadcasted_iota(jnp.int32, (KCH, 1), 0)

    def group_max(x):
        return jnp.max(x.reshape(KCH // 8, 8, lanes), axis=0)

    def group_sum(x):
        return jnp.sum(x.reshape(KCH // 8, 8, lanes), axis=0)

    def finish(lrun, ot):
        l = jnp.sum(lrun, axis=0, keepdims=True)
        return jnp.transpose(ot * _safe_inv(l))

    for g in range(N_KV):
        q = _q_heads(uq_ref, g, gq, cos2, sin2, LOG2E * HEAD_DIM ** -0.5)
        ksl = slice(g * HEAD_DIM, (g + 1) * HEAD_DIM)
        kc = kvc_ref[0, :, ksl].astype(BF16)

        st = jnp.where(cmask_t, _dot_nt(kc, q), NEG)
        pt = jnp.where(cmask_t, jnp.exp2(st - jnp.max(st, axis=0, keepdims=True)), 0.0)
        pt = pt * _safe_inv(jnp.sum(pt, axis=0, keepdims=True))
        ocmp_scr[g] = _dot(vct_ref[0, ksl, :].astype(BF16), pt.astype(BF16))

        half = pt[:, 0:2 * t] + pt[:, 2 * t:4 * t]
        psum_t = half + pltpu.roll(half, t, 1)
        score_t = jnp.where(valid_t, psum_t + FORCE * forced_t, NEG)
        sel_t = jnp.where(valid_t, _rank_select_t(score_t, n_col, min(N_SEL, nb)), 0.0)
        bias_t = (sel_t - 1.0) * (-NEG)
        bias_scr[g] = jnp.concatenate([bias_t, bias_t], axis=1)

        q_scr[g] = q
        mrun_scr[g] = jnp.full((8, lanes), NEG, F32)
        lrun_scr[g] = jnp.zeros((8, lanes), F32)
        ot_scr[g] = jnp.zeros((HEAD_DIM, lanes), F32)

    def sel_pass1(sc, diagonal):
        for g in range(N_KV):
            ksl = slice(g * HEAD_DIM, (g + 1) * HEAD_DIM)
            st = _dot_nt(ks_ref[0, sc * spk:(sc + 1) * spk, ksl], q_scr[g])
            mx = mrun_scr[g]
            for j in range(spk // BLK):
                n = sc * (spk // BLK) + j
                blk = st[j * BLK:(j + 1) * BLK] + bias_scr[g, n:n + 1, :]
                if diagonal:
                    blk = jnp.where(n * BLK + k_in_blk <= qpos_l, blk, NEG)
                s_scr[g, n // bpc, (n % bpc) * BLK:(n % bpc + 1) * BLK, :] = blk
                mx = jnp.maximum(mx, jnp.max(blk.reshape(BLK // 8, 8, lanes), axis=0))
            mrun_scr[g] = mx

    def sel_pass2(sc):
        for g in range(N_KV):
            ksl = slice(g * HEAD_DIM, (g + 1) * HEAD_DIM)
            m = jnp.max(mrun_scr[g], axis=0, keepdims=True)
            lrun = lrun_scr[g]
            ot = ot_scr[g]
            for c in range(sc * (spk // KCH), (sc + 1) * (spk // KCH)):
                p = jnp.exp2(s_scr[g, c] - m)
                lrun = lrun + group_sum(p)
                ot = ot + _dot(vst_ref[c, ksl, :], p.astype(BF16))
            lrun_scr[g] = lrun
            ot_scr[g] = ot

    for sc in range(seq // spk):
        pl.when((sc + 1) * spk <= i * t)(functools.partial(sel_pass1, sc, False))
        pl.when((sc * spk <= i * t) & (i * t < (sc + 1) * spk))(functools.partial(sel_pass1, sc, True))
    for sc in range(seq // spk):
        pl.when(sc * spk <= i * t)(functools.partial(sel_pass2, sc))

    cs = jnp.minimum(jnp.maximum(i - WINDOW // BLK, 0) // bpc, nch - nwc)
    for g in range(N_KV):
        ksl = slice(g * HEAD_DIM, (g + 1) * HEAD_DIM)
        q = q_scr[g]
        o_sel = finish(lrun_scr[g], ot_scr[g])
        mrun = jnp.full((8, lanes), NEG, F32)
        for j in range(nwc):
            c = cs + j
            k = kw_ref[0, pl.ds(pl.multiple_of(c * KCH, KCH), KCH), ksl]
            dpos = qpos_l - (c * KCH + k_in_chunk)
            st = jnp.where((dpos >= 0) & (dpos < WINDOW), _dot_nt(k, q), NEG)
            s_scr[g, j] = st
            mrun = jnp.maximum(mrun, group_max(st))
        m = jnp.max(mrun, axis=0, keepdims=True)
        lrun = jnp.zeros((8, lanes), F32)
        ot = jnp.zeros((HEAD_DIM, lanes), F32)
        for j in range(nwc):
            p = jnp.exp2(s_scr[g, j] - m)
            lrun = lrun + group_sum(p)
            ot = ot + _dot(vwt_ref[cs + j, ksl, :], p.astype(BF16))
        o_win = finish(lrun, ot)

        _gate_and_store(o_ref, g, (jnp.transpose(ocmp_scr[g]), o_sel, o_win),
                        (z0_ref, z1_ref, z2_ref), gate, t)


def _prompt_attn(u, gate, kvc, vct, ks, vst, kw, vwt, cos2, sin2, gq, batch, seq):
    n = u.shape[0]
    t = BLK
    spk = _pick(seq, (SPAN_KEYS, KCH))
    assert GRP * t == 2 * KCH and seq % KCH == 0 and spk % KCH == 0
    nqb = seq // t
    nb = seq // BLK
    nch = seq // KCH
    lanes = GRP * t
    row = lambda b, i: (b * nqb + i, 0)
    keys = pl.BlockSpec((1, seq, KVW), lambda b, i: (b, 0, 0))
    vals_t = pl.BlockSpec((nch, KVW, KCH), lambda b, i: (b, 0, 0))
    return pl.pallas_call(
        functools.partial(_prompt_attn_kernel, seq=seq, spk=spk),
        grid=(batch, nqb),
        in_specs=[pl.BlockSpec((t, HD), row),
                  pl.BlockSpec((t, HD), lambda b, i: (b * nqb + i, 1)),
                  pl.BlockSpec((t, HD), lambda b, i: (b * nqb + i, 2)),
                  pl.BlockSpec((t, HD), lambda b, i: (b * nqb + i, 3)),
                  pl.BlockSpec((t, HEAD_DIM), row),
                  pl.BlockSpec((1, nb, 2 * KVW), lambda b, i: (b, 0, 0)),
                  pl.BlockSpec((1, KVW, nb), lambda b, i: (b, 0, 0)),
                  keys, vals_t, keys, vals_t,
                  pl.BlockSpec((t, HEAD_DIM), lambda b, i: (i, 0)),
                  pl.BlockSpec((t, HEAD_DIM), lambda b, i: (i, 0)),
                  pl.BlockSpec((1, HEAD_DIM), lambda b, i: (0, 0))],
        out_specs=pl.BlockSpec((t, HD), row),
        out_shape=jax.ShapeDtypeStruct((n, HD), BF16),
        scratch_shapes=[pltpu.VMEM((N_KV, lanes, HEAD_DIM), BF16),
                        pltpu.VMEM((N_KV, nch, KCH, lanes), F32),
                        pltpu.VMEM((N_KV, nb, lanes), F32),
                        pltpu.VMEM((N_KV, 8, lanes), F32),
                        pltpu.VMEM((N_KV, 8, lanes), F32),
                        pltpu.VMEM((N_KV, HEAD_DIM, lanes), F32),
                        pltpu.VMEM((N_KV, HEAD_DIM, lanes), F32)],
        compiler_params=_cparams(2),
        name="prompt_attn",
    )(u, u, u, u, gate, kvc.reshape(batch, nb, 2 * KVW), vct, ks.reshape(batch, seq, KVW), vst,
      kw.reshape(batch, seq, KVW), vwt, cos2, sin2, gq)


def _decode_front_kernel(uq_ref, kvc_ref, swin_ref, nwin_ref, cos_ref, sin_ref, gq_ref,
                         qn_ref, sel_ref, ocmp_ref, owin_ref, *, past, t):
    scale = HEAD_DIM ** -0.5
    cos2 = cos_ref[...]
    sin2 = sin_ref[...]
    gq = gq_ref[...]
    nbp = kvc_ref.shape[1]
    wbuf = swin_ref.shape[1]
    qpos = past + lax.broadcasted_iota(jnp.int32, (t, 1), 0)
    qpos_r = _tile_rows(qpos, GRP)
    cur = qpos // BLK
    n_i = lax.broadcasted_iota(jnp.int32, (1, nbp), 1)
    cmask = (n_i * BLK + (BLK - 1)) <= qpos_r
    kwpos_old = (past - wbuf) + lax.broadcasted_iota(jnp.int32, (1, wbuf), 1)
    kwpos_new = past + lax.broadcasted_iota(jnp.int32, (1, NEW_PAD), 1)
    d_old = qpos_r - kwpos_old
    d_new = qpos_r - kwpos_new
    m_old = (d_old >= 0) & (d_old < WINDOW) & (kwpos_old >= 0)
    m_new = (d_new >= 0) & (d_new < WINDOW) & (kwpos_new < past + t)

    for g in range(N_KV):
        q = _q_heads(uq_ref, g, gq, cos2, sin2)
        rows = slice(g * GRP * t, (g + 1) * GRP * t)
        ksl = slice(g * HEAD_DIM, (g + 1) * HEAD_DIM)
        vsl = slice(KVW + g * HEAD_DIM, KVW + (g + 1) * HEAD_DIM)
        qn_ref[0, rows, :] = q

        s = _dot_nt(q, kvc_ref[0, :, ksl].astype(BF16)) * scale
        p, _, l = _softmax_parts(s, cmask)
        p = p * _safe_inv(l)
        ocmp_ref[0, rows, :] = _dot(p.astype(BF16), kvc_ref[0, :, vsl].astype(BF16))

        psum = p[0:t]
        for r in range(1, GRP):
            psum = psum + p[r * t:(r + 1) * t]
        sel_ref[0, g * t:(g + 1) * t, :] = _select_blocks(psum, cur, n_i)

        s_old = _dot_nt(q, swin_ref[0, :, ksl].astype(BF16)) * scale
        s_new = _dot_nt(q, nwin_ref[0, :, ksl].astype(BF16)) * scale
        mx = jnp.maximum(jnp.max(jnp.where(m_old, s_old, NEG), axis=-1, keepdims=True),
                         jnp.max(jnp.where(m_new, s_new, NEG), axis=-1, keepdims=True))
        p_old = jnp.where(m_old, jnp.exp(jnp.where(m_old, s_old, NEG) - mx), 0.0)
        p_new = jnp.where(m_new, jnp.exp(jnp.where(m_new, s_new, NEG) - mx), 0.0)
        l = jnp.sum(p_old, axis=-1, keepdims=True) + jnp.sum(p_new, axis=-1, keepdims=True)
        o = _dot(p_old.astype(BF16), swin_ref[0, :, vsl].astype(BF16)) \
            + _dot(p_new.astype(BF16), nwin_ref[0, :, vsl].astype(BF16))
        owin_ref[0, rows, :] = o * _safe_inv(l)


def _decode_front(u, kvc_all, state_win2, new_win, cos2, sin2, gq, bd, t, past):
    nbp = kvc_all.shape[1]
    wbuf = state_win2.shape[1]
    rows = N_HEADS * t
    return pl.pallas_call(
        functools.partial(_decode_front_kernel, past=past, t=t),
        grid=(bd,),
        in_specs=[pl.BlockSpec((t, HD), lambda b: (b, 0)),
                  pl.BlockSpec((1, nbp, 2 * KVW), lambda b: (b, 0, 0)),
                  pl.BlockSpec((1, wbuf, 2 * KVW), lambda b: (b, 0, 0)),
                  pl.BlockSpec((1, NEW_PAD, 2 * KVW), lambda b: (b, 0, 0)),
                  pl.BlockSpec((t, HEAD_DIM), lambda b: (0, 0)),
                  pl.BlockSpec((t, HEAD_DIM), lambda b: (0, 0)),
                  pl.BlockSpec((1, HEAD_DIM), lambda b: (0, 0))],
        out_specs=[pl.BlockSpec((1, rows, HEAD_DIM), lambda b: (b, 0, 0)),
                   pl.BlockSpec((1, N_KV * t, nbp), lambda b: (b, 0, 0)),
                   pl.BlockSpec((1, rows, HEAD_DIM), lambda b: (b, 0, 0)),
                   pl.BlockSpec((1, rows, HEAD_DIM), lambda b: (b, 0, 0))],
        out_shape=[jax.ShapeDtypeStruct((bd, rows, HEAD_DIM), BF16),
                   jax.ShapeDtypeStruct((bd, N_KV * t, nbp), F32),
                   jax.ShapeDtypeStruct((bd, rows, HEAD_DIM), F32),
                   jax.ShapeDtypeStruct((bd, rows, HEAD_DIM), F32)],
        compiler_params=_cparams(1),
        name="decode_front",
    )(u, kvc_all, state_win2, new_win, cos2, sin2, gq)


def _values_product(vx, pb):
    return _dot(pb, vx)


def _decode_sel_kernel(pt_ref, *refs, t, n_steps):
    del pt_ref
    np_ = PAGES_PER_STEP
    (qn_ref, selc_ref, seln_ref) = refs[:3]
    k_refs = refs[3:3 + np_]
    v_refs = refs[3 + np_:3 + 2 * np_]
    (new_ref, ocmp_ref, owin_ref, z0_ref, z1_ref, z2_ref, gate_ref, e_ref,
     o_ref, m_ref, l_ref, acc_ref) = refs[3 + 2 * np_:]
    c = pl.program_id(1)
    scale = HEAD_DIM ** -0.5
    gt = GRP * t

    @pl.when(c == 0)
    def _():
        m_ref[...] = jnp.full(m_ref.shape, NEG, F32)
        l_ref[...] = jnp.zeros(l_ref.shape, F32)
        acc_ref[...] = jnp.zeros(acc_ref.shape, F32)

    def softmax_part(s, mask, pv_of):
        sm = jnp.where(mask, s, NEG)
        m = jnp.max(sm, axis=-1, keepdims=True)
        p = jnp.where(mask, jnp.exp(sm - m), 0.0)
        return m, jnp.sum(p, axis=-1, keepdims=True), pv_of(p.astype(BF16))

    def merge(parts):
        m_old = m_ref[...]
        m_new = m_old
        for m, _, _ in parts:
            m_new = jnp.maximum(m_new, m)
        alpha = jnp.exp(m_old - m_new)
        l = alpha * l_ref[...]
        acc = alpha * acc_ref[...]
        for m, lp, pv in parts:
            w = jnp.exp(m - m_new)
            l = l + w * lp
            acc = acc + w * pv
        l_ref[...] = l
        acc_ref[...] = acc
        m_ref[...] = m_new

    def cache_rows(page_refs):
        rows = [pr[0, :, 0].reshape(pr.shape[1] * N_KV, HEAD_DIM) for pr in page_refs]
        return jnp.concatenate(rows, axis=0).astype(BF16)

    halves = 2 if np_ % 2 == 0 else 1
    pp = np_ // halves
    bph = e_ref.shape[1]
    parts = []
    for h in range(halves):
        vx = cache_rows(v_refs[h * pp:(h + 1) * pp])
        s = _dot_nt(qn_ref[0], cache_rows(k_refs[h * pp:(h + 1) * pp])) * scale
        mask = jnp.concatenate(
            [_dot(selc_ref[0, 0, g * t:(g + 1) * t, h * bph:(h + 1) * bph].astype(BF16), e_ref[g])
             for g in range(N_KV) for _ in range(GRP)], axis=0) > 0.5
        parts.append(softmax_part(s, mask, functools.partial(_values_product, vx)))
    merge(parts)

    @pl.when(c == n_steps - 1)
    def _():
        gate = jax.nn.sigmoid(gate_ref[...])
        tq = _tile_rows(lax.broadcasted_iota(jnp.int32, (t, 1), 0), N_HEADS)
        causal = lax.broadcasted_iota(jnp.int32, (1, NEW_PAD), 1) <= tq
        sel_new = jnp.concatenate([seln_ref[0, 0, g * t:(g + 1) * t, 0:1]
                                   for g in range(N_KV) for _ in range(GRP)], axis=0) > 0.5
        s_new = jnp.concatenate(
            [_dot_nt(qn_ref[0, g * gt:(g + 1) * gt, :],
                     new_ref[0, :, g * HEAD_DIM:(g + 1) * HEAD_DIM].astype(BF16))
             for g in range(N_KV)], axis=0) * scale

        def pv_new(pb):
            return jnp.concatenate(
                [_dot(pb[g * gt:(g + 1) * gt],
                      new_ref[0, :, KVW + g * HEAD_DIM:KVW + (g + 1) * HEAD_DIM].astype(BF16))
                 for g in range(N_KV)], axis=0)

        merge([softmax_part(s_new, sel_new & causal, pv_new)])
        o_sel = acc_ref[...] * _safe_inv(l_ref[...])
        for g in range(N_KV):
            rows = slice(g * gt, (g + 1) * gt)
            _gate_and_store(o_ref, g, (ocmp_ref[0, rows, :], o_sel[rows], owin_ref[0, rows, :]),
                            (z0_ref, z1_ref, z2_ref), gate, t)


def _decode_sel(page_table, qn, selc, cache_kv, new_sel, ocmp, owin, u, gate, e4, bd, t):
    n_pages = page_table.shape[1]
    page = cache_kv.shape[1]
    n_steps = n_pages // PAGES_PER_STEP
    bps = PAGES_PER_STEP * (page // BLK)
    rows = N_HEADS * t

    def page_spec(part, k):
        return pl.BlockSpec((1, page, 1, N_KV, HEAD_DIM),
                            lambda b, c, pt: (pt[b, c * PAGES_PER_STEP + k], 0, part, 0, 0))

    grid_spec = pltpu.PrefetchScalarGridSpec(
        num_scalar_prefetch=1,
        grid=(bd, n_steps),
        in_specs=[pl.BlockSpec((1, rows, HEAD_DIM), lambda b, c, pt: (b, 0, 0)),
                  pl.BlockSpec((1, 1, N_KV * t, bps), lambda b, c, pt: (b, c, 0, 0)),
                  pl.BlockSpec((1, 1, N_KV * t, bps), lambda b, c, pt: (b, n_steps, 0, 0))]
        + [page_spec(2, k) for k in range(PAGES_PER_STEP)]
        + [page_spec(3, k) for k in range(PAGES_PER_STEP)]
        + [pl.BlockSpec((1, NEW_PAD, 2 * KVW), lambda b, c, pt: (b, 0, 0)),
           pl.BlockSpec((1, rows, HEAD_DIM), lambda b, c, pt: (b, 0, 0)),
           pl.BlockSpec((1, rows, HEAD_DIM), lambda b, c, pt: (b, 0, 0)),
           pl.BlockSpec((t, HD), lambda b, c, pt: (b, 1)),
           pl.BlockSpec((t, HD), lambda b, c, pt: (b, 2)),
           pl.BlockSpec((t, HD), lambda b, c, pt: (b, 3)),
           pl.BlockSpec((t, HEAD_DIM), lambda b, c, pt: (b, 0)),
           pl.BlockSpec(e4.shape, lambda b, c, pt: (0, 0, 0))],
        out_specs=pl.BlockSpec((t, HD), lambda b, c, pt: (b, 0)),
        scratch_shapes=[pltpu.VMEM((rows, 1), F32), pltpu.VMEM((rows, 1), F32),
                        pltpu.VMEM((rows, HEAD_DIM), F32)],
    )
    return pl.pallas_call(
        functools.partial(_decode_sel_kernel, t=t, n_steps=n_steps),
        grid_spec=grid_spec,
        out_shape=jax.ShapeDtypeStruct((bd * t, HD), BF16),
        compiler_params=_cparams(2),
        name="decode_sel",
    )(page_table, qn, selc, selc, *([cache_kv] * (2 * PAGES_PER_STEP)), new_sel, ocmp, owin,
      u, u, u, gate, e4)


def _rope_tables(pos):
    half = HEAD_DIM // 2
    inv = ROPE_THETA ** (-jnp.arange(half, dtype=F32) / half)
    ang = pos.astype(F32)[:, None] * inv[None, :]
    c = jnp.cos(ang)
    s = jnp.sin(ang)
    return jnp.concatenate([c, c], axis=-1), jnp.concatenate([-s, s], axis=-1)


def _block_group_expand(nblocks):
    col = jnp.arange(nblocks * BLK * N_KV, dtype=jnp.int32)
    in_block = (col // N_KV) // BLK == jnp.arange(nblocks, dtype=jnp.int32)[:, None]
    in_group = col % N_KV == jnp.arange(N_KV, dtype=jnp.int32)[:, None]
    return (in_block[None, :, :] & in_group[:, None, :]).astype(BF16)


def kernel(x_prompt, x_sample, state_conv, cache_kv, state_win, page_table, g_a, w_in_a, conv_w,
           w_out_a, g_kv, w_kv, g_k, w_cmp, g_b, w_qz, g_q, w_out_b):
    batch, seq, d = x_prompt.shape
    bd, t, _ = x_sample.shape
    ch = conv_w.shape[-1]
    n_pool, page = cache_kv.shape[:2]
    n_pages = page_table.shape[1]
    past = n_pages * page
    wbuf = state_win.shape[1]
    assert seq % BLK == 0 and page % BLK == 0 and t <= BLK and n_pages % PAGES_PER_STEP == 0

    w_exp3 = jnp.broadcast_to(w_cmp.reshape(BLK, 2 * N_KV, 1), (BLK, 2 * N_KV, HEAD_DIM))
    w_exp = w_exp3.reshape(BLK, 2 * KVW)
    n_gate = N_BRANCH * N_HEADS
    w_qz_t = jnp.swapaxes(w_qz, 1, 2)
    w_gate_t = jnp.pad(w_qz_t[:, 4 * HD:4 * HD + n_gate, :], ((0, 0), (0, HEAD_DIM - n_gate), (0, 0)))

    xp = x_prompt.reshape(batch * seq, d)
    xs = x_sample.reshape(bd * t, d)
    prev0 = jnp.zeros((batch, CONV_W - 1, ch), F32)
    xn_p, xn_s = _rms_cast(xp, g_a[0]), _rms_cast(xs, g_a[0])
    conv_p, conv_s = [], []
    for layer in range(N_A):
        y_p, st_p, y_s, c_s = _conv_matmul(xn_p, xn_s, w_in_a, layer, prev0, state_conv[layer],
                                           conv_w[layer], seq, t)
        conv_p.append(st_p)
        conv_s.append(c_s.reshape(bd, t, ch)[:, t - (CONV_W - 1):])
        gains = [g_a[layer + 1]] if layer + 1 < N_A else [g_kv, g_b[0]]
        xp, xs, xns = _matmul_res_norm(y_p, y_s, w_out_a, layer, xp, xs, gains)
        xn_p, xn_s = xns[0]
    conv_p, conv_s = jnp.stack(conv_p), jnp.stack(conv_s)
    (xn_kv_p, xn_kv_s), (xn_q_p, xn_q_s) = xns

    kv_p, kv_s = _matmul(xn_kv_p, xn_kv_s, w_kv[None], 0, 6 * KVW)
    cos_p, sin_p = _rope_tables(jnp.arange(seq, dtype=jnp.int32))
    cos_s, sin_s = _rope_tables(past + jnp.arange(t, dtype=jnp.int32))
    rows_p, win_p, ks_p, vst_p, kw_p, vwt_p = _kv_post(kv_p, cos_p, sin_p, g_k, seq, True)
    rows_s, win_s = _kv_post(kv_s, cos_s, sin_s, g_k, t, False)
    nb_p = seq // BLK
    kvc_p = _compress_rows(rows_p, w_exp3).reshape(batch * nb_p, 2 * KVW)
    vct_p = kvc_p.reshape(batch, nb_p, 2, KVW)[:, :, 1].transpose(0, 2, 1)

    cache4 = cache_kv.reshape(n_pool, page, 4 * N_KV, HEAD_DIM)
    kvc_past = _compress_pages(cache4, page_table, w_exp3).reshape(bd, past // BLK, 2 * KVW)
    rows_s3 = rows_s.reshape(bd, t, 4 * KVW)
    rows_pad = jnp.pad(rows_s3, ((0, 0), (0, NEW_PAD - t), (0, 0)))
    kvc_new = _compress(rows_pad[:, :BLK, :2 * KVW].reshape(bd * BLK, 2 * KVW), w_exp)
    nb_past = past // BLK
    bps = PAGES_PER_STEP * (page // BLK)
    nbp = nb_past + bps
    kvc_all = jnp.concatenate(
        [kvc_past, kvc_new[:, None, :], jnp.zeros((bd, bps - 1, 2 * KVW), F32)], axis=1)
    new_sel = rows_pad[:, :, 2 * KVW:]
    new_win = jnp.pad(win_s.reshape(bd, t, 2 * KVW), ((0, 0), (0, NEW_PAD - t), (0, 0)))
    state_win2 = state_win.reshape(bd, wbuf, 2 * KVW)
    e4 = _block_group_expand(bps // 2 if PAGES_PER_STEP % 2 == 0 else bps)

    for lb in range(N_B):
        u_p, u_s = _matmul(xn_q_p, xn_q_s, w_qz_t, lb, 4 * HD, w_is_t=True)
        gate_p, gate_s = _matmul(xn_q_p, xn_q_s, w_gate_t, lb, HEAD_DIM, w_is_t=True)
        o_p = _prompt_attn(u_p, gate_p, kvc_p, vct_p, ks_p, vst_p, kw_p, vwt_p, cos_p, sin_p,
                           g_q[lb:lb + 1], batch, seq)
        qn, sel, ocmp, owin = _decode_front(u_s, kvc_all, state_win2, new_win, cos_s, sin_s,
                                            g_q[lb:lb + 1], bd, t, past)
        selc = sel.reshape(bd, N_KV * t, nbp // bps, bps).transpose(0, 2, 1, 3)
        o_s = _decode_sel(page_table, qn, selc, cache_kv, new_sel, ocmp, owin, u_s, gate_s, e4, bd, t)
        if lb + 1 < N_B:
            xp, xs, [(xn_q_p, xn_q_s)] = _matmul_res_norm(o_p, o_s, w_out_b, lb, xp, xs, [g_b[lb + 1]])
        else:
            xp, xs = _matmul(o_p, o_s, w_out_b, lb, d, res=xp, res_s=xs)

    wn = min(WINDOW, seq)
    y_prompt = xp.reshape(batch, seq, d)
    y_sample = xs.reshape(bd, t, d)
    kv_rows_prompt = rows_p.reshape(batch, seq, 4, N_KV, HEAD_DIM)
    kv_rows_sample = rows_s.reshape(bd, t, 4, N_KV, HEAD_DIM)
    win_prompt = win_p.reshape(batch, seq, 2, N_KV, HEAD_DIM)[:, seq - wn:]
    kvw = jnp.concatenate([state_win, win_s.reshape(bd, t, 2, N_KV, HEAD_DIM)], axis=1)
    win_sample = kvw[:, kvw.shape[1] - min(WINDOW, kvw.shape[1]):]

    return (y_prompt, y_sample, conv_p, conv_s, kv_rows_prompt, kv_rows_sample, win_prompt, win_sample)
```

```python
import functools

import jax
import jax.numpy as jnp
from jax import lax
from jax.experimental import pallas as pl
from jax.experimental.pallas import tpu as pltpu

N_A = 2
N_B = 2
CONV_W = 3
N_HEADS = 16
N_KV = 4
GRP = N_HEADS // N_KV
HEAD_DIM = 128
N_BRANCH = 3
BLK = 64
N_SEL = 16
WINDOW = 512
ROPE_THETA = 10000.0
EPS = 1e-6
NEG = -1e30
FORCE = 1e3
LOG2E = 1.4426950408889634

HD = N_HEADS * HEAD_DIM
KVW = N_KV * HEAD_DIM
NEW_PAD = 128
KCH = 128
SPAN_KEYS = 512
PAGES_PER_STEP = 16
VMEM_LIMIT = 56 * 1024 * 1024

F32 = jnp.float32
BF16 = jnp.bfloat16


def _cparams(n_axes):
    return pltpu.CompilerParams(
        dimension_semantics=("arbitrary",) * n_axes, vmem_limit_bytes=VMEM_LIMIT)


def _dot_nt(a, b):
    return lax.dot_general(a, b, (((1,), (1,)), ((), ())), preferred_element_type=F32)


def _dot(a, b):
    return jnp.dot(a, b, preferred_element_type=F32)


def _pick(n, prefs):
    for p in prefs:
        if n % p == 0:
            return p
    return n


def _rms_cast_kernel(x_ref, g_ref, o_ref):
    x = x_ref[...]
    y = x * lax.rsqrt(jnp.mean(x * x, axis=-1, keepdims=True) + EPS)
    o_ref[...] = (y * g_ref[...]).astype(BF16)


def _rms_cast(x, g):
    n, d = x.shape
    tm = _pick(n, (512, 256, 128, 64, 8))
    return pl.pallas_call(
        _rms_cast_kernel,
        grid=(n // tm,),
        in_specs=[pl.BlockSpec((tm, d), lambda i: (i, 0)),
                  pl.BlockSpec((1, d), lambda i: (0, 0))],
        out_specs=pl.BlockSpec((tm, d), lambda i: (i, 0)),
        out_shape=jax.ShapeDtypeStruct((n, d), BF16),
        compiler_params=_cparams(1),
        name="rms_cast",
    )(x, g.reshape(1, d))


def _main_tile(nt):
    return lambda i: jnp.minimum(i, nt - 1)


def _mm_kernel(*refs, has_res, w_is_t, nt):
    it = iter(refs)
    x_ref, xs_ref, w_ref = next(it), next(it), next(it)
    r_ref, rs_ref = (next(it), next(it)) if has_res else (None, None)
    o_ref, os_ref, wb_ref = next(it), next(it), next(it)
    i = pl.program_id(1)

    @pl.when(i == 0)
    def _():
        w = w_ref[...]
        wb_ref[...] = (jnp.transpose(w) if w_is_t else w).astype(BF16)

    def project(xr, rr, orf):
        acc = _dot(xr[...], wb_ref[...])
        orf[...] = acc if rr is None else rr[...] + acc

    def project_side_t():
        os_ref[...] = _dot_nt(w_ref[...].astype(BF16), xs_ref[...])

    pl.when(i < nt)(functools.partial(project, x_ref, r_ref, o_ref))
    pl.when(i == nt)(project_side_t if w_is_t else functools.partial(project, xs_ref, rs_ref, os_ref))


def _matmul(x, xs, w, layer, m_out, res=None, res_s=None, w_is_t=False):
    n, k = x.shape
    ns = xs.shape[0]
    tm = _pick(n, (1024, 512, 256, 128, 64, 8))
    tn = _pick(m_out, (1024, 512, 256, 128))
    nt = n // tm
    main = _main_tile(nt)
    if w_is_t:
        w_spec = pl.BlockSpec((None, tn, k), lambda j, i: (layer, j, 0))
    else:
        w_spec = pl.BlockSpec((None, k, tn), lambda j, i: (layer, 0, j))
    in_specs = [pl.BlockSpec((tm, k), lambda j, i: (main(i), 0)),
                pl.BlockSpec((ns, k), lambda j, i: (0, 0)), w_spec]
    args = [x, xs, w]
    if res is not None:
        in_specs += [pl.BlockSpec((tm, tn), lambda j, i: (main(i), j)),
                     pl.BlockSpec((ns, tn), lambda j, i: (0, j))]
        args += [res, res_s]
    return pl.pallas_call(
        functools.partial(_mm_kernel, has_res=res is not None, w_is_t=w_is_t, nt=nt),
        grid=(m_out // tn, nt + 1),
        in_specs=in_specs,
        out_specs=[pl.BlockSpec((tm, tn), lambda j, i: (main(i), j)),
                   pl.BlockSpec((tn, ns), lambda j, i: (j, 0)) if w_is_t
                   else pl.BlockSpec((ns, tn), lambda j, i: (0, j))],
        out_shape=[jax.ShapeDtypeStruct((n, m_out), F32),
                   jax.ShapeDtypeStruct((m_out, ns) if w_is_t else (ns, m_out), F32)],
        scratch_shapes=[pltpu.VMEM((k, tn), BF16)],
        compiler_params=_cparams(2),
        name="matmul",
    )(*args)


def _mm_res_norm_kernel(*refs, n_norm, nt):
    x_ref, xs_ref, w_ref, r_ref, rs_ref = refs[:5]
    g_refs = refs[5:5 + n_norm]
    o_ref, os_ref = refs[5 + n_norm:7 + n_norm]
    xn_refs = refs[7 + n_norm:7 + 3 * n_norm]
    wb_ref = refs[7 + 3 * n_norm]
    i = pl.program_id(0)

    @pl.when(i == 0)
    def _():
        wb_ref[...] = w_ref[...].astype(BF16)

    def project(xr, rr, orf, xnrs):
        out = rr[...] + _dot(xr[...], wb_ref[...])
        orf[...] = out
        y = out * lax.rsqrt(jnp.mean(out * out, axis=-1, keepdims=True) + EPS)
        for g_ref, xn_ref in zip(g_refs, xnrs):
            xn_ref[...] = (y * g_ref[...]).astype(BF16)

    pl.when(i < nt)(functools.partial(project, x_ref, r_ref, o_ref, xn_refs[0::2]))
    pl.when(i == nt)(functools.partial(project, xs_ref, rs_ref, os_ref, xn_refs[1::2]))


def _matmul_res_norm(x, xs, w, layer, res, res_s, gains):
    n, k = x.shape
    ns = xs.shape[0]
    d = res.shape[1]
    tm = _pick(n, (512, 256, 128, 64, 8))
    nt = n // tm
    main = _main_tile(nt)
    row = lambda i: (main(i), 0)
    side = lambda i: (0, 0)
    n_norm = len(gains)
    outs = pl.pallas_call(
        functools.partial(_mm_res_norm_kernel, n_norm=n_norm, nt=nt),
        grid=(nt + 1,),
        in_specs=[pl.BlockSpec((tm, k), row), pl.BlockSpec((ns, k), side),
                  pl.BlockSpec((None, k, d), lambda i: (layer, 0, 0), pipeline_mode=pl.Buffered(1)),
                  pl.BlockSpec((tm, d), row), pl.BlockSpec((ns, d), side)]
        + [pl.BlockSpec((1, d), side)] * n_norm,
        out_specs=[pl.BlockSpec((tm, d), row), pl.BlockSpec((ns, d), side)]
        + [pl.BlockSpec((tm, d), row), pl.BlockSpec((ns, d), side)] * n_norm,
        out_shape=[jax.ShapeDtypeStruct((n, d), F32), jax.ShapeDtypeStruct((ns, d), F32)]
        + [jax.ShapeDtypeStruct((n, d), BF16), jax.ShapeDtypeStruct((ns, d), BF16)] * n_norm,
        scratch_shapes=[pltpu.VMEM((k, d), BF16)],
        compiler_params=_cparams(1),
        name="matmul_res_norm",
    )(x, xs, w, res, res_s, *[g.reshape(1, d) for g in gains])
    return outs[0], outs[1], [(outs[2 + 2 * j], outs[3 + 2 * j]) for j in range(n_norm)]


def _conv_taps(c, c1, c2, w_ref):
    return w_ref[0:1, :] * c2 + w_ref[1:2, :] * c1 + w_ref[2:3, :] * c


def _conv_mm_kernel(x_ref, xs_ref, wb_ref, wc_ref, wh_ref, wz_ref, prev_ref, p1_ref, p2_ref, cw_ref,
                    y_ref, st_ref, ys_ref, cs_ref, wbf_ref, carry_ref, *, tps, nt, side_seq):
    i = pl.program_id(1)

    @pl.when(i == 0)
    def _():
        for k, wr in enumerate((wb_ref, wc_ref, wh_ref, wz_ref)):
            wbf_ref[k] = wr[...].astype(BF16)

    def parts(xr):
        x = xr[...]
        return (_dot(x, wbf_ref[0]), _dot(x, wbf_ref[1]) * _dot(x, wbf_ref[2]), _dot(x, wbf_ref[3]))

    @pl.when(i < nt)
    def _():
        b, c, z = parts(x_ref)
        tm = c.shape[0]

        @pl.when(i % tps == 0)
        def _():
            carry_ref[0:2, :] = prev_ref[0]

        p0 = carry_ref[0:1, :]
        p1 = carry_ref[1:2, :]
        row = lax.broadcasted_iota(jnp.int32, c.shape, 0)
        c1 = jnp.where(row == 0, p1, pltpu.roll(c, 1, 0))
        c2 = jnp.where(row == 0, p0, jnp.where(row == 1, p1, pltpu.roll(c, 2, 0)))
        y_ref[...] = (jax.nn.silu(z) * (b * _conv_taps(c, c1, c2, cw_ref))).astype(BF16)
        last = c[tm - 2:tm, :]
        carry_ref[0:2, :] = last
        st_ref[0] = last

    @pl.when(i == nt)
    def _():
        b, c, z = parts(xs_ref)
        pos = lax.broadcasted_iota(jnp.int32, c.shape, 0) & (side_seq - 1)
        c1 = jnp.where(pos == 0, p1_ref[...], pltpu.roll(c, 1, 0))
        c2 = jnp.where(pos < 2, p2_ref[...], pltpu.roll(c, 2, 0))
        ys_ref[...] = (jax.nn.silu(z) * (b * _conv_taps(c, c1, c2, cw_ref))).astype(BF16)
        cs_ref[...] = c


def _conv_matmul(xn, xs, w_in, layer, prev, prev_s, w_conv, seq, side_seq):
    n, k = xn.shape
    ns = xs.shape[0]
    ch = w_in.shape[2] // 4
    tm = _pick(seq, (1024, 512, 256, 128, 64, 8))
    tc = _pick(ch, (256, 128))
    tps = seq // tm
    nc = ch // tc
    nt = n // tm
    main = _main_tile(nt)
    assert side_seq >= CONV_W - 1 and side_seq & (side_seq - 1) == 0
    p2 = jnp.pad(prev_s, ((0, 0), (0, side_seq - 2), (0, 0))).reshape(ns, ch)
    p1 = jnp.pad(prev_s[:, 1:2], ((0, 0), (0, side_seq - 1), (0, 0))).reshape(ns, ch)

    def w_spec(part):
        return pl.BlockSpec((None, k, tc), lambda j, i: (layer, 0, part * nc + j))

    side = pl.BlockSpec((ns, tc), lambda j, i: (0, j))
    return pl.pallas_call(
        functools.partial(_conv_mm_kernel, tps=tps, nt=nt, side_seq=side_seq),
        grid=(nc, nt + 1),
        in_specs=[pl.BlockSpec((tm, k), lambda j, i: (main(i), 0)),
                  pl.BlockSpec((ns, k), lambda j, i: (0, 0))] + [w_spec(p) for p in range(4)]
        + [pl.BlockSpec((1, 2, tc), lambda j, i: (main(i) // tps, 0, j)), side, side,
           pl.BlockSpec((CONV_W, tc), lambda j, i: (0, j))],
        out_specs=[pl.BlockSpec((tm, tc), lambda j, i: (main(i), j)),
                   pl.BlockSpec((1, 2, tc), lambda j, i: (main(i) // tps, 0, j)), side, side],
        out_shape=[jax.ShapeDtypeStruct((n, ch), BF16),
                   jax.ShapeDtypeStruct((n // seq, CONV_W - 1, ch), F32),
                   jax.ShapeDtypeStruct((ns, ch), BF16),
                   jax.ShapeDtypeStruct((ns, ch), F32)],
        scratch_shapes=[pltpu.VMEM((4, k, tc), BF16), pltpu.VMEM((8, tc), F32)],
        compiler_params=_cparams(2),
        name="conv_matmul",
    )(xn, xs, w_in, w_in, w_in, w_in, prev, p1, p2, w_conv)


def _norm_rope(x, g, cos2, sin2):
    y = x * lax.rsqrt(jnp.mean(x * x, axis=-1, keepdims=True) + EPS) * g
    return y * cos2 + pltpu.roll(y, HEAD_DIM // 2, 1) * sin2


def _kv_post_kernel(kv_ref, cos_ref, sin_ref, gk_ref, rows_ref, win_ref, *attn_refs):
    cos2 = cos_ref[...]
    sin2 = sin_ref[...]
    tm = kv_ref.shape[0]
    for br in range(N_BRANCH):
        base = br * 2 * KVW
        out_ref, row0 = (rows_ref, br * 2 * N_KV) if br < 2 else (win_ref, 0)
        for g in range(N_KV):
            c0 = base + g * HEAD_DIM
            k = _norm_rope(kv_ref[:, c0:c0 + HEAD_DIM], gk_ref[br:br + 1, :], cos2, sin2)
            out_ref[:, row0 + g, :] = k
            out_ref[:, row0 + N_KV + g, :] = kv_ref[:, c0 + KVW:c0 + KVW + HEAD_DIM]
            if attn_refs and br >= 1:
                attn_refs[2 * (br - 1)][:, g * HEAD_DIM:(g + 1) * HEAD_DIM] = k.astype(BF16)
        if attn_refs and br >= 1:
            v = kv_ref[:, base + KVW:base + 2 * KVW]
            vt_ref = attn_refs[2 * (br - 1) + 1]
            for a in range(tm // KCH):
                vt_ref[a] = jnp.transpose(v[a * KCH:(a + 1) * KCH, :]).astype(BF16)


def _kv_post(kv, cos2, sin2, g_k, seq, for_attn):
    n = kv.shape[0]
    tm = _pick(seq, (256, 128, 64, 8))
    tps = seq // tm
    out_specs = [pl.BlockSpec((tm, 4 * N_KV, HEAD_DIM), lambda i: (i, 0, 0)),
                 pl.BlockSpec((tm, 2 * N_KV, HEAD_DIM), lambda i: (i, 0, 0))]
    out_shape = [jax.ShapeDtypeStruct((n, 4 * N_KV, HEAD_DIM), F32),
                 jax.ShapeDtypeStruct((n, 2 * N_KV, HEAD_DIM), F32)]
    if for_attn:
        assert tm % KCH == 0
        for _ in range(2):
            out_specs += [pl.BlockSpec((tm, KVW), lambda i: (i, 0)),
                          pl.BlockSpec((tm // KCH, KVW, KCH), lambda i: (i, 0, 0))]
            out_shape += [jax.ShapeDtypeStruct((n, KVW), BF16),
                          jax.ShapeDtypeStruct((n // KCH, KVW, KCH), BF16)]
    return pl.pallas_call(
        _kv_post_kernel,
        grid=(n // tm,),
        in_specs=[pl.BlockSpec((tm, 6 * KVW), lambda i: (i, 0)),
                  pl.BlockSpec((tm, HEAD_DIM), lambda i: (i % tps, 0)),
                  pl.BlockSpec((tm, HEAD_DIM), lambda i: (i % tps, 0)),
                  pl.BlockSpec((N_BRANCH, HEAD_DIM), lambda i: (0, 0))],
        out_specs=out_specs,
        out_shape=out_shape,
        compiler_params=_cparams(1),
        name="kv_post",
    )(kv, cos2, sin2, g_k)


def _compress_kernel(rows_ref, w_ref, o_ref):
    x = rows_ref[...]
    nb = x.shape[0] // BLK
    x = x.reshape(nb, BLK, x.shape[1]) * w_ref[...][None]
    o_ref[...] = jnp.sum(x, axis=1)


def _compress(rows, w_exp):
    n = rows.shape[0]
    nblk = n // BLK
    per = _pick(nblk, (8,))
    return pl.pallas_call(
        _compress_kernel,
        grid=(nblk // per,),
        in_specs=[pl.BlockSpec((per * BLK, 2 * KVW), lambda i: (i, 0)),
                  pl.BlockSpec((BLK, 2 * KVW), lambda i: (0, 0))],
        out_specs=pl.BlockSpec((per, 2 * KVW), lambda i: (i, 0)),
        out_shape=jax.ShapeDtypeStruct((nblk, 2 * KVW), F32),
        compiler_params=_cparams(1),
        name="compress",
    )(rows, w_exp)


def _block_sums(x, w):
    nb = x.shape[0] // BLK
    return jnp.sum(x.reshape(nb, BLK, 2 * N_KV, HEAD_DIM) * w[None], axis=1)


def _compress_rows_kernel(rows_ref, w_ref, o_ref):
    o_ref[...] = _block_sums(rows_ref[...], w_ref[...])


def _compress_rows(rows3, w_exp3):
    n = rows3.shape[0]
    nblk = n // BLK
    per = _pick(nblk, (8,))
    return pl.pallas_call(
        _compress_rows_kernel,
        grid=(nblk // per,),
        in_specs=[pl.BlockSpec((per * BLK, 2 * N_KV, HEAD_DIM), lambda i: (i, 0, 0)),
                  pl.BlockSpec((BLK, 2 * N_KV, HEAD_DIM), lambda i: (0, 0, 0))],
        out_specs=pl.BlockSpec((per, 2 * N_KV, HEAD_DIM), lambda i: (i, 0, 0)),
        out_shape=jax.ShapeDtypeStruct((nblk, 2 * N_KV, HEAD_DIM), F32),
        compiler_params=_cparams(1),
        name="compress_rows",
    )(rows3, w_exp3)


def _compress_pages_kernel(pt_ref, *refs):
    page_refs = refs[:PAGES_PER_STEP]
    w_ref = refs[PAGES_PER_STEP]
    o_ref = refs[PAGES_PER_STEP + 1]
    w = w_ref[...]
    for k, pr in enumerate(page_refs):
        nb = pr.shape[1] // BLK
        o_ref[0, k * nb:(k + 1) * nb] = _block_sums(pr[0], w)


def _compress_pages(cache4, page_table, w_exp3):
    bd, n_pages = page_table.shape
    page = cache4.shape[1]
    bpp = page // BLK
    steps = n_pages // PAGES_PER_STEP
    rows_out = PAGES_PER_STEP * bpp

    def page_spec(k):
        return pl.BlockSpec((1, page, 2 * N_KV, HEAD_DIM),
                            lambda b, c, pt: (pt[b, c * PAGES_PER_STEP + k], 0, 0, 0))

    grid_spec = pltpu.PrefetchScalarGridSpec(
        num_scalar_prefetch=1,
        grid=(bd, steps),
        in_specs=[page_spec(k) for k in range(PAGES_PER_STEP)]
        + [pl.BlockSpec((BLK, 2 * N_KV, HEAD_DIM), lambda b, c, pt: (0, 0, 0))],
        out_specs=pl.BlockSpec((1, rows_out, 2 * N_KV, HEAD_DIM), lambda b, c, pt: (b, c, 0, 0)),
    )
    return pl.pallas_call(
        _compress_pages_kernel,
        grid_spec=grid_spec,
        out_shape=jax.ShapeDtypeStruct((bd, n_pages * bpp, 2 * N_KV, HEAD_DIM), F32),
        compiler_params=_cparams(2),
        name="compress_pages",
    )(page_table, *([cache4] * PAGES_PER_STEP), w_exp3)


def _softmax_parts(s, mask):
    sm = jnp.where(mask, s, NEG)
    m = jnp.max(sm, axis=-1, keepdims=True)
    p = jnp.where(mask, jnp.exp(sm - m), 0.0)
    return p, m, jnp.sum(p, axis=-1, keepdims=True)


def _safe_inv(l):
    return jnp.where(l > 0.0, 1.0 / l, 0.0)


def _topk_mask(score, n_f, k):
    sel = jnp.zeros(score.shape, F32)
    big = jnp.float32(score.shape[-1] + 1)
    for _ in range(k):
        mx = jnp.max(score, axis=-1, keepdims=True)
        idx = jnp.min(jnp.where(score == mx, n_f, big), axis=-1, keepdims=True)
        hit = n_f == idx
        sel = jnp.where(hit, 1.0, sel)
        score = jnp.where(hit, -jnp.inf, score)
    return sel


def _select_blocks(psum, cur, n_i):
    forced = ((n_i == 0) | (n_i == cur) | (n_i == cur - 1)).astype(F32)
    valid = n_i <= cur
    score = jnp.where(valid, psum + FORCE * forced, NEG)
    sel = _topk_mask(score, n_i.astype(F32), min(N_SEL, psum.shape[-1]))
    return jnp.where(valid, sel, 0.0)


def _q_heads(uq_ref, g, gq, cos2, sin2, scale=None):
    hs = []
    for r in range(GRP):
        c0 = (g * GRP + r) * HEAD_DIM
        h = _norm_rope(uq_ref[:, c0:c0 + HEAD_DIM], gq, cos2, sin2)
        hs.append(h if scale is None else h * scale)
    return jnp.concatenate(hs, axis=0).astype(BF16)


def _tile_rows(x, reps):
    return jnp.concatenate([x] * reps, axis=0)


def _gate_and_store(o_ref, g, outs, z_refs, gate, t):
    for r in range(GRP):
        h = g * GRP + r
        acc = None
        for br in range(N_BRANCH):
            zg = jax.nn.silu(z_refs[br][:, h * HEAD_DIM:(h + 1) * HEAD_DIM]) \
                * gate[:, br * N_HEADS + h:br * N_HEADS + h + 1]
            term = outs[br][r * t:(r + 1) * t, :] * zg
            acc = term if acc is None else acc + term
        o_ref[:, h * HEAD_DIM:(h + 1) * HEAD_DIM] = acc.astype(o_ref.dtype)


def _rank_select_t(score_t, n_col, k):
    rank = jnp.zeros(score_t.shape, F32)
    for m in range(score_t.shape[0]):
        row = score_t[m:m + 1, :]
        tie = (n_col > m).astype(F32)
        rank = rank + jnp.where(row > score_t, 1.0, 0.0) + jnp.where(row == score_t, tie, 0.0)
    return jnp.where(rank < k, 1.0, 0.0)


def _prompt_attn_kernel(uq_ref, z0_ref, z1_ref, z2_ref, gate_ref, kvc_ref, vct_ref, ks_ref, vst_ref,
                        kw_ref, vwt_ref, cos_ref, sin_ref, gq_ref, o_ref,
                        q_scr, s_scr, bias_scr, mrun_scr, lrun_scr, ot_scr, ocmp_scr, *, seq, spk):
    i = pl.program_id(1)
    t = BLK
    lanes = GRP * t
    nb = seq // BLK
    nch = seq // KCH
    nwc = min(WINDOW // KCH + 1, nch)
    bpc = KCH // BLK
    k_in_blk = lax.broadcasted_iota(jnp.int32, (BLK, 1), 0)
    cos2 = cos_ref[...]
    sin2 = sin_ref[...]
    gq = gq_ref[...]
    gate = jax.nn.sigmoid(gate_ref[...])
    qpos_l = i * t + (lax.broadcasted_iota(jnp.int32, (1, lanes), 1) & (t - 1))
    qpos_h = qpos_l[:, 0:2 * t]
    cur_h = qpos_h // BLK
    n_col = lax.broadcasted_iota(jnp.int32, (nb, 1), 0)
    cmask_t = (n_col * BLK + (BLK - 1)) <= qpos_l
    forced_t = ((n_col == 0) | (n_col == cur_h) | (n_col == cur_h - 1)).astype(F32)
    valid_t = n_col <= cur_h
    k_in_chunk = lax.broadcasted_iota(jnp.int32, (KCH, 1), 0)

    def group_max(x):
        return jnp.max(x.reshape(KCH // 8, 8, lanes), axis=0)

    def group_sum(x):
        return jnp.sum(x.reshape(KCH // 8, 8, lanes), axis=0)

    def finish(lrun, ot):
        l = jnp.sum(lrun, axis=0, keepdims=True)
        return jnp.transpose(ot * _safe_inv(l))

    for g in range(N_KV):
        q = _q_heads(uq_ref, g, gq, cos2, sin2, LOG2E * HEAD_DIM ** -0.5)
        ksl = slice(g * HEAD_DIM, (g + 1) * HEAD_DIM)
        kc = kvc_ref[0, :, ksl].astype(BF16)

        st = jnp.where(cmask_t, _dot_nt(kc, q), NEG)
        pt = jnp.where(cmask_t, jnp.exp2(st - jnp.max(st, axis=0, keepdims=True)), 0.0)
        pt = pt * _safe_inv(jnp.sum(pt, axis=0, keepdims=True))
        ocmp_scr[g] = _dot(vct_ref[0, ksl, :].astype(BF16), pt.astype(BF16))

        half = pt[:, 0:2 * t] + pt[:, 2 * t:4 * t]
        psum_t = half + pltpu.roll(half, t, 1)
        score_t = jnp.where(valid_t, psum_t + FORCE * forced_t, NEG)
        sel_t = jnp.where(valid_t, _rank_select_t(score_t, n_col, min(N_SEL, nb)), 0.0)
        bias_t = (sel_t - 1.0) * (-NEG)
        bias_scr[g] = jnp.concatenate([bias_t, bias_t], axis=1)

        q_scr[g] = q
        mrun_scr[g] = jnp.full((8, lanes), NEG, F32)
        lrun_scr[g] = jnp.zeros((8, lanes), F32)
        ot_scr[g] = jnp.zeros((HEAD_DIM, lanes), F32)

    def sel_pass1(sc, diagonal):
        for g in range(N_KV):
            ksl = slice(g * HEAD_DIM, (g + 1) * HEAD_DIM)
            st = _dot_nt(ks_ref[0, sc * spk:(sc + 1) * spk, ksl], q_scr[g])
            mx = mrun_scr[g]
            for j in range(spk // BLK):
                n = sc * (spk // BLK) + j
                blk = st[j * BLK:(j + 1) * BLK] + bias_scr[g, n:n + 1, :]
                if diagonal:
                    blk = jnp.where(n * BLK + k_in_blk <= qpos_l, blk, NEG)
                s_scr[g, n // bpc, (n % bpc) * BLK:(n % bpc + 1) * BLK, :] = blk
                mx = jnp.maximum(mx, jnp.max(blk.reshape(BLK // 8, 8, lanes), axis=0))
            mrun_scr[g] = mx

    def sel_pass2(sc):
        for g in range(N_KV):
            ksl = slice(g * HEAD_DIM, (g + 1) * HEAD_DIM)
            m = jnp.max(mrun_scr[g], axis=0, keepdims=True)
            lrun = lrun_scr[g]
            ot = ot_scr[g]
            for c in range(sc * (spk // KCH), (sc + 1) * (spk // KCH)):
                p = jnp.exp2(s_scr[g, c] - m)
                lrun = lrun + group_sum(p)
                ot = ot + _dot(vst_ref[c, ksl, :], p.astype(BF16))
            lrun_scr[g] = lrun
            ot_scr[g] = ot

    for sc in range(seq // spk):
        pl.when((sc + 1) * spk <= i * t)(functools.partial(sel_pass1, sc, False))
        pl.when((sc * spk <= i * t) & (i * t < (sc + 1) * spk))(functools.partial(sel_pass1, sc, True))
    for sc in range(seq // spk):
        pl.when(sc * spk <= i * t)(functools.partial(sel_pass2, sc))

    cs = jnp.minimum(jnp.maximum(i - WINDOW // BLK, 0) // bpc, nch - nwc)
    for g in range(N_KV):
        ksl = slice(g * HEAD_DIM, (g + 1) * HEAD_DIM)
        q = q_scr[g]
        o_sel = finish(lrun_scr[g], ot_scr[g])
        mrun = jnp.full((8, lanes), NEG, F32)
        for j in range(nwc):
            c = cs + j
            k = kw_ref[0, pl.ds(pl.multiple_of(c * KCH, KCH), KCH), ksl]
            dpos = qpos_l - (c * KCH + k_in_chunk)
            st = jnp.where((dpos >= 0) & (dpos < WINDOW), _dot_nt(k, q), NEG)
            s_scr[g, j] = st
            mrun = jnp.maximum(mrun, group_max(st))
        m = jnp.max(mrun, axis=0, keepdims=True)
        lrun = jnp.zeros((8, lanes), F32)
        ot = jnp.zeros((HEAD_DIM, lanes), F32)
        for j in range(nwc):
            p = jnp.exp2(s_scr[g, j] - m)
            lrun = lrun + group_sum(p)
            ot = ot + _dot(vwt_ref[cs + j, ksl, :], p.astype(BF16))
        o_win = finish(lrun, ot)

        _gate_and_store(o_ref, g, (jnp.transpose(ocmp_scr[g]), o_sel, o_win),
                        (z0_ref, z1_ref, z2_ref), gate, t)


def _prompt_attn(u, gate, kvc, vct, ks, vst, kw, vwt, cos2, sin2, gq, batch, seq):
    n = u.shape[0]
    t = BLK
    spk = _pick(seq, (SPAN_KEYS, KCH))
    assert GRP * t == 2 * KCH and seq % KCH == 0 and spk % KCH == 0
    nqb = seq // t
    nb = seq // BLK
    nch = seq // KCH
    lanes = GRP * t
    row = lambda b, i: (b * nqb + i, 0)
    keys = pl.BlockSpec((1, seq, KVW), lambda b, i: (b, 0, 0))
    vals_t = pl.BlockSpec((nch, KVW, KCH), lambda b, i: (b, 0, 0))
    return pl.pallas_call(
        functools.partial(_prompt_attn_kernel, seq=seq, spk=spk),
        grid=(batch, nqb),
        in_specs=[pl.BlockSpec((t, HD), row),
                  pl.BlockSpec((t, HD), lambda b, i: (b * nqb + i, 1)),
                  pl.BlockSpec((t, HD), lambda b, i: (b * nqb + i, 2)),
                  pl.BlockSpec((t, HD), lambda b, i: (b * nqb + i, 3)),
                  pl.BlockSpec((t, HEAD_DIM), row),
                  pl.BlockSpec((1, nb, 2 * KVW), lambda b, i: (b, 0, 0)),
                  pl.BlockSpec((1, KVW, nb), lambda b, i: (b, 0, 0)),
                  keys, vals_t, keys, vals_t,
                  pl.BlockSpec((t, HEAD_DIM), lambda b, i: (i, 0)),
                  pl.BlockSpec((t, HEAD_DIM), lambda b, i: (i, 0)),
                  pl.BlockSpec((1, HEAD_DIM), lambda b, i: (0, 0))],
        out_specs=pl.BlockSpec((t, HD), row),
        out_shape=jax.ShapeDtypeStruct((n, HD), BF16),
        scratch_shapes=[pltpu.VMEM((N_KV, lanes, HEAD_DIM), BF16),
                        pltpu.VMEM((N_KV, nch, KCH, lanes), F32),
                        pltpu.VMEM((N_KV, nb, lanes), F32),
                        pltpu.VMEM((N_KV, 8, lanes), F32),
                        pltpu.VMEM((N_KV, 8, lanes), F32),
                        pltpu.VMEM((N_KV, HEAD_DIM, lanes), F32),
                        pltpu.VMEM((N_KV, HEAD_DIM, lanes), F32)],
        compiler_params=_cparams(2),
        name="prompt_attn",
    )(u, u, u, u, gate, kvc.reshape(batch, nb, 2 * KVW), vct, ks.reshape(batch, seq, KVW), vst,
      kw.reshape(batch, seq, KVW), vwt, cos2, sin2, gq)


def _decode_front_kernel(uq_ref, kvc_ref, swin_ref, nwin_ref, cos_ref, sin_ref, gq_ref,
                         qn_ref, sel_ref, ocmp_ref, owin_ref, *, past, t):
    scale = HEAD_DIM ** -0.5
    cos2 = cos_ref[...]
    sin2 = sin_ref[...]
    gq = gq_ref[...]
    nbp = kvc_ref.shape[1]
    wbuf = swin_ref.shape[1]
    qpos = past + lax.broadcasted_iota(jnp.int32, (t, 1), 0)
    qpos_r = _tile_rows(qpos, GRP)
    cur = qpos // BLK
    n_i = lax.broadcasted_iota(jnp.int32, (1, nbp), 1)
    cmask = (n_i * BLK + (BLK - 1)) <= qpos_r
    kwpos_old = (past - wbuf) + lax.broadcasted_iota(jnp.int32, (1, wbuf), 1)
    kwpos_new = past + lax.broadcasted_iota(jnp.int32, (1, NEW_PAD), 1)
    d_old = qpos_r - kwpos_old
    d_new = qpos_r - kwpos_new
    m_old = (d_old >= 0) & (d_old < WINDOW) & (kwpos_old >= 0)
    m_new = (d_new >= 0) & (d_new < WINDOW) & (kwpos_new < past + t)

    psums = []
    for g in range(N_KV):
        q = _q_heads(uq_ref, g, gq, cos2, sin2)
        rows = slice(g * GRP * t, (g + 1) * GRP * t)
        ksl = slice(g * HEAD_DIM, (g + 1) * HEAD_DIM)
        vsl = slice(KVW + g * HEAD_DIM, KVW + (g + 1) * HEAD_DIM)
        qn_ref[0, rows, :] = q

        s = _dot_nt(q, kvc_ref[0, :, ksl].astype(BF16)) * scale
        p, _, l = _softmax_parts(s, cmask)
        p = p * _safe_inv(l)
        ocmp_ref[0, rows, :] = _dot(p.astype(BF16), kvc_ref[0, :, vsl].astype(BF16))

        psum = p[0:t]
        for r in range(1, GRP):
            psum = psum + p[r * t:(r + 1) * t]
        psums.append(psum)

        s_old = _dot_nt(q, swin_ref[0, :, ksl].astype(BF16)) * scale
        s_new = _dot_nt(q, nwin_ref[0, :, ksl].astype(BF16)) * scale
        mx = jnp.maximum(jnp.max(jnp.where(m_old, s_old, NEG), axis=-1, keepdims=True),
                         jnp.max(jnp.where(m_new, s_new, NEG), axis=-1, keepdims=True))
        p_old = jnp.where(m_old, jnp.exp(jnp.where(m_old, s_old, NEG) - mx), 0.0)
        p_new = jnp.where(m_new, jnp.exp(jnp.where(m_new, s_new, NEG) - mx), 0.0)
        l = jnp.sum(p_old, axis=-1, keepdims=True) + jnp.sum(p_new, axis=-1, keepdims=True)
        o = _dot(p_old.astype(BF16), swin_ref[0, :, vsl].astype(BF16)) \
            + _dot(p_new.astype(BF16), nwin_ref[0, :, vsl].astype(BF16))
        owin_ref[0, rows, :] = o * _safe_inv(l)

    sel_ref[0] = _select_blocks(jnp.concatenate(psums, axis=0), _tile_rows(cur, N_KV), n_i)


def _decode_front(u, kvc_all, state_win2, new_win, cos2, sin2, gq, bd, t, past):
    nbp = kvc_all.shape[1]
    wbuf = state_win2.shape[1]
    rows = N_HEADS * t
    return pl.pallas_call(
        functools.partial(_decode_front_kernel, past=past, t=t),
        grid=(bd,),
        in_specs=[pl.BlockSpec((t, HD), lambda b: (b, 0)),
                  pl.BlockSpec((1, nbp, 2 * KVW), lambda b: (b, 0, 0)),
                  pl.BlockSpec((1, wbuf, 2 * KVW), lambda b: (b, 0, 0)),
                  pl.BlockSpec((1, NEW_PAD, 2 * KVW), lambda b: (b, 0, 0)),
                  pl.BlockSpec((t, HEAD_DIM), lambda b: (0, 0)),
                  pl.BlockSpec((t, HEAD_DIM), lambda b: (0, 0)),
                  pl.BlockSpec((1, HEAD_DIM), lambda b: (0, 0))],
        out_specs=[pl.BlockSpec((1, rows, HEAD_DIM), lambda b: (b, 0, 0)),
                   pl.BlockSpec((1, N_KV * t, nbp), lambda b: (b, 0, 0)),
                   pl.BlockSpec((1, rows, HEAD_DIM), lambda b: (b, 0, 0)),
                   pl.BlockSpec((1, rows, HEAD_DIM), lambda b: (b, 0, 0))],
        out_shape=[jax.ShapeDtypeStruct((bd, rows, HEAD_DIM), BF16),
                   jax.ShapeDtypeStruct((bd, N_KV * t, nbp), F32),
                   jax.ShapeDtypeStruct((bd, rows, HEAD_DIM), F32),
                   jax.ShapeDtypeStruct((bd, rows, HEAD_DIM), F32)],
        compiler_params=_cparams(1),
        name="decode_front",
    )(u, kvc_all, state_win2, new_win, cos2, sin2, gq)


def _values_product(vx, pb):
    return _dot(pb, vx)


def _decode_sel_kernel(pt_ref, *refs, t, n_steps):
    del pt_ref
    np_ = PAGES_PER_STEP
    (qn_ref, selc_ref, seln_ref) = refs[:3]
    k_refs = refs[3:3 + np_]
    v_refs = refs[3 + np_:3 + 2 * np_]
    (new_ref, ocmp_ref, owin_ref, z0_ref, z1_ref, z2_ref, gate_ref, e_ref,
     o_ref, m_ref, l_ref, acc_ref) = refs[3 + 2 * np_:]
    c = pl.program_id(1)
    scale = HEAD_DIM ** -0.5
    gt = GRP * t

    @pl.when(c == 0)
    def _():
        m_ref[...] = jnp.full(m_ref.shape, NEG, F32)
        l_ref[...] = jnp.zeros(l_ref.shape, F32)
        acc_ref[...] = jnp.zeros(acc_ref.shape, F32)

    def softmax_part(s, mask, pv_of):
        sm = jnp.where(mask, s, NEG)
        m = jnp.max(sm, axis=-1, keepdims=True)
        p = jnp.where(mask, jnp.exp(sm - m), 0.0)
        return m, jnp.sum(p, axis=-1, keepdims=True), pv_of(p.astype(BF16))

    def merge(parts):
        m_old = m_ref[...]
        m_new = m_old
        for m, _, _ in parts:
            m_new = jnp.maximum(m_new, m)
        alpha = jnp.exp(m_old - m_new)
        l = alpha * l_ref[...]
        acc = alpha * acc_ref[...]
        for m, lp, pv in parts:
            w = jnp.exp(m - m_new)
            l = l + w * lp
            acc = acc + w * pv
        l_ref[...] = l
        acc_ref[...] = acc
        m_ref[...] = m_new

    def cache_rows(page_refs):
        rows = [pr[0, :, 0].reshape(pr.shape[1] * N_KV, HEAD_DIM) for pr in page_refs]
        return jnp.concatenate(rows, axis=0).astype(BF16)

    halves = 2 if np_ % 2 == 0 else 1
    pp = np_ // halves
    bph = e_ref.shape[1]
    parts = []
    for h in range(halves):
        vx = cache_rows(v_refs[h * pp:(h + 1) * pp])
        s = _dot_nt(qn_ref[0], cache_rows(k_refs[h * pp:(h + 1) * pp])) * scale
        mask = jnp.concatenate(
            [_dot(selc_ref[0, 0, g * t:(g + 1) * t, h * bph:(h + 1) * bph].astype(BF16), e_ref[g])
             for g in range(N_KV) for _ in range(GRP)], axis=0) > 0.5
        parts.append(softmax_part(s, mask, functools.partial(_values_product, vx)))
    merge(parts)

    @pl.when(c == n_steps - 1)
    def _():
        gate = jax.nn.sigmoid(gate_ref[...])
        tq = _tile_rows(lax.broadcasted_iota(jnp.int32, (t, 1), 0), N_HEADS)
        causal = lax.broadcasted_iota(jnp.int32, (1, NEW_PAD), 1) <= tq
        sel_new = jnp.concatenate([seln_ref[0, 0, g * t:(g + 1) * t, 0:1]
                                   for g in range(N_KV) for _ in range(GRP)], axis=0) > 0.5
        s_new = jnp.concatenate(
            [_dot_nt(qn_ref[0, g * gt:(g + 1) * gt, :],
                     new_ref[0, :, g * HEAD_DIM:(g + 1) * HEAD_DIM].astype(BF16))
             for g in range(N_KV)], axis=0) * scale

        def pv_new(pb):
            return jnp.concatenate(
                [_dot(pb[g * gt:(g + 1) * gt],
                      new_ref[0, :, KVW + g * HEAD_DIM:KVW + (g + 1) * HEAD_DIM].astype(BF16))
                 for g in range(N_KV)], axis=0)

        merge([softmax_part(s_new, sel_new & causal, pv_new)])
        o_sel = acc_ref[...] * _safe_inv(l_ref[...])
        for g in range(N_KV):
            rows = slice(g * gt, (g + 1) * gt)
            _gate_and_store(o_ref, g, (ocmp_ref[0, rows, :], o_sel[rows], owin_ref[0, rows, :]),
                            (z0_ref, z1_ref, z2_ref), gate, t)


def _decode_sel(page_table, qn, selc, cache_kv, new_sel, ocmp, owin, u, gate, e4, bd, t):
    n_pages = page_table.shape[1]
    page = cache_kv.shape[1]
    n_steps = n_pages // PAGES_PER_STEP
    bps = PAGES_PER_STEP * (page // BLK)
    rows = N_HEADS * t

    def page_spec(part, k):
        return pl.BlockSpec((1, page, 1, N_KV, HEAD_DIM),
                            lambda b, c, pt: (pt[b, c * PAGES_PER_STEP + k], 0, part, 0, 0))

    grid_spec = pltpu.PrefetchScalarGridSpec(
        num_scalar_prefetch=1,
        grid=(bd, n_steps),
        in_specs=[pl.BlockSpec((1, rows, HEAD_DIM), lambda b, c, pt: (b, 0, 0)),
                  pl.BlockSpec((1, 1, N_KV * t, bps), lambda b, c, pt: (b, c, 0, 0)),
                  pl.BlockSpec((1, 1, N_KV * t, bps), lambda b, c, pt: (b, n_steps, 0, 0))]
        + [page_spec(2, k) for k in range(PAGES_PER_STEP)]
        + [page_spec(3, k) for k in range(PAGES_PER_STEP)]
        + [pl.BlockSpec((1, NEW_PAD, 2 * KVW), lambda b, c, pt: (b, 0, 0)),
           pl.BlockSpec((1, rows, HEAD_DIM), lambda b, c, pt: (b, 0, 0)),
           pl.BlockSpec((1, rows, HEAD_DIM), lambda b, c, pt: (b, 0, 0)),
           pl.BlockSpec((t, HD), lambda b, c, pt: (b, 1)),
           pl.BlockSpec((t, HD), lambda b, c, pt: (b, 2)),
           pl.BlockSpec((t, HD), lambda b, c, pt: (b, 3)),
           pl.BlockSpec((t, HEAD_DIM), lambda b, c, pt: (b, 0)),
           pl.BlockSpec(e4.shape, lambda b, c, pt: (0, 0, 0))],
        out_specs=pl.BlockSpec((t, HD), lambda b, c, pt: (b, 0)),
        scratch_shapes=[pltpu.VMEM((rows, 1), F32), pltpu.VMEM((rows, 1), F32),
                        pltpu.VMEM((rows, HEAD_DIM), F32)],
    )
    return pl.pallas_call(
        functools.partial(_decode_sel_kernel, t=t, n_steps=n_steps),
        grid_spec=grid_spec,
        out_shape=jax.ShapeDtypeStruct((bd * t, HD), BF16),
        compiler_params=_cparams(2),
        name="decode_sel",
    )(page_table, qn, selc, selc, *([cache_kv] * (2 * PAGES_PER_STEP)), new_sel, ocmp, owin,
      u, u, u, gate, e4)


def _rope_tables(pos):
    half = HEAD_DIM // 2
    inv = ROPE_THETA ** (-jnp.arange(half, dtype=F32) / half)
    ang = pos.astype(F32)[:, None] * inv[None, :]
    c = jnp.cos(ang)
    s = jnp.sin(ang)
    return jnp.concatenate([c, c], axis=-1), jnp.concatenate([-s, s], axis=-1)


def _block_group_expand(nblocks):
    col = jnp.arange(nblocks * BLK * N_KV, dtype=jnp.int32)
    in_block = (col // N_KV) // BLK == jnp.arange(nblocks, dtype=jnp.int32)[:, None]
    in_group = col % N_KV == jnp.arange(N_KV, dtype=jnp.int32)[:, None]
    return (in_block[None, :, :] & in_group[:, None, :]).astype(BF16)


def kernel(x_prompt, x_sample, state_conv, cache_kv, state_win, page_table, g_a, w_in_a, conv_w,
           w_out_a, g_kv, w_kv, g_k, w_cmp, g_b, w_qz, g_q, w_out_b):
    batch, seq, d = x_prompt.shape
    bd, t, _ = x_sample.shape
    ch = conv_w.shape[-1]
    n_pool, page = cache_kv.shape[:2]
    n_pages = page_table.shape[1]
    past = n_pages * page
    wbuf = state_win.shape[1]
    assert seq % BLK == 0 and page % BLK == 0 and t <= BLK and n_pages % PAGES_PER_STEP == 0

    w_exp3 = jnp.broadcast_to(w_cmp.reshape(BLK, 2 * N_KV, 1), (BLK, 2 * N_KV, HEAD_DIM))
    w_exp = w_exp3.reshape(BLK, 2 * KVW)
    n_gate = N_BRANCH * N_HEADS
    w_qz_t = jnp.swapaxes(w_qz, 1, 2)
    w_gate_t = jnp.pad(w_qz_t[:, 4 * HD:4 * HD + n_gate, :], ((0, 0), (0, HEAD_DIM - n_gate), (0, 0)))

    xp = x_prompt.reshape(batch * seq, d)
    xs = x_sample.reshape(bd * t, d)
    prev0 = jnp.zeros((batch, CONV_W - 1, ch), F32)
    xn_p, xn_s = _rms_cast(xp, g_a[0]), _rms_cast(xs, g_a[0])
    conv_p, conv_s = [], []
    for layer in range(N_A):
        y_p, st_p, y_s, c_s = _conv_matmul(xn_p, xn_s, w_in_a, layer, prev0, state_conv[layer],
                                           conv_w[layer], seq, t)
        conv_p.append(st_p)
        conv_s.append(c_s.reshape(bd, t, ch)[:, t - (CONV_W - 1):])
        gains = [g_a[layer + 1]] if layer + 1 < N_A else [g_kv, g_b[0]]
        xp, xs, xns = _matmul_res_norm(y_p, y_s, w_out_a, layer, xp, xs, gains)
        xn_p, xn_s = xns[0]
    conv_p, conv_s = jnp.stack(conv_p), jnp.stack(conv_s)
    (xn_kv_p, xn_kv_s), (xn_q_p, xn_q_s) = xns

    kv_p, kv_s = _matmul(xn_kv_p, xn_kv_s, w_kv[None], 0, 6 * KVW)
    cos_p, sin_p = _rope_tables(jnp.arange(seq, dtype=jnp.int32))
    cos_s, sin_s = _rope_tables(past + jnp.arange(t, dtype=jnp.int32))
    rows_p, win_p, ks_p, vst_p, kw_p, vwt_p = _kv_post(kv_p, cos_p, sin_p, g_k, seq, True)
    rows_s, win_s = _kv_post(kv_s, cos_s, sin_s, g_k, t, False)
    nb_p = seq // BLK
    kvc_p = _compress_rows(rows_p, w_exp3).reshape(batch * nb_p, 2 * KVW)
    vct_p = kvc_p.reshape(batch, nb_p, 2, KVW)[:, :, 1].transpose(0, 2, 1)

    cache4 = cache_kv.reshape(n_pool, page, 4 * N_KV, HEAD_DIM)
    kvc_past = _compress_pages(cache4, page_table, w_exp3).reshape(bd, past // BLK, 2 * KVW)
    rows_s3 = rows_s.reshape(bd, t, 4 * KVW)
    rows_pad = jnp.pad(rows_s3, ((0, 0), (0, NEW_PAD - t), (0, 0)))
    kvc_new = _compress(rows_pad[:, :BLK, :2 * KVW].reshape(bd * BLK, 2 * KVW), w_exp)
    nb_past = past // BLK
    bps = PAGES_PER_STEP * (page // BLK)
    nbp = nb_past + bps
    kvc_all = jnp.concatenate(
        [kvc_past, kvc_new[:, None, :], jnp.zeros((bd, bps - 1, 2 * KVW), F32)], axis=1)
    new_sel = rows_pad[:, :, 2 * KVW:]
    new_win = jnp.pad(win_s.reshape(bd, t, 2 * KVW), ((0, 0), (0, NEW_PAD - t), (0, 0)))
    state_win2 = state_win.reshape(bd, wbuf, 2 * KVW)
    e4 = _block_group_expand(bps // 2 if PAGES_PER_STEP % 2 == 0 else bps)

    for lb in range(N_B):
        u_p, u_s_t = _matmul(xn_q_p, xn_q_s, w_qz_t, lb, 4 * HD, w_is_t=True)
        gate_p, gate_s_t = _matmul(xn_q_p, xn_q_s, w_gate_t, lb, HEAD_DIM, w_is_t=True)
        u_s, gate_s = u_s_t.T, gate_s_t.T
        o_p = _prompt_attn(u_p, gate_p, kvc_p, vct_p, ks_p, vst_p, kw_p, vwt_p, cos_p, sin_p,
                           g_q[lb:lb + 1], batch, seq)
        qn, sel, ocmp, owin = _decode_front(u_s, kvc_all, state_win2, new_win, cos_s, sin_s,
                                            g_q[lb:lb + 1], bd, t, past)
        selc = sel.reshape(bd, N_KV * t, nbp // bps, bps).transpose(0, 2, 1, 3)
        o_s = _decode_sel(page_table, qn, selc, cache_kv, new_sel, ocmp, owin, u_s, gate_s, e4, bd, t)
        if lb + 1 < N_B:
            xp, xs, [(xn_q_p, xn_q_s)] = _matmul_res_norm(o_p, o_s, w_out_b, lb, xp, xs, [g_b[lb + 1]])
        else:
            xp, xs = _matmul(o_p, o_s, w_out_b, lb, d, res=xp, res_s=xs)

    wn = min(WINDOW, seq)
    y_prompt = xp.reshape(batch, seq, d)
    y_sample = xs.reshape(bd, t, d)
    kv_rows_prompt = rows_p.reshape(batch, seq, 4, N_KV, HEAD_DIM)
    kv_rows_sample = rows_s.reshape(bd, t, 4, N_KV, HEAD_DIM)
    win_prompt = win_p.reshape(batch, seq, 2, N_KV, HEAD_DIM)[:, seq - wn:]
    kvw = jnp.concatenate([state_win, win_s.reshape(bd, t, 2, N_KV, HEAD_DIM)], axis=1)
    win_sample = kvw[:, kvw.shape[1] - min(WINDOW, kvw.shape[1]):]

    return (y_prompt, y_sample, conv_p, conv_s, kv_rows_prompt, kv_rows_sample, win_prompt, win_sample)
```

```python
import functools

import jax
import jax.numpy as jnp
from jax import lax
from jax.experimental import pallas as pl
from jax.experimental.pallas import tpu as pltpu

N_A = 2
N_B = 2
CONV_W = 3
N_HEADS = 16
N_KV = 4
GRP = N_HEADS // N_KV
HEAD_DIM = 128
N_BRANCH = 3
BLK = 64
N_SEL = 16
WINDOW = 512
ROPE_THETA = 10000.0
EPS = 1e-6
NEG = -1e30
FORCE = 1e3
LOG2E = 1.4426950408889634

HD = N_HEADS * HEAD_DIM
KVW = N_KV * HEAD_DIM
NEW_PAD = 128
KCH = 128
SPAN_KEYS = 512
PAGES_PER_STEP = 16
VMEM_LIMIT = 56 * 1024 * 1024

F32 = jnp.float32
BF16 = jnp.bfloat16


def _cparams(n_axes):
    return pltpu.CompilerParams(
        dimension_semantics=("arbitrary",) * n_axes, vmem_limit_bytes=VMEM_LIMIT)


def _dot_nt(a, b):
    return lax.dot_general(a, b, (((1,), (1,)), ((), ())), preferred_element_type=F32)


def _dot(a, b):
    return jnp.dot(a, b, preferred_element_type=F32)


def _pick(n, prefs):
    for p in prefs:
        if n % p == 0:
            return p
    return n


def _rms_cast_kernel(x_ref, g_ref, o_ref):
    x = x_ref[...]
    y = x * lax.rsqrt(jnp.mean(x * x, axis=-1, keepdims=True) + EPS)
    o_ref[...] = (y * g_ref[...]).astype(BF16)


def _rms_cast(x, g):
    n, d = x.shape
    tm = _pick(n, (512, 256, 128, 64, 8))
    return pl.pallas_call(
        _rms_cast_kernel,
        grid=(n // tm,),
        in_specs=[pl.BlockSpec((tm, d), lambda i: (i, 0)),
                  pl.BlockSpec((1, d), lambda i: (0, 0))],
        out_specs=pl.BlockSpec((tm, d), lambda i: (i, 0)),
        out_shape=jax.ShapeDtypeStruct((n, d), BF16),
        compiler_params=_cparams(1),
        name="rms_cast",
    )(x, g.reshape(1, d))


def _main_tile(nt):
    del nt
    return lambda i: jnp.maximum(i - 1, 0)


def _mm_kernel(*refs, has_res, w_is_t, nt):
    it = iter(refs)
    x_ref, xs_ref, w_ref = next(it), next(it), next(it)
    r_ref, rs_ref = (next(it), next(it)) if has_res else (None, None)
    o_ref, os_ref, wb_ref = next(it), next(it), next(it)
    i = pl.program_id(1)

    @pl.when(i == 0)
    def _():
        w = w_ref[...]
        wb_ref[...] = (jnp.transpose(w) if w_is_t else w).astype(BF16)

    def project(xr, rr, orf):
        acc = _dot(xr[...], wb_ref[...])
        orf[...] = acc if rr is None else rr[...] + acc

    def project_side_t():
        os_ref[...] = _dot_nt(w_ref[...].astype(BF16), xs_ref[...])

    pl.when(i == 0)(project_side_t if w_is_t else functools.partial(project, xs_ref, rs_ref, os_ref))
    pl.when(i >= 1)(functools.partial(project, x_ref, r_ref, o_ref))


def _matmul(x, xs, w, layer, m_out, res=None, res_s=None, w_is_t=False):
    n, k = x.shape
    ns = xs.shape[0]
    tm = _pick(n, (1024, 512, 256, 128, 64, 8))
    tn = _pick(m_out, (1024, 512, 256, 128))
    nt = n // tm
    main = _main_tile(nt)
    if w_is_t:
        w_spec = pl.BlockSpec((None, tn, k), lambda j, i: (layer, j, 0))
    else:
        w_spec = pl.BlockSpec((None, k, tn), lambda j, i: (layer, 0, j))
    in_specs = [pl.BlockSpec((tm, k), lambda j, i: (main(i), 0)),
                pl.BlockSpec((ns, k), lambda j, i: (0, 0)), w_spec]
    args = [x, xs, w]
    if res is not None:
        in_specs += [pl.BlockSpec((tm, tn), lambda j, i: (main(i), j)),
                     pl.BlockSpec((ns, tn), lambda j, i: (0, j))]
        args += [res, res_s]
    return pl.pallas_call(
        functools.partial(_mm_kernel, has_res=res is not None, w_is_t=w_is_t, nt=nt),
        grid=(m_out // tn, nt + 1),
        in_specs=in_specs,
        out_specs=[pl.BlockSpec((tm, tn), lambda j, i: (main(i), j)),
                   pl.BlockSpec((tn, ns), lambda j, i: (j, 0)) if w_is_t
                   else pl.BlockSpec((ns, tn), lambda j, i: (0, j))],
        out_shape=[jax.ShapeDtypeStruct((n, m_out), F32),
                   jax.ShapeDtypeStruct((m_out, ns) if w_is_t else (ns, m_out), F32)],
        scratch_shapes=[pltpu.VMEM((k, tn), BF16)],
        compiler_params=_cparams(2),
        name="matmul",
    )(*args)


def _mm_res_norm_kernel(*refs, n_norm, nt):
    x_ref, xs_ref, w_ref, r_ref, rs_ref = refs[:5]
    g_refs = refs[5:5 + n_norm]
    o_ref, os_ref = refs[5 + n_norm:7 + n_norm]
    xn_refs = refs[7 + n_norm:7 + 3 * n_norm]
    wb_ref = refs[7 + 3 * n_norm]
    i = pl.program_id(0)

    @pl.when(i == 0)
    def _():
        wb_ref[...] = w_ref[...].astype(BF16)

    def project(xr, rr, orf, xnrs):
        out = rr[...] + _dot(xr[...], wb_ref[...])
        orf[...] = out
        y = out * lax.rsqrt(jnp.mean(out * out, axis=-1, keepdims=True) + EPS)
        for g_ref, xn_ref in zip(g_refs, xnrs):
            xn_ref[...] = (y * g_ref[...]).astype(BF16)

    pl.when(i == 0)(functools.partial(project, xs_ref, rs_ref, os_ref, xn_refs[1::2]))
    pl.when(i >= 1)(functools.partial(project, x_ref, r_ref, o_ref, xn_refs[0::2]))


def _matmul_res_norm(x, xs, w, layer, res, res_s, gains):
    n, k = x.shape
    ns = xs.shape[0]
    d = res.shape[1]
    tm = _pick(n, (512, 256, 128, 64, 8))
    nt = n // tm
    main = _main_tile(nt)
    row = lambda i: (main(i), 0)
    side = lambda i: (0, 0)
    n_norm = len(gains)
    outs = pl.pallas_call(
        functools.partial(_mm_res_norm_kernel, n_norm=n_norm, nt=nt),
        grid=(nt + 1,),
        in_specs=[pl.BlockSpec((tm, k), row), pl.BlockSpec((ns, k), side),
                  pl.BlockSpec((None, k, d), lambda i: (layer, 0, 0), pipeline_mode=pl.Buffered(1)),
                  pl.BlockSpec((tm, d), row), pl.BlockSpec((ns, d), side)]
        + [pl.BlockSpec((1, d), side)] * n_norm,
        out_specs=[pl.BlockSpec((tm, d), row), pl.BlockSpec((ns, d), side)]
        + [pl.BlockSpec((tm, d), row), pl.BlockSpec((ns, d), side)] * n_norm,
        out_shape=[jax.ShapeDtypeStruct((n, d), F32), jax.ShapeDtypeStruct((ns, d), F32)]
        + [jax.ShapeDtypeStruct((n, d), BF16), jax.ShapeDtypeStruct((ns, d), BF16)] * n_norm,
        scratch_shapes=[pltpu.VMEM((k, d), BF16)],
        compiler_params=_cparams(1),
        name="matmul_res_norm",
    )(x, xs, w, res, res_s, *[g.reshape(1, d) for g in gains])
    return outs[0], outs[1], [(outs[2 + 2 * j], outs[3 + 2 * j]) for j in range(n_norm)]


def _conv_taps(c, c1, c2, w_ref):
    return w_ref[0:1, :] * c2 + w_ref[1:2, :] * c1 + w_ref[2:3, :] * c


def _conv_mm_kernel(x_ref, xs_ref, wb_ref, wc_ref, wh_ref, wz_ref, prev_ref, p1_ref, p2_ref, cw_ref,
                    y_ref, st_ref, ys_ref, cs_ref, wbf_ref, carry_ref, *, tps, nt, side_seq):
    i = pl.program_id(1)

    @pl.when(i == 0)
    def _():
        for k, wr in enumerate((wb_ref, wc_ref, wh_ref, wz_ref)):
            wbf_ref[k] = wr[...].astype(BF16)

    def parts(xr):
        x = xr[...]
        return (_dot(x, wbf_ref[0]), _dot(x, wbf_ref[1]) * _dot(x, wbf_ref[2]), _dot(x, wbf_ref[3]))

    @pl.when(i >= 1)
    def _():
        b, c, z = parts(x_ref)
        tm = c.shape[0]

        @pl.when((i - 1) % tps == 0)
        def _():
            carry_ref[0:2, :] = prev_ref[0]

        p0 = carry_ref[0:1, :]
        p1 = carry_ref[1:2, :]
        row = lax.broadcasted_iota(jnp.int32, c.shape, 0)
        c1 = jnp.where(row == 0, p1, pltpu.roll(c, 1, 0))
        c2 = jnp.where(row == 0, p0, jnp.where(row == 1, p1, pltpu.roll(c, 2, 0)))
        y_ref[...] = (jax.nn.silu(z) * (b * _conv_taps(c, c1, c2, cw_ref))).astype(BF16)
        last = c[tm - 2:tm, :]
        carry_ref[0:2, :] = last
        st_ref[0] = last

    @pl.when(i == 0)
    def _():
        b, c, z = parts(xs_ref)
        pos = lax.broadcasted_iota(jnp.int32, c.shape, 0) & (side_seq - 1)
        c1 = jnp.where(pos == 0, p1_ref[...], pltpu.roll(c, 1, 0))
        c2 = jnp.where(pos < 2, p2_ref[...], pltpu.roll(c, 2, 0))
        ys_ref[...] = (jax.nn.silu(z) * (b * _conv_taps(c, c1, c2, cw_ref))).astype(BF16)
        cs_ref[...] = c


def _conv_matmul(xn, xs, w_in, layer, prev, prev_s, w_conv, seq, side_seq):
    n, k = xn.shape
    ns = xs.shape[0]
    ch = w_in.shape[2] // 4
    tm = _pick(seq, (1024, 512, 256, 128, 64, 8))
    tc = _pick(ch, (256, 128))
    tps = seq // tm
    nc = ch // tc
    nt = n // tm
    main = _main_tile(nt)
    assert side_seq >= CONV_W - 1 and side_seq & (side_seq - 1) == 0
    p2 = jnp.pad(prev_s, ((0, 0), (0, side_seq - 2), (0, 0))).reshape(ns, ch)
    p1 = jnp.pad(prev_s[:, 1:2], ((0, 0), (0, side_seq - 1), (0, 0))).reshape(ns, ch)

    def w_spec(part):
        return pl.BlockSpec((None, k, tc), lambda j, i: (layer, 0, part * nc + j))

    side = pl.BlockSpec((ns, tc), lambda j, i: (0, j))
    return pl.pallas_call(
        functools.partial(_conv_mm_kernel, tps=tps, nt=nt, side_seq=side_seq),
        grid=(nc, nt + 1),
        in_specs=[pl.BlockSpec((tm, k), lambda j, i: (main(i), 0)),
                  pl.BlockSpec((ns, k), lambda j, i: (0, 0))] + [w_spec(p) for p in range(4)]
        + [pl.BlockSpec((1, 2, tc), lambda j, i: (main(i) // tps, 0, j)), side, side,
           pl.BlockSpec((CONV_W, tc), lambda j, i: (0, j))],
        out_specs=[pl.BlockSpec((tm, tc), lambda j, i: (main(i), j)),
                   pl.BlockSpec((1, 2, tc), lambda j, i: (main(i) // tps, 0, j)), side, side],
        out_shape=[jax.ShapeDtypeStruct((n, ch), BF16),
                   jax.ShapeDtypeStruct((n // seq, CONV_W - 1, ch), F32),
                   jax.ShapeDtypeStruct((ns, ch), BF16),
                   jax.ShapeDtypeStruct((ns, ch), F32)],
        scratch_shapes=[pltpu.VMEM((4, k, tc), BF16), pltpu.VMEM((8, tc), F32)],
        compiler_params=_cparams(2),
        name="conv_matmul",
    )(xn, xs, w_in, w_in, w_in, w_in, prev, p1, p2, w_conv)


def _norm_rope(x, g, cos2, sin2):
    y = x * lax.rsqrt(jnp.mean(x * x, axis=-1, keepdims=True) + EPS) * g
    return y * cos2 + pltpu.roll(y, HEAD_DIM // 2, 1) * sin2


def _kv_post_kernel(kv_ref, cos_ref, sin_ref, gk_ref, rows_ref, win_ref, *attn_refs):
    cos2 = cos_ref[...]
    sin2 = sin_ref[...]
    tm = kv_ref.shape[0]
    for br in range(N_BRANCH):
        base = br * 2 * KVW
        out_ref, row0 = (rows_ref, br * 2 * N_KV) if br < 2 else (win_ref, 0)
        for g in range(N_KV):
            c0 = base + g * HEAD_DIM
            k = _norm_rope(kv_ref[:, c0:c0 + HEAD_DIM], gk_ref[br:br + 1, :], cos2, sin2)
            out_ref[:, row0 + g, :] = k
            out_ref[:, row0 + N_KV + g, :] = kv_ref[:, c0 + KVW:c0 + KVW + HEAD_DIM]
            if attn_refs and br >= 1:
                attn_refs[2 * (br - 1)][:, g * HEAD_DIM:(g + 1) * HEAD_DIM] = k.astype(BF16)
        if attn_refs and br >= 1:
            v = kv_ref[:, base + KVW:base + 2 * KVW]
            vt_ref = attn_refs[2 * (br - 1) + 1]
            for a in range(tm // KCH):
                vt_ref[a] = jnp.transpose(v[a * KCH:(a + 1) * KCH, :]).astype(BF16)


def _kv_post(kv, cos2, sin2, g_k, seq, for_attn):
    n = kv.shape[0]
    tm = _pick(seq, (256, 128, 64, 8))
    tps = seq // tm
    out_specs = [pl.BlockSpec((tm, 4 * N_KV, HEAD_DIM), lambda i: (i, 0, 0)),
                 pl.BlockSpec((tm, 2 * N_KV, HEAD_DIM), lambda i: (i, 0, 0))]
    out_shape = [jax.ShapeDtypeStruct((n, 4 * N_KV, HEAD_DIM), F32),
                 jax.ShapeDtypeStruct((n, 2 * N_KV, HEAD_DIM), F32)]
    if for_attn:
        assert tm % KCH == 0
        for _ in range(2):
            out_specs += [pl.BlockSpec((tm, KVW), lambda i: (i, 0)),
                          pl.BlockSpec((tm // KCH, KVW, KCH), lambda i: (i, 0, 0))]
            out_shape += [jax.ShapeDtypeStruct((n, KVW), BF16),
                          jax.ShapeDtypeStruct((n // KCH, KVW, KCH), BF16)]
    return pl.pallas_call(
        _kv_post_kernel,
        grid=(n // tm,),
        in_specs=[pl.BlockSpec((tm, 6 * KVW), lambda i: (i, 0)),
                  pl.BlockSpec((tm, HEAD_DIM), lambda i: (i % tps, 0)),
                  pl.BlockSpec((tm, HEAD_DIM), lambda i: (i % tps, 0)),
                  pl.BlockSpec((N_BRANCH, HEAD_DIM), lambda i: (0, 0))],
        out_specs=out_specs,
        out_shape=out_shape,
        compiler_params=_cparams(1),
        name="kv_post",
    )(kv, cos2, sin2, g_k)


def _compress_kernel(rows_ref, w_ref, o_ref):
    x = rows_ref[...]
    nb = x.shape[0] // BLK
    x = x.reshape(nb, BLK, x.shape[1]) * w_ref[...][None]
    o_ref[...] = jnp.sum(x, axis=1)


def _compress(rows, w_exp):
    n = rows.shape[0]
    nblk = n // BLK
    per = _pick(nblk, (8,))
    return pl.pallas_call(
        _compress_kernel,
        grid=(nblk // per,),
        in_specs=[pl.BlockSpec((per * BLK, 2 * KVW), lambda i: (i, 0)),
                  pl.BlockSpec((BLK, 2 * KVW), lambda i: (0, 0))],
        out_specs=pl.BlockSpec((per, 2 * KVW), lambda i: (i, 0)),
        out_shape=jax.ShapeDtypeStruct((nblk, 2 * KVW), F32),
        compiler_params=_cparams(1),
        name="compress",
    )(rows, w_exp)


def _block_sums(x, w):
    nb = x.shape[0] // BLK
    return jnp.sum(x.reshape(nb, BLK, 2 * N_KV, HEAD_DIM) * w[None], axis=1)


def _compress_rows_kernel(rows_ref, w_ref, o_ref):
    o_ref[...] = _block_sums(rows_ref[...], w_ref[...])


def _compress_rows(rows3, w_exp3):
    n = rows3.shape[0]
    nblk = n // BLK
    per = _pick(nblk, (8,))
    return pl.pallas_call(
        _compress_rows_kernel,
        grid=(nblk // per,),
        in_specs=[pl.BlockSpec((per * BLK, 2 * N_KV, HEAD_DIM), lambda i: (i, 0, 0)),
                  pl.BlockSpec((BLK, 2 * N_KV, HEAD_DIM), lambda i: (0, 0, 0))],
        out_specs=pl.BlockSpec((per, 2 * N_KV, HEAD_DIM), lambda i: (i, 0, 0)),
        out_shape=jax.ShapeDtypeStruct((nblk, 2 * N_KV, HEAD_DIM), F32),
        compiler_params=_cparams(1),
        name="compress_rows",
    )(rows3, w_exp3)


def _compress_pages_kernel(pt_ref, *refs):
    page_refs = refs[:PAGES_PER_STEP]
    w_ref = refs[PAGES_PER_STEP]
    o_ref = refs[PAGES_PER_STEP + 1]
    w = w_ref[...]
    for k, pr in enumerate(page_refs):
        nb = pr.shape[1] // BLK
        o_ref[0, k * nb:(k + 1) * nb] = _block_sums(pr[0], w)


def _compress_pages(cache4, page_table, w_exp3):
    bd, n_pages = page_table.shape
    page = cache4.shape[1]
    bpp = page // BLK
    steps = n_pages // PAGES_PER_STEP
    rows_out = PAGES_PER_STEP * bpp

    def page_spec(k):
        return pl.BlockSpec((1, page, 2 * N_KV, HEAD_DIM),
                            lambda b, c, pt: (pt[b, c * PAGES_PER_STEP + k], 0, 0, 0))

    grid_spec = pltpu.PrefetchScalarGridSpec(
        num_scalar_prefetch=1,
        grid=(bd, steps),
        in_specs=[page_spec(k) for k in range(PAGES_PER_STEP)]
        + [pl.BlockSpec((BLK, 2 * N_KV, HEAD_DIM), lambda b, c, pt: (0, 0, 0))],
        out_specs=pl.BlockSpec((1, rows_out, 2 * N_KV, HEAD_DIM), lambda b, c, pt: (b, c, 0, 0)),
    )
    return pl.pallas_call(
        _compress_pages_kernel,
        grid_spec=grid_spec,
        out_shape=jax.ShapeDtypeStruct((bd, n_pages * bpp, 2 * N_KV, HEAD_DIM), F32),
        compiler_params=_cparams(2),
        name="compress_pages",
    )(page_table, *([cache4] * PAGES_PER_STEP), w_exp3)


def _softmax_parts(s, mask):
    sm = jnp.where(mask, s, NEG)
    m = jnp.max(sm, axis=-1, keepdims=True)
    p = jnp.where(mask, jnp.exp(sm - m), 0.0)
    return p, m, jnp.sum(p, axis=-1, keepdims=True)


def _safe_inv(l):
    return jnp.where(l > 0.0, 1.0 / l, 0.0)


def _topk_mask(score, n_f, k):
    sel = jnp.zeros(score.shape, F32)
    big = jnp.float32(score.shape[-1] + 1)
    for _ in range(k):
        mx = jnp.max(score, axis=-1, keepdims=True)
        idx = jnp.min(jnp.where(score == mx, n_f, big), axis=-1, keepdims=True)
        hit = n_f == idx
        sel = jnp.where(hit, 1.0, sel)
        score = jnp.where(hit, -jnp.inf, score)
    return sel


def _select_blocks(psum, cur, n_i):
    forced = ((n_i == 0) | (n_i == cur) | (n_i == cur - 1)).astype(F32)
    valid = n_i <= cur
    score = jnp.where(valid, psum + FORCE * forced, NEG)
    sel = _topk_mask(score, n_i.astype(F32), min(N_SEL, psum.shape[-1]))
    return jnp.where(valid, sel, 0.0)


def _q_heads(uq_ref, g, gq, cos2, sin2, scale=None):
    hs = []
    for r in range(GRP):
        c0 = (g * GRP + r) * HEAD_DIM
        h = _norm_rope(uq_ref[:, c0:c0 + HEAD_DIM], gq, cos2, sin2)
        hs.append(h if scale is None else h * scale)
    return jnp.concatenate(hs, axis=0).astype(BF16)


def _tile_rows(x, reps):
    return jnp.concatenate([x] * reps, axis=0)


def _gate_and_store(o_ref, g, outs, z_refs, gate, t):
    for r in range(GRP):
        h = g * GRP + r
        acc = None
        for br in range(N_BRANCH):
            zg = jax.nn.silu(z_refs[br][:, h * HEAD_DIM:(h + 1) * HEAD_DIM]) \
                * gate[:, br * N_HEADS + h:br * N_HEADS + h + 1]
            term = outs[br][r * t:(r + 1) * t, :] * zg
            acc = term if acc is None else acc + term
        o_ref[:, h * HEAD_DIM:(h + 1) * HEAD_DIM] = acc.astype(o_ref.dtype)


def _rank_select_t(score_t, n_col, k):
    rank = jnp.zeros(score_t.shape, F32)
    for m in range(score_t.shape[0]):
        row = score_t[m:m + 1, :]
        tie = (n_col > m).astype(F32)
        rank = rank + jnp.where(row > score_t, 1.0, 0.0) + jnp.where(row == score_t, tie, 0.0)
    return jnp.where(rank < k, 1.0, 0.0)


def _prompt_attn_kernel(uq_ref, z0_ref, z1_ref, z2_ref, gate_ref, kvc_ref, vct_ref, ks_ref, vst_ref,
                        kw_ref, vwt_ref, cos_ref, sin_ref, gq_ref, o_ref,
                        q_scr, s_scr, bias_scr, mrun_scr, lrun_scr, ot_scr, ocmp_scr, *, seq, spk):
    i = pl.program_id(1)
    t = BLK
    lanes = GRP * t
    nb = seq // BLK
    nch = seq // KCH
    nwc = min(WINDOW // KCH + 1, nch)
    bpc = KCH // BLK
    k_in_blk = lax.broadcasted_iota(jnp.int32, (BLK, 1), 0)
    cos2 = cos_ref[...]
    sin2 = sin_ref[...]
    gq = gq_ref[...]
    gate = jax.nn.sigmoid(gate_ref[...])
    qpos_l = i * t + (lax.broadcasted_iota(jnp.int32, (1, lanes), 1) & (t - 1))
    qpos_h = qpos_l[:, 0:2 * t]
    cur_h = qpos_h // BLK
    n_col = lax.broadcasted_iota(jnp.int32, (nb, 1), 0)
    cmask_t = (n_col * BLK + (BLK - 1)) <= qpos_l
    forced_t = ((n_col == 0) | (n_col == cur_h) | (n_col == cur_h - 1)).astype(F32)
    valid_t = n_col <= cur_h
    k_in_chunk = lax.broadcasted_iota(jnp.int32, (KCH, 1), 0)

    def group_max(x):
        return jnp.max(x.reshape(KCH // 8, 8, lanes), axis=0)

    def group_sum(x):
        return jnp.sum(x.reshape(KCH // 8, 8, lanes), axis=0)

    def finish(lrun, ot):
        l = jnp.sum(lrun, axis=0, keepdims=True)
        return jnp.transpose(ot * _safe_inv(l))

    for g in range(N_KV):
        q = _q_heads(uq_ref, g, gq, cos2, sin2, LOG2E * HEAD_DIM ** -0.5)
        ksl = slice(g * HEAD_DIM, (g + 1) * HEAD_DIM)
        kc = kvc_ref[0, :, ksl].astype(BF16)

        st = jnp.where(cmask_t, _dot_nt(kc, q), NEG)
        pt = jnp.where(cmask_t, jnp.exp2(st - jnp.max(st, axis=0, keepdims=True)), 0.0)
        pt = pt * _safe_inv(jnp.sum(pt, axis=0, keepdims=True))
        ocmp_scr[g] = _dot(vct_ref[0, ksl, :].astype(BF16), pt.astype(BF16))

        half = pt[:, 0:2 * t] + pt[:, 2 * t:4 * t]
        psum_t = half + pltpu.roll(half, t, 1)
        score_t = jnp.where(valid_t, psum_t + FORCE * forced_t, NEG)
        sel_t = jnp.where(valid_t, _rank_select_t(score_t, n_col, min(N_SEL, nb)), 0.0)
        bias_t = (sel_t - 1.0) * (-NEG)
        bias_scr[g] = jnp.concatenate([bias_t, bias_t], axis=1)

        q_scr[g] = q
        mrun_scr[g] = jnp.full((8, lanes), NEG, F32)
        lrun_scr[g] = jnp.zeros((8, lanes), F32)
        ot_scr[g] = jnp.zeros((HEAD_DIM, lanes), F32)

    def sel_pass1(sc, diagonal):
        for g in range(N_KV):
            ksl = slice(g * HEAD_DIM, (g + 1) * HEAD_DIM)
            st = _dot_nt(ks_ref[0, sc * spk:(sc + 1) * spk, ksl], q_scr[g])
            mx = mrun_scr[g]
            for j in range(spk // BLK):
                n = sc * (spk // BLK) + j
                blk = st[j * BLK:(j + 1) * BLK] + bias_scr[g, n:n + 1, :]
                if diagonal:
                    blk = jnp.where(n * BLK + k_in_blk <= qpos_l, blk, NEG)
                s_scr[g, n // bpc, (n % bpc) * BLK:(n % bpc + 1) * BLK, :] = blk
                mx = jnp.maximum(mx, jnp.max(blk.reshape(BLK // 8, 8, lanes), axis=0))
            mrun_scr[g] = mx

    def sel_pass2(sc):
        for g in range(N_KV):
            ksl = slice(g * HEAD_DIM, (g + 1) * HEAD_DIM)
            m = jnp.max(mrun_scr[g], axis=0, keepdims=True)
            lrun = lrun_scr[g]
            ot = ot_scr[g]
            for c in range(sc * (spk // KCH), (sc + 1) * (spk // KCH)):
                p = jnp.exp2(s_scr[g, c] - m)
                lrun = lrun + group_sum(p)
                ot = ot + _dot(vst_ref[c, ksl, :], p.astype(BF16))
            lrun_scr[g] = lrun
            ot_scr[g] = ot

    for sc in range(seq // spk):
        pl.when((sc + 1) * spk <= i * t)(functools.partial(sel_pass1, sc, False))
        pl.when((sc * spk <= i * t) & (i * t < (sc + 1) * spk))(functools.partial(sel_pass1, sc, True))
    for sc in range(seq // spk):
        pl.when(sc * spk <= i * t)(functools.partial(sel_pass2, sc))

    cs = jnp.minimum(jnp.maximum(i - WINDOW // BLK, 0) // bpc, nch - nwc)
    for g in range(N_KV):
        ksl = slice(g * HEAD_DIM, (g + 1) * HEAD_DIM)
        q = q_scr[g]
        o_sel = finish(lrun_scr[g], ot_scr[g])
        mrun = jnp.full((8, lanes), NEG, F32)
        for j in range(nwc):
            c = cs + j
            k = kw_ref[0, pl.ds(pl.multiple_of(c * KCH, KCH), KCH), ksl]
            dpos = qpos_l - (c * KCH + k_in_chunk)
            st = jnp.where((dpos >= 0) & (dpos < WINDOW), _dot_nt(k, q), NEG)
            s_scr[g, j] = st
            mrun = jnp.maximum(mrun, group_max(st))
        m = jnp.max(mrun, axis=0, keepdims=True)
        lrun = jnp.zeros((8, lanes), F32)
        ot = jnp.zeros((HEAD_DIM, lanes), F32)
        for j in range(nwc):
            p = jnp.exp2(s_scr[g, j] - m)
            lrun = lrun + group_sum(p)
            ot = ot + _dot(vwt_ref[cs + j, ksl, :], p.astype(BF16))
        o_win = finish(lrun, ot)

        _gate_and_store(o_ref, g, (jnp.transpose(ocmp_scr[g]), o_sel, o_win),
                        (z0_ref, z1_ref, z2_ref), gate, t)


def _prompt_attn(u, gate, kvc, vct, ks, vst, kw, vwt, cos2, sin2, gq, batch, seq):
    n = u.shape[0]
    t = BLK
    spk = _pick(seq, (SPAN_KEYS, KCH))
    assert GRP * t == 2 * KCH and seq % KCH == 0 and spk % KCH == 0
    nqb = seq // t
    nb = seq // BLK
    nch = seq // KCH
    lanes = GRP * t
    row = lambda b, i: (b * nqb + i, 0)
    keys = pl.BlockSpec((1, seq, KVW), lambda b, i: (b, 0, 0))
    vals_t = pl.BlockSpec((nch, KVW, KCH), lambda b, i: (b, 0, 0))
    return pl.pallas_call(
        functools.partial(_prompt_attn_kernel, seq=seq, spk=spk),
        grid=(batch, nqb),
        in_specs=[pl.BlockSpec((t, HD), row),
                  pl.BlockSpec((t, HD), lambda b, i: (b * nqb + i, 1)),
                  pl.BlockSpec((t, HD), lambda b, i: (b * nqb + i, 2)),
                  pl.BlockSpec((t, HD), lambda b, i: (b * nqb + i, 3)),
                  pl.BlockSpec((t, HEAD_DIM), row),
                  pl.BlockSpec((1, nb, 2 * KVW), lambda b, i: (b, 0, 0)),
                  pl.BlockSpec((1, KVW, nb), lambda b, i: (b, 0, 0)),
                  keys, vals_t, keys, vals_t,
                  pl.BlockSpec((t, HEAD_DIM), lambda b, i: (i, 0)),
                  pl.BlockSpec((t, HEAD_DIM), lambda b, i: (i, 0)),
                  pl.BlockSpec((1, HEAD_DIM), lambda b, i: (0, 0))],
        out_specs=pl.BlockSpec((t, HD), row),
        out_shape=jax.ShapeDtypeStruct((n, HD), BF16),
        scratch_shapes=[pltpu.VMEM((N_KV, lanes, HEAD_DIM), BF16),
                        pltpu.VMEM((N_KV, nch, KCH, lanes), F32),
                        pltpu.VMEM((N_KV, nb, lanes), F32),
                        pltpu.VMEM((N_KV, 8, lanes), F32),
                        pltpu.VMEM((N_KV, 8, lanes), F32),
                        pltpu.VMEM((N_KV, HEAD_DIM, lanes), F32),
                        pltpu.VMEM((N_KV, HEAD_DIM, lanes), F32)],
        compiler_params=_cparams(2),
        name="prompt_attn",
    )(u, u, u, u, gate, kvc.reshape(batch, nb, 2 * KVW), vct, ks.reshape(batch, seq, KVW), vst,
      kw.reshape(batch, seq, KVW), vwt, cos2, sin2, gq)


def _decode_front_kernel(uq_ref, kvc_ref, swin_ref, nwin_ref, cos_ref, sin_ref, gq_ref,
                         qn_ref, sel_ref, ocmp_ref, owin_ref, *, past, t):
    scale = HEAD_DIM ** -0.5
    cos2 = cos_ref[...]
    sin2 = sin_ref[...]
    gq = gq_ref[...]
    nbp = kvc_ref.shape[1]
    wbuf = swin_ref.shape[1]
    qpos = past + lax.broadcasted_iota(jnp.int32, (t, 1), 0)
    qpos_r = _tile_rows(qpos, GRP)
    cur = qpos // BLK
    n_i = lax.broadcasted_iota(jnp.int32, (1, nbp), 1)
    cmask = (n_i * BLK + (BLK - 1)) <= qpos_r
    kwpos_old = (past - wbuf) + lax.broadcasted_iota(jnp.int32, (1, wbuf), 1)
    kwpos_new = past + lax.broadcasted_iota(jnp.int32, (1, NEW_PAD), 1)
    d_old = qpos_r - kwpos_old
    d_new = qpos_r - kwpos_new
    m_old = (d_old >= 0) & (d_old < WINDOW) & (kwpos_old >= 0)
    m_new = (d_new >= 0) & (d_new < WINDOW) & (kwpos_new < past + t)

    psums = []
    for g in range(N_KV):
        q = _q_heads(uq_ref, g, gq, cos2, sin2)
        rows = slice(g * GRP * t, (g + 1) * GRP * t)
        ksl = slice(g * HEAD_DIM, (g + 1) * HEAD_DIM)
        vsl = slice(KVW + g * HEAD_DIM, KVW + (g + 1) * HEAD_DIM)
        qn_ref[0, rows, :] = q

        s = _dot_nt(q, kvc_ref[0, :, ksl].astype(BF16)) * scale
        p, _, l = _softmax_parts(s, cmask)
        p = p * _safe_inv(l)
        ocmp_ref[0, rows, :] = _dot(p.astype(BF16), kvc_ref[0, :, vsl].astype(BF16))

        psum = p[0:t]
        for r in range(1, GRP):
            psum = psum + p[r * t:(r + 1) * t]
        psums.append(psum)

        s_old = _dot_nt(q, swin_ref[0, :, ksl].astype(BF16)) * scale
        s_new = _dot_nt(q, nwin_ref[0, :, ksl].astype(BF16)) * scale
        mx = jnp.maximum(jnp.max(jnp.where(m_old, s_old, NEG), axis=-1, keepdims=True),
                         jnp.max(jnp.where(m_new, s_new, NEG), axis=-1, keepdims=True))
        p_old = jnp.where(m_old, jnp.exp(jnp.where(m_old, s_old, NEG) - mx), 0.0)
        p_new = jnp.where(m_new, jnp.exp(jnp.where(m_new, s_new, NEG) - mx), 0.0)
        l = jnp.sum(p_old, axis=-1, keepdims=True) + jnp.sum(p_new, axis=-1, keepdims=True)
        o = _dot(p_old.astype(BF16), swin_ref[0, :, vsl].astype(BF16)) \
            + _dot(p_new.astype(BF16), nwin_ref[0, :, vsl].astype(BF16))
        owin_ref[0, rows, :] = o * _safe_inv(l)

    sel_ref[0] = _select_blocks(jnp.concatenate(psums, axis=0), _tile_rows(cur, N_KV), n_i)


def _decode_front(u, kvc_all, state_win2, new_win, cos2, sin2, gq, bd, t, past):
    nbp = kvc_all.shape[1]
    wbuf = state_win2.shape[1]
    rows = N_HEADS * t
    return pl.pallas_call(
        functools.partial(_decode_front_kernel, past=past, t=t),
        grid=(bd,),
        in_specs=[pl.BlockSpec((t, HD), lambda b: (b, 0)),
                  pl.BlockSpec((1, nbp, 2 * KVW), lambda b: (b, 0, 0)),
                  pl.BlockSpec((1, wbuf, 2 * KVW), lambda b: (b, 0, 0)),
                  pl.BlockSpec((1, NEW_PAD, 2 * KVW), lambda b: (b, 0, 0)),
                  pl.BlockSpec((t, HEAD_DIM), lambda b: (0, 0)),
                  pl.BlockSpec((t, HEAD_DIM), lambda b: (0, 0)),
                  pl.BlockSpec((1, HEAD_DIM), lambda b: (0, 0))],
        out_specs=[pl.BlockSpec((1, rows, HEAD_DIM), lambda b: (b, 0, 0)),
                   pl.BlockSpec((1, N_KV * t, nbp), lambda b: (b, 0, 0)),
                   pl.BlockSpec((1, rows, HEAD_DIM), lambda b: (b, 0, 0)),
                   pl.BlockSpec((1, rows, HEAD_DIM), lambda b: (b, 0, 0))],
        out_shape=[jax.ShapeDtypeStruct((bd, rows, HEAD_DIM), BF16),
                   jax.ShapeDtypeStruct((bd, N_KV * t, nbp), F32),
                   jax.ShapeDtypeStruct((bd, rows, HEAD_DIM), F32),
                   jax.ShapeDtypeStruct((bd, rows, HEAD_DIM), F32)],
        compiler_params=_cparams(1),
        name="decode_front",
    )(u, kvc_all, state_win2, new_win, cos2, sin2, gq)


def _values_product(vx, pb):
    return _dot(pb, vx)


def _decode_sel_kernel(pt_ref, *refs, t, n_steps):
    del pt_ref
    np_ = PAGES_PER_STEP
    (qn_ref, selc_ref, seln_ref) = refs[:3]
    k_refs = refs[3:3 + np_]
    v_refs = refs[3 + np_:3 + 2 * np_]
    (new_ref, ocmp_ref, owin_ref, z0_ref, z1_ref, z2_ref, gate_ref, e_ref,
     o_ref, m_ref, l_ref, acc_ref) = refs[3 + 2 * np_:]
    c = pl.program_id(1)
    scale = HEAD_DIM ** -0.5
    gt = GRP * t

    @pl.when(c == 0)
    def _():
        m_ref[...] = jnp.full(m_ref.shape, NEG, F32)
        l_ref[...] = jnp.zeros(l_ref.shape, F32)
        acc_ref[...] = jnp.zeros(acc_ref.shape, F32)

    def softmax_part(s, mask, pv_of):
        sm = jnp.where(mask, s, NEG)
        m = jnp.max(sm, axis=-1, keepdims=True)
        p = jnp.where(mask, jnp.exp(sm - m), 0.0)
        return m, jnp.sum(p, axis=-1, keepdims=True), pv_of(p.astype(BF16))

    def merge(parts):
        m_old = m_ref[...]
        m_new = m_old
        for m, _, _ in parts:
            m_new = jnp.maximum(m_new, m)
        alpha = jnp.exp(m_old - m_new)
        l = alpha * l_ref[...]
        acc = alpha * acc_ref[...]
        for m, lp, pv in parts:
            w = jnp.exp(m - m_new)
            l = l + w * lp
            acc = acc + w * pv
        l_ref[...] = l
        acc_ref[...] = acc
        m_ref[...] = m_new

    def cache_rows(page_refs):
        rows = [pr[0, :, 0].reshape(pr.shape[1] * N_KV, HEAD_DIM) for pr in page_refs]
        return jnp.concatenate(rows, axis=0).astype(BF16)

    halves = 2 if np_ % 2 == 0 else 1
    pp = np_ // halves
    bph = e_ref.shape[1]
    parts = []
    for h in range(halves):
        vx = cache_rows(v_refs[h * pp:(h + 1) * pp])
        s = _dot_nt(qn_ref[0], cache_rows(k_refs[h * pp:(h + 1) * pp])) * scale
        mask = jnp.concatenate(
            [_dot(selc_ref[0, 0, g * t:(g + 1) * t, h * bph:(h + 1) * bph].astype(BF16), e_ref[g])
             for g in range(N_KV) for _ in range(GRP)], axis=0) > 0.5
        parts.append(softmax_part(s, mask, functools.partial(_values_product, vx)))
    merge(parts)

    @pl.when(c == n_steps - 1)
    def _():
        gate = jax.nn.sigmoid(gate_ref[...])
        tq = _tile_rows(lax.broadcasted_iota(jnp.int32, (t, 1), 0), N_HEADS)
        causal = lax.broadcasted_iota(jnp.int32, (1, NEW_PAD), 1) <= tq
        sel_new = jnp.concatenate([seln_ref[0, 0, g * t:(g + 1) * t, 0:1]
                                   for g in range(N_KV) for _ in range(GRP)], axis=0) > 0.5
        s_new = jnp.concatenate(
            [_dot_nt(qn_ref[0, g * gt:(g + 1) * gt, :],
                     new_ref[0, :, g * HEAD_DIM:(g + 1) * HEAD_DIM].astype(BF16))
             for g in range(N_KV)], axis=0) * scale

        def pv_new(pb):
            return jnp.concatenate(
                [_dot(pb[g * gt:(g + 1) * gt],
                      new_ref[0, :, KVW + g * HEAD_DIM:KVW + (g + 1) * HEAD_DIM].astype(BF16))
                 for g in range(N_KV)], axis=0)

        merge([softmax_part(s_new, sel_new & causal, pv_new)])
        o_sel = acc_ref[...] * _safe_inv(l_ref[...])
        for g in range(N_KV):
            rows = slice(g * gt, (g + 1) * gt)
            _gate_and_store(o_ref, g, (ocmp_ref[0, rows, :], o_sel[rows], owin_ref[0, rows, :]),
                            (z0_ref, z1_ref, z2_ref), gate, t)


def _decode_sel(page_table, qn, selc, cache_kv, new_sel, ocmp, owin, u, gate, e4, bd, t):
    n_pages = page_table.shape[1]
    page = cache_kv.shape[1]
    n_steps = n_pages // PAGES_PER_STEP
    bps = PAGES_PER_STEP * (page // BLK)
    rows = N_HEADS * t

    def page_spec(part, k):
        return pl.BlockSpec((1, page, 1, N_KV, HEAD_DIM),
                            lambda b, c, pt: (pt[b, c * PAGES_PER_STEP + k], 0, part, 0, 0))

    grid_spec = pltpu.PrefetchScalarGridSpec(
        num_scalar_prefetch=1,
        grid=(bd, n_steps),
        in_specs=[pl.BlockSpec((1, rows, HEAD_DIM), lambda b, c, pt: (b, 0, 0)),
                  pl.BlockSpec((1, 1, N_KV * t, bps), lambda b, c, pt: (b, c, 0, 0)),
                  pl.BlockSpec((1, 1, N_KV * t, bps), lambda b, c, pt: (b, n_steps, 0, 0))]
        + [page_spec(2, k) for k in range(PAGES_PER_STEP)]
        + [page_spec(3, k) for k in range(PAGES_PER_STEP)]
        + [pl.BlockSpec((1, NEW_PAD, 2 * KVW), lambda b, c, pt: (b, 0, 0)),
           pl.BlockSpec((1, rows, HEAD_DIM), lambda b, c, pt: (b, 0, 0)),
           pl.BlockSpec((1, rows, HEAD_DIM), lambda b, c, pt: (b, 0, 0)),
           pl.BlockSpec((t, HD), lambda b, c, pt: (b, 1)),
           pl.BlockSpec((t, HD), lambda b, c, pt: (b, 2)),
           pl.BlockSpec((t, HD), lambda b, c, pt: (b, 3)),
           pl.BlockSpec((t, HEAD_DIM), lambda b, c, pt: (b, 0)),
           pl.BlockSpec(e4.shape, lambda b, c, pt: (0, 0, 0))],
        out_specs=pl.BlockSpec((t, HD), lambda b, c, pt: (b, 0)),
        scratch_shapes=[pltpu.VMEM((rows, 1), F32), pltpu.VMEM((rows, 1), F32),
                        pltpu.VMEM((rows, HEAD_DIM), F32)],
    )
    return pl.pallas_call(
        functools.partial(_decode_sel_kernel, t=t, n_steps=n_steps),
        grid_spec=grid_spec,
        out_shape=jax.ShapeDtypeStruct((bd * t, HD), BF16),
        compiler_params=_cparams(2),
        name="decode_sel",
    )(page_table, qn, selc, selc, *([cache_kv] * (2 * PAGES_PER_STEP)), new_sel, ocmp, owin,
      u, u, u, gate, e4)


def _rope_tables(pos):
    half = HEAD_DIM // 2
    inv = ROPE_THETA ** (-jnp.arange(half, dtype=F32) / half)
    ang = pos.astype(F32)[:, None] * inv[None, :]
    c = jnp.cos(ang)
    s = jnp.sin(ang)
    return jnp.concatenate([c, c], axis=-1), jnp.concatenate([-s, s], axis=-1)


def _block_group_expand(nblocks):
    col = jnp.arange(nblocks * BLK * N_KV, dtype=jnp.int32)
    in_block = (col // N_KV) // BLK == jnp.arange(nblocks, dtype=jnp.int32)[:, None]
    in_group = col % N_KV == jnp.arange(N_KV, dtype=jnp.int32)[:, None]
    return (in_block[None, :, :] & in_group[:, None, :]).astype(BF16)


def kernel(x_prompt, x_sample, state_conv, cache_kv, state_win, page_table, g_a, w_in_a, conv_w,
           w_out_a, g_kv, w_kv, g_k, w_cmp, g_b, w_qz, g_q, w_out_b):
    batch, seq, d = x_prompt.shape
    bd, t, _ = x_sample.shape
    ch = conv_w.shape[-1]
    n_pool, page = cache_kv.shape[:2]
    n_pages = page_table.shape[1]
    past = n_pages * page
    wbuf = state_win.shape[1]
    assert seq % BLK == 0 and page % BLK == 0 and t <= BLK and n_pages % PAGES_PER_STEP == 0

    w_exp3 = jnp.broadcast_to(w_cmp.reshape(BLK, 2 * N_KV, 1), (BLK, 2 * N_KV, HEAD_DIM))
    w_exp = w_exp3.reshape(BLK, 2 * KVW)
    n_gate = N_BRANCH * N_HEADS
    w_qz_t = jnp.swapaxes(w_qz, 1, 2)
    w_gate_t = jnp.pad(w_qz_t[:, 4 * HD:4 * HD + n_gate, :], ((0, 0), (0, HEAD_DIM - n_gate), (0, 0)))

    xp = x_prompt.reshape(batch * seq, d)
    xs = x_sample.reshape(bd * t, d)
    prev0 = jnp.zeros((batch, CONV_W - 1, ch), F32)
    xn_p, xn_s = _rms_cast(xp, g_a[0]), _rms_cast(xs, g_a[0])
    conv_p, conv_s = [], []
    for layer in range(N_A):
        y_p, st_p, y_s, c_s = _conv_matmul(xn_p, xn_s, w_in_a, layer, prev0, state_conv[layer],
                                           conv_w[layer], seq, t)
        conv_p.append(st_p)
        conv_s.append(c_s.reshape(bd, t, ch)[:, t - (CONV_W - 1):])
        gains = [g_a[layer + 1]] if layer + 1 < N_A else [g_kv, g_b[0]]
        xp, xs, xns = _matmul_res_norm(y_p, y_s, w_out_a, layer, xp, xs, gains)
        xn_p, xn_s = xns[0]
    conv_p, conv_s = jnp.stack(conv_p), jnp.stack(conv_s)
    (xn_kv_p, xn_kv_s), (xn_q_p, xn_q_s) = xns

    kv_p, kv_s = _matmul(xn_kv_p, xn_kv_s, w_kv[None], 0, 6 * KVW)
    cos_p, sin_p = _rope_tables(jnp.arange(seq, dtype=jnp.int32))
    cos_s, sin_s = _rope_tables(past + jnp.arange(t, dtype=jnp.int32))
    rows_p, win_p, ks_p, vst_p, kw_p, vwt_p = _kv_post(kv_p, cos_p, sin_p, g_k, seq, True)
    rows_s, win_s = _kv_post(kv_s, cos_s, sin_s, g_k, t, False)
    nb_p = seq // BLK
    kvc_p = _compress_rows(rows_p, w_exp3).reshape(batch * nb_p, 2 * KVW)
    vct_p = kvc_p.reshape(batch, nb_p, 2, KVW)[:, :, 1].transpose(0, 2, 1)

    cache4 = cache_kv.reshape(n_pool, page, 4 * N_KV, HEAD_DIM)
    kvc_past = _compress_pages(cache4, page_table, w_exp3).reshape(bd, past // BLK, 2 * KVW)
    rows_s3 = rows_s.reshape(bd, t, 4 * KVW)
    rows_pad = jnp.pad(rows_s3, ((0, 0), (0, NEW_PAD - t), (0, 0)))
    kvc_new = _compress(rows_pad[:, :BLK, :2 * KVW].reshape(bd * BLK, 2 * KVW), w_exp)
    nb_past = past // BLK
    bps = PAGES_PER_STEP * (page // BLK)
    nbp = nb_past + bps
    kvc_all = jnp.concatenate(
        [kvc_past, kvc_new[:, None, :], jnp.zeros((bd, bps - 1, 2 * KVW), F32)], axis=1)
    new_sel = rows_pad[:, :, 2 * KVW:]
    new_win = jnp.pad(win_s.reshape(bd, t, 2 * KVW), ((0, 0), (0, NEW_PAD - t), (0, 0)))
    state_win2 = state_win.reshape(bd, wbuf, 2 * KVW)
    e4 = _block_group_expand(bps // 2 if PAGES_PER_STEP % 2 == 0 else bps)

    for lb in range(N_B):
        u_p, u_s_t = _matmul(xn_q_p, xn_q_s, w_qz_t, lb, 4 * HD, w_is_t=True)
        gate_p, gate_s_t = _matmul(xn_q_p, xn_q_s, w_gate_t, lb, HEAD_DIM, w_is_t=True)
        u_s, gate_s = u_s_t.T, gate_s_t.T
        o_p = _prompt_attn(u_p, gate_p, kvc_p, vct_p, ks_p, vst_p, kw_p, vwt_p, cos_p, sin_p,
                           g_q[lb:lb + 1], batch, seq)
        qn, sel, ocmp, owin = _decode_front(u_s, kvc_all, state_win2, new_win, cos_s, sin_s,
                                            g_q[lb:lb + 1], bd, t, past)
        selc = sel.reshape(bd, N_KV * t, nbp // bps, bps).transpose(0, 2, 1, 3)
        o_s = _decode_sel(page_table, qn, selc, cache_kv, new_sel, ocmp, owin, u_s, gate_s, e4, bd, t)
        if lb + 1 < N_B:
            xp, xs, [(xn_q_p, xn_q_s)] = _matmul_res_norm(o_p, o_s, w_out_b, lb, xp, xs, [g_b[lb + 1]])
        else:
            xp, xs = _matmul(o_p, o_s, w_out_b, lb, d, res=xp, res_s=xs)

    wn = min(WINDOW, seq)
    y_prompt = xp.reshape(batch, seq, d)
    y_sample = xs.reshape(bd, t, d)
    kv_rows_prompt = rows_p.reshape(batch, seq, 4, N_KV, HEAD_DIM)
    kv_rows_sample = rows_s.reshape(bd, t, 4, N_KV, HEAD_DIM)
    win_prompt = win_p.reshape(batch, seq, 2, N_KV, HEAD_DIM)[:, seq - wn:]
    kvw = jnp.concatenate([state_win, win_s.reshape(bd, t, 2, N_KV, HEAD_DIM)], axis=1)
    win_sample = kvw[:, kvw.shape[1] - min(WINDOW, kvw.shape[1]):]

    return (y_prompt, y_sample, conv_p, conv_s, kv_rows_prompt, kv_rows_sample, win_prompt, win_sample)
```
